```python
import jax, jax.numpy as jnp
from jax import lax
import numpy as np

D_MODEL = 1024
BATCH = 8
SEQ = 2048
DEPTH = 1

LN_EPS = 1e-5
RMS_EPS = 1e-6

GLA_HEADS = 4
GLA_DK = D_MODEL // 2
GLA_DV = D_MODEL
GLA_HK = GLA_DK // GLA_HEADS
GLA_HV = GLA_DV // GLA_HEADS
GLA_GATE_RANK = 16
GLA_TAU = 16.0
GLA_CHUNK = 64

MLA_HEADS = 8
MLA_Q_RANK = 384
MLA_KV_RANK = 256
MLA_NOPE = 128
MLA_ROPE = 64
MLA_V = 128
MLA_QK = MLA_NOPE + MLA_ROPE
ROPE_THETA = 10000.0
Q_BLOCK = 128

D_FF = 4 * D_MODEL

N_BRANCH = 2
DEEPNORM_ALPHA = (2.0 * DEPTH) ** 0.25
DEEPNORM_BETA = (8.0 * DEPTH) ** -0.25

IN_SPLITS = (GLA_DK, GLA_DK, GLA_DV, GLA_DV, GLA_GATE_RANK, MLA_Q_RANK, MLA_KV_RANK + MLA_ROPE, N_BRANCH * D_MODEL)
D_IN = sum(IN_SPLITS)

kernel_name = 'hybrid_gla_mla_deepnorm_block'


def layer_norm(x, g, b):
    xf = x.astype(jnp.float32)
    mu = jnp.mean(xf, axis=-1, keepdims=True)
    var = jnp.mean(jnp.square(xf - mu), axis=-1, keepdims=True)
    return ((xf - mu) * lax.rsqrt(var + LN_EPS) * g + b).astype(x.dtype)


def rms_norm(x, g):
    xf = x.astype(jnp.float32)
    return (xf * lax.rsqrt(jnp.mean(jnp.square(xf), axis=-1, keepdims=True) + RMS_EPS) * g).astype(x.dtype)


def rope_tables(positions):
    inv_freq = 1.0 / (ROPE_THETA ** (jnp.arange(0, MLA_ROPE, 2, dtype=jnp.float32) / MLA_ROPE))
    ang = positions.astype(jnp.float32)[..., None] * inv_freq
    return jnp.cos(ang), jnp.sin(ang)


def apply_rope(x, cos, sin):
    half = x.shape[-1] // 2
    xf = x.astype(jnp.float32)
    x1, x2 = xf[..., :half], xf[..., half:]
    return jnp.concatenate([x1 * cos - x2 * sin, x2 * cos + x1 * sin], axis=-1).astype(x.dtype)


def split_heads(t, n_heads):
    b, s, _ = t.shape
    return t.reshape(b, s, n_heads, -1).transpose(0, 2, 1, 3)


def merge_heads(t):
    b, h, s, d = t.shape
    return t.transpose(0, 2, 1, 3).reshape(b, s, h * d)


def gla_chunked(q, k, v, log_a):
    bsz, nh, seq, dk = q.shape
    dv = v.shape[-1]
    n_chunks = seq // GLA_CHUNK

    def to_chunks(t):
        return t.reshape(bsz, nh, n_chunks, GLA_CHUNK, t.shape[-1]).transpose(2, 0, 1, 3, 4)

    causal = jnp.tril(jnp.ones((GLA_CHUNK, GLA_CHUNK), dtype=bool))

    def step(state, inp):
        qi, ki, vi, gi = inp
        b = jnp.cumsum(gi, axis=2)
        b_last = b[:, :, -1:, :]
        diff = b[:, :, :, None, :] - b[:, :, None, :, :]
        decay = jnp.exp(jnp.where(causal[:, :, None], diff, -jnp.inf))
        scores = jnp.einsum('bhid,bhjd,bhijd->bhij', qi, ki, decay)
        o = jnp.einsum('bhij,bhjv->bhiv', scores, vi) + jnp.einsum('bhid,bhdv->bhiv', qi * jnp.exp(b), state)
        state = jnp.exp(b_last)[:, :, 0, :, None] * state + jnp.einsum('bhjd,bhjv->bhdv', ki * jnp.exp(b_last - b), vi)
        return state, o

    state0 = jnp.zeros((bsz, nh, dk, dv), jnp.float32)
    _, o = lax.scan(step, state0, (to_chunks(q), to_chunks(k), to_chunks(v), to_chunks(log_a)))
    return o.transpose(1, 2, 0, 3, 4).reshape(bsz, nh, seq, dv)


def mla_causal_attention(q_nope, q_rope, k_nope, k_rope, v):
    seq = q_nope.shape[2]
    scale = MLA_QK ** -0.5
    outs = []
    for blk in range(seq // Q_BLOCK):
        q0, q1 = blk * Q_BLOCK, (blk + 1) * Q_BLOCK
        s = (jnp.einsum('bhqd,bhkd->bhqk', q_nope[:, :, q0:q1], k_nope[:, :, :q1])
             + jnp.einsum('bhqr,bkr->bhqk', q_rope[:, :, q0:q1], k_rope[:, :q1]))
        s = s.astype(jnp.float32) * scale
        mask = (q0 + jnp.arange(Q_BLOCK))[:, None] >= jnp.arange(q1)[None, :]
        p = jax.nn.softmax(jnp.where(mask, s, -jnp.inf), axis=-1).astype(v.dtype)
        outs.append(jnp.einsum('bhqk,bhkv->bhqv', p, v[:, :, :q1]))
    return jnp.concatenate(outs, axis=2)


def token_mixer(h, cos, sin, w_in, w_gla_a2, b_gla_a2, gla_norm_g, w_o_gla,
                q_a_norm_g, w_q_b, kv_a_norm_g, w_kv_b, w_o_mla, b_gate, w_out):
    bsz, seq, _ = h.shape
    offsets = np.cumsum(IN_SPLITS)[:-1].tolist()
    q_g, k_g, v_g, r_g, a_lr, q_lat, kv_lat, gate_logits = jnp.split(h @ w_in, offsets, axis=-1)

    log_a = jax.nn.log_sigmoid((a_lr @ w_gla_a2 + b_gla_a2).astype(jnp.float32)) / GLA_TAU
    o_gla = gla_chunked(split_heads(q_g.astype(jnp.float32), GLA_HEADS) * GLA_HK ** -0.5,
                        split_heads(k_g.astype(jnp.float32), GLA_HEADS),
                        split_heads(v_g.astype(jnp.float32), GLA_HEADS),
                        split_heads(log_a, GLA_HEADS))
    o_gla = rms_norm(o_gla.transpose(0, 2, 1, 3), gla_norm_g.reshape(GLA_HEADS, GLA_HV))
    o_gla = o_gla.reshape(bsz, seq, GLA_DV).astype(h.dtype)
    y_gla = (o_gla * jax.nn.silu(r_g)) @ w_o_gla

    q = (rms_norm(q_lat, q_a_norm_g) @ w_q_b).reshape(bsz, seq, MLA_HEADS, MLA_QK)
    q_nope = q[..., :MLA_NOPE]
    q_rope = apply_rope(q[..., MLA_NOPE:], cos[:, :, None], sin[:, :, None])
    c_kv = rms_norm(kv_lat[..., :MLA_KV_RANK], kv_a_norm_g)
    k_rope = apply_rope(kv_lat[..., MLA_KV_RANK:], cos, sin)
    kv = (c_kv @ w_kv_b).reshape(bsz, seq, MLA_HEADS, MLA_NOPE + MLA_V)
    k_nope, v = kv[..., :MLA_NOPE], kv[..., MLA_NOPE:]
    o_mla = mla_causal_attention(q_nope.transpose(0, 2, 1, 3), q_rope.transpose(0, 2, 1, 3),
                                 k_nope.transpose(0, 2, 1, 3), k_rope, v.transpose(0, 2, 1, 3))
    y_mla = merge_heads(o_mla) @ w_o_mla

    g_gla, g_mla = jnp.split(jax.nn.sigmoid(gate_logits + b_gate), N_BRANCH, axis=-1)
    return (g_gla * y_gla + g_mla * y_mla) @ w_out


def setup_inputs(seed: int = 0) -> dict:
    key = jax.random.key(seed)
    ks = jax.random.split(key, 24)
    f32 = jnp.float32

    def normal(k, shape, scale):
        return jax.random.normal(k, shape, f32) * scale

    def gain(k, shape):
        return 1.0 + 0.02 * jax.random.normal(k, shape, f32)

    beta = DEEPNORM_BETA
    x = jax.random.normal(ks[0], (BATCH, SEQ, D_MODEL), f32)
    offset = jax.random.randint(ks[1], (BATCH, 1), 0, 4096, dtype=jnp.int32)
    positions = offset + jnp.arange(SEQ, dtype=jnp.int32)[None, :]
    v_lo = 2 * GLA_DK
    in_col_scale = jnp.ones((D_IN,), f32).at[v_lo:v_lo + GLA_DV].set(beta)
    kv_col_scale = jnp.tile(jnp.concatenate([jnp.ones((MLA_NOPE,), f32), jnp.full((MLA_V,), beta, f32)]), MLA_HEADS)
    return {
        'x': x,
        'positions': positions,
        'ln_in_g': gain(ks[2], (D_MODEL,)),
        'ln_in_b': normal(ks[3], (D_MODEL,), 0.02),
        'w_in': normal(ks[4], (DEPTH, D_MODEL, D_IN), D_MODEL ** -0.5) * in_col_scale,
        'w_gla_a2': normal(ks[5], (DEPTH, GLA_GATE_RANK, GLA_DK), GLA_GATE_RANK ** -0.5),
        'b_gla_a2': normal(ks[6], (DEPTH, GLA_DK), 0.1),
        'gla_norm_g': gain(ks[7], (DEPTH, GLA_DV)),
        'w_o_gla': normal(ks[8], (DEPTH, GLA_DV, D_MODEL), beta * GLA_DV ** -0.5),
        'q_a_norm_g': gain(ks[9], (DEPTH, MLA_Q_RANK)),
        'w_q_b': normal(ks[10], (DEPTH, MLA_Q_RANK, MLA_HEADS * MLA_QK), MLA_Q_RANK ** -0.5),
        'kv_a_norm_g': gain(ks[11], (DEPTH, MLA_KV_RANK)),
        'w_kv_b': normal(ks[12], (DEPTH, MLA_KV_RANK, MLA_HEADS * (MLA_NOPE + MLA_V)), MLA_KV_RANK ** -0.5) * kv_col_scale,
        'w_o_mla': normal(ks[13], (DEPTH, MLA_HEADS * MLA_V, D_MODEL), beta * (MLA_HEADS * MLA_V) ** -0.5),
        'b_gate': normal(ks[14], (DEPTH, N_BRANCH * D_MODEL), 0.1),
        'w_out': normal(ks[15], (DEPTH, D_MODEL, D_MODEL), beta * D_MODEL ** -0.5),
        'ln1_g': gain(ks[16], (DEPTH, D_MODEL)),
        'ln1_b': normal(ks[17], (DEPTH, D_MODEL), 0.02),
        'w_ff1': normal(ks[18], (DEPTH, D_MODEL, D_FF), beta * D_MODEL ** -0.5),
        'w_ff2': normal(ks[19], (DEPTH, D_FF, D_MODEL), beta * D_FF ** -0.5),
        'ln2_g': gain(ks[20], (DEPTH, D_MODEL)),
        'ln2_b': normal(ks[21], (DEPTH, D_MODEL), 0.02),
    }


def reference(x, positions, ln_in_g, ln_in_b, w_in, w_gla_a2, b_gla_a2, gla_norm_g, w_o_gla,
              q_a_norm_g, w_q_b, kv_a_norm_g, w_kv_b, w_o_mla, b_gate, w_out,
              ln1_g, ln1_b, w_ff1, w_ff2, ln2_g, ln2_b):
    cos, sin = rope_tables(positions)
    h = layer_norm(x, ln_in_g, ln_in_b)
    for l in range(DEPTH):
        mix = token_mixer(h, cos, sin, w_in[l], w_gla_a2[l], b_gla_a2[l], gla_norm_g[l], w_o_gla[l],
                          q_a_norm_g[l], w_q_b[l], kv_a_norm_g[l], w_kv_b[l], w_o_mla[l], b_gate[l], w_out[l])
        h = layer_norm(DEEPNORM_ALPHA * h + mix, ln1_g[l], ln1_b[l])
        ff = jnp.square(jax.nn.relu(h @ w_ff1[l])) @ w_ff2[l]
        h = layer_norm(DEEPNORM_ALPHA * h + ff, ln2_g[l], ln2_b[l])
    return h
```

```python
import functools

import jax
import jax.numpy as jnp
from jax import lax
from jax.experimental import pallas as pl
from jax.experimental.pallas import tpu as pltpu

D_MODEL = 1024
DEPTH = 1
LN_EPS = 1e-5
RMS_EPS = 1e-6

GLA_HEADS = 4
GLA_DK = D_MODEL // 2
GLA_DV = D_MODEL
GLA_HK = GLA_DK // GLA_HEADS
GLA_HV = GLA_DV // GLA_HEADS
GLA_GATE_RANK = 16
GLA_TAU = 16.0

MLA_HEADS = 8
MLA_Q_RANK = 384
MLA_KV_RANK = 256
MLA_NOPE = 128
MLA_ROPE = 64
MLA_V = 128
MLA_QK = MLA_NOPE + MLA_ROPE
ROPE_THETA = 10000.0

D_FF = 4 * D_MODEL
DEEPNORM_ALPHA = (2.0 * DEPTH) ** 0.25

_OFF_QG = 0
_OFF_KG = _OFF_QG + GLA_DK
_OFF_VG = _OFF_KG + GLA_DK
_OFF_RG = _OFF_VG + GLA_DV
_OFF_ALR = _OFF_RG + GLA_DV
_OFF_QLAT = _OFF_ALR + GLA_GATE_RANK
_OFF_KVLAT = _OFF_QLAT + MLA_Q_RANK
_OFF_KROPE = _OFF_KVLAT + MLA_KV_RANK
_OFF_GATE = _OFF_KROPE + MLA_ROPE
D_IN = _OFF_GATE + 2 * D_MODEL

LANE = 128
ALR_PAD = LANE
VMEM_LIMIT = 56 * 1024 * 1024

PRE_TM = 256
GLA_CHUNK = 64
MLA_TQ = 256
MLA_TK = 256
POST_TM = 512
FFN_TM = 512
FFN_TF = 1024

_NT = (((1,), (1,)), ((), ()))
_TN = (((0,), (0,)), ((), ()))

BF16 = jnp.bfloat16
F32 = jnp.float32


def _dot(a, b):
    return jnp.dot(a, b, preferred_element_type=F32)


def _layer_norm(x, g, b):
    mu = jnp.mean(x, axis=-1, keepdims=True)
    xc = x - mu
    var = jnp.mean(xc * xc, axis=-1, keepdims=True)
    return xc * lax.rsqrt(var + LN_EPS) * g + b


def _rms_norm(x, g):
    return x * lax.rsqrt(jnp.mean(x * x, axis=-1, keepdims=True) + RMS_EPS) * g


def _sigmoid(x):
    return 1.0 / (1.0 + jnp.exp(-x))


def _const_spec(shape):
    zeros = (0,) * len(shape)
    return pl.BlockSpec(shape, lambda *_: zeros, pipeline_mode=pl.Buffered(1))


def _pre_kernel(x_ref, cos_ref, sin_ref, lng_ref, lnb_ref,
                w_main_ref, w_gate_ref, w_misc_ref, w_alr_ref, w_a2_ref, b_a2_ref,
                b_gate_ref, qn_g_ref, kvn_g_ref, wq_nope_ref, wq_rope_ref, wq_rot_ref,
                wkv_k_ref, wkv_v_ref,
                qg_ref, kg_ref, vg_ref, rg_ref, gate_ref, la_ref, q_ref, k_ref, v_ref):
    h = _layer_norm(x_ref[0], lng_ref[...], lnb_ref[...])
    hb = h.astype(BF16)

    main = _dot(hb, w_main_ref[...])
    qg_ref[0] = (main[:, _OFF_QG:_OFF_KG] * GLA_HK ** -0.5).astype(BF16)
    kg_ref[0] = main[:, _OFF_KG:_OFF_VG].astype(BF16)
    vg_ref[0] = main[:, _OFF_VG:_OFF_RG].astype(BF16)
    r = main[:, _OFF_RG:_OFF_ALR]
    rg_ref[0] = (r * _sigmoid(r)).astype(BF16)

    gate_ref[0] = _sigmoid(_dot(hb, w_gate_ref[...]) + b_gate_ref[...]).astype(BF16)

    alr = _dot(hb, w_alr_ref[...]).astype(BF16)
    z = _dot(alr, w_a2_ref[...]) + b_a2_ref[...]
    log_sig = jnp.minimum(z, 0.0) - jnp.log(1.0 + jnp.exp(-jnp.abs(z)))
    la_ref[0] = log_sig / GLA_TAU

    misc = _dot(hb, w_misc_ref[...])
    c64 = cos_ref[0][:, :MLA_ROPE]
    s64 = sin_ref[0][:, :MLA_ROPE]
    qn = _rms_norm(misc[:, :MLA_Q_RANK], qn_g_ref[...]).astype(BF16)
    ckv = _rms_norm(misc[:, MLA_Q_RANK:MLA_Q_RANK + MLA_KV_RANK], kvn_g_ref[...]).astype(BF16)
    kr0 = MLA_Q_RANK + MLA_KV_RANK
    k_rope = misc[:, kr0:kr0 + MLA_ROPE] * c64 + misc[:, kr0 + MLA_ROPE:kr0 + 2 * MLA_ROPE] * s64
    k_rope = k_rope.astype(BF16)

    scale = MLA_QK ** -0.5
    q_nope = _dot(qn, wq_nope_ref[...]) * scale
    cos_h = jnp.concatenate([cos_ref[0]] * (MLA_HEADS * MLA_ROPE // LANE), axis=-1)
    sin_h = jnp.concatenate([sin_ref[0]] * (MLA_HEADS * MLA_ROPE // LANE), axis=-1)
    q_rope = (_dot(qn, wq_rope_ref[...]) * cos_h + _dot(qn, wq_rot_ref[...]) * sin_h) * scale
    k_nope = _dot(ckv, wkv_k_ref[...])
    v = _dot(ckv, wkv_v_ref[...])
    for hd in range(MLA_HEADS):
        q_ref[0, hd, :, :MLA_NOPE] = q_nope[:, hd * MLA_NOPE:(hd + 1) * MLA_NOPE].astype(BF16)
        q_ref[0, hd, :, MLA_NOPE:] = q_rope[:, hd * MLA_ROPE:(hd + 1) * MLA_ROPE].astype(BF16)
        k_ref[0, hd, :, :MLA_NOPE] = k_nope[:, hd * MLA_NOPE:(hd + 1) * MLA_NOPE].astype(BF16)
        k_ref[0, hd, :, MLA_NOPE:] = k_rope
        v_ref[0, hd] = v[:, hd * MLA_V:(hd + 1) * MLA_V].astype(BF16)


def _pre_call(x, cos_t, sin_t, lng, lnb, w_main, w_gate, w_misc, w_alr, w_a2, b_a2, b_gate,
              qn_g, kvn_g, wq_nope, wq_rope, wq_rot, wkv_k, wkv_v):
    bsz, seq, _ = x.shape
    tm = PRE_TM
    row = lambda width: pl.BlockSpec((1, tm, width), lambda b, i: (b, i, 0))
    head = lambda width: pl.BlockSpec((1, MLA_HEADS, tm, width), lambda b, i: (b, 0, i, 0))
    consts = (lng, lnb, w_main, w_gate, w_misc, w_alr, w_a2, b_a2, b_gate, qn_g, kvn_g,
              wq_nope, wq_rope, wq_rot, wkv_k, wkv_v)
    out_shape = (
        jax.ShapeDtypeStruct((bsz, seq, GLA_DK), BF16),
        jax.ShapeDtypeStruct((bsz, seq, GLA_DK), BF16),
        jax.ShapeDtypeStruct((bsz, seq, GLA_DV), BF16),
        jax.ShapeDtypeStruct((bsz, seq, GLA_DV), BF16),
        jax.ShapeDtypeStruct((bsz, seq, 2 * D_MODEL), BF16),
        jax.ShapeDtypeStruct((bsz, seq, GLA_DK), F32),
        jax.ShapeDtypeStruct((bsz, MLA_HEADS, seq, MLA_QK), BF16),
        jax.ShapeDtypeStruct((bsz, MLA_HEADS, seq, MLA_QK), BF16),
        jax.ShapeDtypeStruct((bsz, MLA_HEADS, seq, MLA_V), BF16),
    )
    out_specs = (row(GLA_DK), row(GLA_DK), row(GLA_DV), row(GLA_DV), row(2 * D_MODEL),
                 row(GLA_DK), head(MLA_QK), head(MLA_QK), head(MLA_V))
    return pl.pallas_call(
        _pre_kernel,
        grid=(bsz, seq // tm),
        in_specs=[row(D_MODEL), row(LANE), row(LANE)] + [_const_spec(c.shape) for c in consts],
        out_specs=out_specs,
        out_shape=out_shape,
        compiler_params=pltpu.CompilerParams(
            dimension_semantics=("parallel", "parallel"), vmem_limit_bytes=VMEM_LIMIT),
        name="pre",
    )(x, cos_t, sin_t, *consts)


def _gla_kernel(q_ref, k_ref, v_ref, r_ref, la_ref, g_ref, o_ref, st_ref):
    st_ref[...] = jnp.zeros_like(st_ref)
    c = GLA_CHUNK
    row = lax.broadcasted_iota(jnp.int32, (c, c), 0)
    col = lax.broadcasted_iota(jnp.int32, (c, c), 1)
    causal = row >= col
    tri = causal.astype(BF16)
    g = g_ref[0]

    def chunk(ci, carry):
        r0 = pl.multiple_of(ci * c, c)
        q = q_ref[0, pl.ds(r0, c), :].astype(F32)
        k = k_ref[0, pl.ds(r0, c), :].astype(F32)
        v = v_ref[0, pl.ds(r0, c), :]
        la = la_ref[0, pl.ds(r0, c), :]
        la_hi = la.astype(BF16)
        la_lo = (la - la_hi.astype(F32)).astype(BF16)
        b = _dot(tri, la_hi) + _dot(tri, la_lo)
        b_last = b[c - 1:c, :]
        qe = (q * jnp.exp(b)).astype(BF16)
        ke = (k * jnp.exp(-b)).astype(BF16)
        kd = (k * jnp.exp(b_last - b)).astype(BF16)
        s = lax.dot_general(qe, ke, _NT, preferred_element_type=F32)
        p = jnp.where(causal, s, 0.0).astype(BF16)
        st = st_ref[...]
        o = _dot(p, v) + lax.dot_general(qe, st.astype(BF16), _NT, preferred_element_type=F32)
        st_ref[...] = st * jnp.exp(b_last) + lax.dot_general(v, kd, _TN, preferred_element_type=F32)
        on = _rms_norm(o, g)
        o_ref[0, pl.ds(r0, c), :] = (on * r_ref[0, pl.ds(r0, c), :].astype(F32)).astype(BF16)
        return carry

    lax.fori_loop(0, q_ref.shape[1] // c, chunk, 0)


def _gla_call(qg, kg, vg, rg, la, g):
    bsz, seq, _ = qg.shape
    kspec = pl.BlockSpec((1, seq, GLA_HK), lambda b, h: (b, 0, h))
    vspec = pl.BlockSpec((1, seq, GLA_HV), lambda b, h: (b, 0, h))
    return pl.pallas_call(
        _gla_kernel,
        grid=(bsz, GLA_HEADS),
        in_specs=[kspec, kspec, vspec, vspec, kspec,
                  pl.BlockSpec((1, 1, GLA_HV), lambda b, h: (h, 0, 0))],
        out_specs=vspec,
        out_shape=jax.ShapeDtypeStruct((bsz, seq, GLA_DV), BF16),
        scratch_shapes=[pltpu.VMEM((GLA_HV, GLA_HK), F32)],
        compiler_params=pltpu.CompilerParams(
            dimension_semantics=("parallel", "parallel"), vmem_limit_bytes=VMEM_LIMIT),
        name="gla",
    )(qg, kg, vg, rg, la, g)


def _mla_kernel(q_ref, k_ref, v_ref, o_ref):
    tq, tk = MLA_TQ, MLA_TK
    qi = pl.program_id(2)
    q = q_ref[0, 0]

    def block(j, carry, masked):
        m, l, acc = carry
        k0 = pl.multiple_of(j * tk, tk)
        k = k_ref[0, 0, pl.ds(k0, tk), :]
        v = v_ref[0, 0, pl.ds(k0, tk), :]
        s = lax.dot_general(q, k, _NT, preferred_element_type=F32)
        if masked:
            row = lax.broadcasted_iota(jnp.int32, (tq, tk), 0)
            col = lax.broadcasted_iota(jnp.int32, (tq, tk), 1)
            s = jnp.where(row >= col, s, -jnp.inf)
        m_new = jnp.maximum(m, jnp.max(s, axis=-1, keepdims=True))
        alpha = jnp.exp(m - m_new)
        p = jnp.exp(s - m_new)
        l = alpha * l + jnp.sum(p, axis=-1, keepdims=True)
        acc = alpha * acc + _dot(p.astype(BF16), v)
        return m_new, l, acc

    init = (jnp.full((tq, 1), -jnp.inf, F32), jnp.zeros((tq, 1), F32),
            jnp.zeros((tq, MLA_V), F32))
    carry = lax.fori_loop(0, qi, functools.partial(block, masked=False), init)
    _, l, acc = block(qi, carry, masked=True)
    o_ref[0] = (acc / l).astype(BF16)


def _mla_call(q, k, v):
    bsz, nh, seq, _ = q.shape
    assert MLA_TQ == MLA_TK
    return pl.pallas_call(
        _mla_kernel,
        grid=(bsz, nh, seq // MLA_TQ),
        in_specs=[pl.BlockSpec((1, 1, MLA_TQ, MLA_QK), lambda b, h, i: (b, h, i, 0)),
                  pl.BlockSpec((1, 1, seq, MLA_QK), lambda b, h, i: (b, h, 0, 0)),
                  pl.BlockSpec((1, 1, seq, MLA_V), lambda b, h, i: (b, h, 0, 0))],
        out_specs=pl.BlockSpec((1, MLA_TQ, MLA_V), lambda b, h, i: (b, i, h)),
        out_shape=jax.ShapeDtypeStruct((bsz, seq, nh * MLA_V), BF16),
        compiler_params=pltpu.CompilerParams(
            dimension_semantics=("parallel", "parallel", "parallel"),
            vmem_limit_bytes=VMEM_LIMIT),
        name="mla",
    )(q, k, v)


def _post_kernel(x_ref, a_ref, m_ref, gate_ref, lng_ref, lnb_ref, wog_ref, wom_ref, wout_ref,
                 ln1g_ref, ln1b_ref, h1_ref):
    y_gla = _dot(a_ref[...], wog_ref[...])
    y_mla = _dot(m_ref[...], wom_ref[...])
    gate = gate_ref[...].astype(F32)
    merged = gate[:, :D_MODEL] * y_gla + gate[:, D_MODEL:] * y_mla
    mix = _dot(merged.astype(BF16), wout_ref[...])
    h = _layer_norm(x_ref[...], lng_ref[...], lnb_ref[...])
    h1_ref[...] = _layer_norm(DEEPNORM_ALPHA * h + mix, ln1g_ref[...], ln1b_ref[...])


def _post_call(x2, act, omla, gate, lng, lnb, wog, wom, wout, ln1g, ln1b):
    n = x2.shape[0]
    tm = POST_TM
    row = lambda width: pl.BlockSpec((tm, width), lambda i: (i, 0))
    consts = (lng, lnb, wog, wom, wout, ln1g, ln1b)
    return pl.pallas_call(
        _post_kernel,
        grid=(n // tm,),
        in_specs=[row(D_MODEL), row(D_MODEL), row(D_MODEL), row(2 * D_MODEL)]
        + [_const_spec(c.shape) for c in consts],
        out_specs=row(D_MODEL),
        out_shape=jax.ShapeDtypeStruct((n, D_MODEL), F32),
        compiler_params=pltpu.CompilerParams(
            dimension_semantics=("parallel",), vmem_limit_bytes=VMEM_LIMIT),
        name="post",
    )(x2, act, omla, gate, *consts)


def _ffn_kernel(h1_ref, w1_ref, w2_ref, g_ref, b_ref, o_ref):
    h1 = h1_ref[...]
    hb = h1.astype(BF16)
    acc = jnp.zeros(h1.shape, F32)
    for f0 in range(0, D_FF, FFN_TF):
        a = jnp.maximum(_dot(hb, w1_ref[:, f0:f0 + FFN_TF]), 0.0)
        acc = acc + _dot((a * a).astype(BF16), w2_ref[f0:f0 + FFN_TF, :])
    o_ref[...] = _layer_norm(DEEPNORM_ALPHA * h1 + acc, g_ref[...], b_ref[...])


def _ffn_call(h1, w1, w2, g, b):
    n = h1.shape[0]
    tm = FFN_TM
    row = pl.BlockSpec((tm, D_MODEL), lambda i: (i, 0))
    consts = (w1, w2, g, b)
    return pl.pallas_call(
        _ffn_kernel,
        grid=(n // tm,),
        in_specs=[row] + [_const_spec(c.shape) for c in consts],
        out_specs=row,
        out_shape=jax.ShapeDtypeStruct((n, D_MODEL), F32),
        compiler_params=pltpu.CompilerParams(
            dimension_semantics=("parallel",), vmem_limit_bytes=VMEM_LIMIT),
        name="ffn",
    )(h1, *consts)


def _rot_cols(w):
    half = w.shape[-1] // 2
    return jnp.concatenate([-w[..., half:], w[..., :half]], axis=-1)


def kernel(x, positions, ln_in_g, ln_in_b, w_in, w_gla_a2, b_gla_a2, gla_norm_g, w_o_gla,
           q_a_norm_g, w_q_b, kv_a_norm_g, w_kv_b, w_o_mla, b_gate, w_out,
           ln1_g, ln1_b, w_ff1, w_ff2, ln2_g, ln2_b):
    assert DEPTH == 1 and w_in.shape[0] == 1
    bsz, seq, _ = x.shape
    n = bsz * seq
    row2 = lambda a: a.reshape(1, -1)

    inv_freq = 1.0 / (ROPE_THETA ** (jnp.arange(0, MLA_ROPE, 2, dtype=F32) / MLA_ROPE))
    ang = positions.astype(F32)[..., None] * inv_freq
    cos_t = jnp.tile(jnp.cos(ang), (1, 1, LANE // (MLA_ROPE // 2)))
    sin_t = jnp.tile(jnp.sin(ang), (1, 1, LANE // (MLA_ROPE // 2)))

    wi = w_in[0]
    k_rope_w = wi[:, _OFF_KROPE:_OFF_GATE]
    w_main = wi[:, :_OFF_ALR].astype(BF16)
    w_gate = wi[:, _OFF_GATE:].astype(BF16)
    w_misc = jnp.concatenate(
        [wi[:, _OFF_QLAT:_OFF_KROPE], k_rope_w, _rot_cols(k_rope_w)], axis=-1).astype(BF16)
    w_alr = jnp.pad(wi[:, _OFF_ALR:_OFF_QLAT], ((0, 0), (0, ALR_PAD - GLA_GATE_RANK))).astype(BF16)
    w_a2 = jnp.pad(w_gla_a2[0], ((0, ALR_PAD - GLA_GATE_RANK), (0, 0))).astype(BF16)

    wq = w_q_b[0].reshape(MLA_Q_RANK, MLA_HEADS, MLA_QK)
    wq_nope = wq[:, :, :MLA_NOPE].reshape(MLA_Q_RANK, -1).astype(BF16)
    wq_rope = wq[:, :, MLA_NOPE:].reshape(MLA_Q_RANK, -1).astype(BF16)
    wq_rot = _rot_cols(wq[:, :, MLA_NOPE:]).reshape(MLA_Q_RANK, -1).astype(BF16)
    wkv = w_kv_b[0].reshape(MLA_KV_RANK, MLA_HEADS, MLA_NOPE + MLA_V)
    wkv_k = wkv[:, :, :MLA_NOPE].reshape(MLA_KV_RANK, -1).astype(BF16)
    wkv_v = wkv[:, :, MLA_NOPE:].reshape(MLA_KV_RANK, -1).astype(BF16)

    qg, kg, vg, rg, gate, la, q, k, v = _pre_call(
        x, cos_t, sin_t, row2(ln_in_g), row2(ln_in_b), w_main, w_gate, w_misc, w_alr, w_a2,
        row2(b_gla_a2[0]), row2(b_gate[0]), row2(q_a_norm_g[0]), row2(kv_a_norm_g[0]),
        wq_nope, wq_rope, wq_rot, wkv_k, wkv_v)

    act = _gla_call(qg, kg, vg, rg, la, gla_norm_g[0].reshape(GLA_HEADS, 1, GLA_HV))
    omla = _mla_call(q, k, v)

    h1 = _post_call(
        x.reshape(n, D_MODEL), act.reshape(n, GLA_DV), omla.reshape(n, MLA_HEADS * MLA_V),
        gate.reshape(n, 2 * D_MODEL), row2(ln_in_g), row2(ln_in_b),
        w_o_gla[0].astype(BF16), w_o_mla[0].astype(BF16), w_out[0].astype(BF16),
        row2(ln1_g[0]), row2(ln1_b[0]))

    out = _ffn_call(h1, w_ff1[0].astype(BF16), w_ff2[0].astype(BF16),
                    row2(ln2_g[0]), row2(ln2_b[0]))
    return out.reshape(bsz, seq, D_MODEL)
```

```python
import jax
import jax.numpy as jnp
from jax import lax
from jax.experimental import pallas as pl
from jax.experimental.pallas import tpu as pltpu

D_MODEL = 1024
DEPTH = 1
LN_EPS = 1e-5
RMS_EPS = 1e-6

GLA_HEADS = 4
GLA_DK = D_MODEL // 2
GLA_DV = D_MODEL
GLA_HK = GLA_DK // GLA_HEADS
GLA_HV = GLA_DV // GLA_HEADS
GLA_GATE_RANK = 16
GLA_TAU = 16.0

MLA_HEADS = 8
MLA_Q_RANK = 384
MLA_KV_RANK = 256
MLA_NOPE = 128
MLA_ROPE = 64
MLA_V = 128
MLA_QK = MLA_NOPE + MLA_ROPE
ROPE_THETA = 10000.0

D_FF = 4 * D_MODEL
DEEPNORM_ALPHA = (2.0 * DEPTH) ** 0.25

_OFF_QG = 0
_OFF_KG = _OFF_QG + GLA_DK
_OFF_VG = _OFF_KG + GLA_DK
_OFF_RG = _OFF_VG + GLA_DV
_OFF_ALR = _OFF_RG + GLA_DV
_OFF_QLAT = _OFF_ALR + GLA_GATE_RANK
_OFF_KVLAT = _OFF_QLAT + MLA_Q_RANK
_OFF_KROPE = _OFF_KVLAT + MLA_KV_RANK
_OFF_GATE = _OFF_KROPE + MLA_ROPE
D_IN = _OFF_GATE + 2 * D_MODEL

LANE = 128
ALR_PAD = LANE
VMEM_LIMIT = 56 * 1024 * 1024

PRE_TM = 256
GLA_CHUNK = 64
MLA_TQ = 256
POST_TM = 512
FFN_TM = 512
FFN_TF = 1024

_NT = (((1,), (1,)), ((), ()))
_TN = (((0,), (0,)), ((), ()))

BF16 = jnp.bfloat16
F32 = jnp.float32


def _dot(a, b):
    return jnp.dot(a, b, preferred_element_type=F32)


def _layer_norm(x, g, b):
    mu = jnp.mean(x, axis=-1, keepdims=True)
    xc = x - mu
    var = jnp.mean(xc * xc, axis=-1, keepdims=True)
    return xc * lax.rsqrt(var + LN_EPS) * g + b


def _rms_norm(x, g):
    return x * lax.rsqrt(jnp.mean(x * x, axis=-1, keepdims=True) + RMS_EPS) * g


def _sigmoid(x):
    return 1.0 / (1.0 + jnp.exp(-x))


def _const_spec(shape):
    zeros = (0,) * len(shape)
    return pl.BlockSpec(shape, lambda *_: zeros, pipeline_mode=pl.Buffered(1))


def _pre_kernel(x_ref, cos_ref, sin_ref, lng_ref, lnb_ref,
                w_main_ref, w_gate_ref, w_misc_ref, w_alr_ref, w_a2_ref, b_a2_ref,
                b_gate_ref, qn_g_ref, kvn_g_ref, wq_nope_ref, wq_rope_ref, wq_rot_ref,
                wkv_k_ref, wkv_v_ref,
                qg_ref, kg_ref, vg_ref, rg_ref, gate_ref, la_ref, q_ref, k_ref, v_ref):
    h = _layer_norm(x_ref[0], lng_ref[...], lnb_ref[...])
    hb = h.astype(BF16)

    main = _dot(hb, w_main_ref[...])
    qg_ref[0] = (main[:, _OFF_QG:_OFF_KG] * GLA_HK ** -0.5).astype(BF16)
    kg_ref[0] = main[:, _OFF_KG:_OFF_VG].astype(BF16)
    vg_ref[0] = main[:, _OFF_VG:_OFF_RG].astype(BF16)
    r = main[:, _OFF_RG:_OFF_ALR]
    rg_ref[0] = (r * _sigmoid(r)).astype(BF16)

    gate_ref[0] = _sigmoid(_dot(hb, w_gate_ref[...]) + b_gate_ref[...]).astype(BF16)

    alr = _dot(hb, w_alr_ref[...]).astype(BF16)
    z = _dot(alr, w_a2_ref[...]) + b_a2_ref[...]
    log_sig = jnp.minimum(z, 0.0) - jnp.log(1.0 + jnp.exp(-jnp.abs(z)))
    la_ref[0] = log_sig / GLA_TAU

    misc = _dot(hb, w_misc_ref[...])
    c64 = cos_ref[0][:, :MLA_ROPE]
    s64 = sin_ref[0][:, :MLA_ROPE]
    qn = _rms_norm(misc[:, :MLA_Q_RANK], qn_g_ref[...]).astype(BF16)
    ckv = _rms_norm(misc[:, MLA_Q_RANK:MLA_Q_RANK + MLA_KV_RANK], kvn_g_ref[...]).astype(BF16)
    kr0 = MLA_Q_RANK + MLA_KV_RANK
    k_rope = misc[:, kr0:kr0 + MLA_ROPE] * c64 + misc[:, kr0 + MLA_ROPE:kr0 + 2 * MLA_ROPE] * s64
    k_rope = k_rope.astype(BF16)

    scale = MLA_QK ** -0.5
    q_nope = _dot(qn, wq_nope_ref[...]) * scale
    cos_h = jnp.concatenate([cos_ref[0]] * (MLA_HEADS * MLA_ROPE // LANE), axis=-1)
    sin_h = jnp.concatenate([sin_ref[0]] * (MLA_HEADS * MLA_ROPE // LANE), axis=-1)
    q_rope = (_dot(qn, wq_rope_ref[...]) * cos_h + _dot(qn, wq_rot_ref[...]) * sin_h) * scale
    k_nope = _dot(ckv, wkv_k_ref[...])
    v = _dot(ckv, wkv_v_ref[...])
    for hd in range(MLA_HEADS):
        q_ref[0, hd, :, :MLA_NOPE] = q_nope[:, hd * MLA_NOPE:(hd + 1) * MLA_NOPE].astype(BF16)
        q_ref[0, hd, :, MLA_NOPE:] = q_rope[:, hd * MLA_ROPE:(hd + 1) * MLA_ROPE].astype(BF16)
        k_ref[0, hd, :, :MLA_NOPE] = k_nope[:, hd * MLA_NOPE:(hd + 1) * MLA_NOPE].astype(BF16)
        k_ref[0, hd, :, MLA_NOPE:] = k_rope
        v_ref[0, hd] = v[:, hd * MLA_V:(hd + 1) * MLA_V].astype(BF16)


def _pre_call(x, cos_t, sin_t, lng, lnb, w_main, w_gate, w_misc, w_alr, w_a2, b_a2, b_gate,
              qn_g, kvn_g, wq_nope, wq_rope, wq_rot, wkv_k, wkv_v):
    bsz, seq, _ = x.shape
    tm = PRE_TM
    row = lambda width: pl.BlockSpec((1, tm, width), lambda b, i: (b, i, 0))
    head = lambda width: pl.BlockSpec((1, MLA_HEADS, tm, width), lambda b, i: (b, 0, i, 0))
    consts = (lng, lnb, w_main, w_gate, w_misc, w_alr, w_a2, b_a2, b_gate, qn_g, kvn_g,
              wq_nope, wq_rope, wq_rot, wkv_k, wkv_v)
    out_shape = (
        jax.ShapeDtypeStruct((bsz, seq, GLA_DK), BF16),
        jax.ShapeDtypeStruct((bsz, seq, GLA_DK), BF16),
        jax.ShapeDtypeStruct((bsz, seq, GLA_DV), BF16),
        jax.ShapeDtypeStruct((bsz, seq, GLA_DV), BF16),
        jax.ShapeDtypeStruct((bsz, seq, 2 * D_MODEL), BF16),
        jax.ShapeDtypeStruct((bsz, seq, GLA_DK), F32),
        jax.ShapeDtypeStruct((bsz, MLA_HEADS, seq, MLA_QK), BF16),
        jax.ShapeDtypeStruct((bsz, MLA_HEADS, seq, MLA_QK), BF16),
        jax.ShapeDtypeStruct((bsz, MLA_HEADS, seq, MLA_V), BF16),
    )
    out_specs = (row(GLA_DK), row(GLA_DK), row(GLA_DV), row(GLA_DV), row(2 * D_MODEL),
                 row(GLA_DK), head(MLA_QK), head(MLA_QK), head(MLA_V))
    return pl.pallas_call(
        _pre_kernel,
        grid=(bsz, seq // tm),
        in_specs=[row(D_MODEL), row(LANE), row(LANE)] + [_const_spec(c.shape) for c in consts],
        out_specs=out_specs,
        out_shape=out_shape,
        compiler_params=pltpu.CompilerParams(
            dimension_semantics=("parallel", "parallel"), vmem_limit_bytes=VMEM_LIMIT),
        name="pre",
    )(x, cos_t, sin_t, *consts)


def _gla_kernel(q_ref, k_ref, v_ref, r_ref, la_ref, g_ref, o_ref, st_ref):
    st_ref[...] = jnp.zeros_like(st_ref)
    c = GLA_CHUNK
    row = lax.broadcasted_iota(jnp.int32, (c, c), 0)
    col = lax.broadcasted_iota(jnp.int32, (c, c), 1)
    causal = row >= col
    tri = causal.astype(BF16)
    g = g_ref[0]

    def chunk(ci, carry):
        r0 = pl.multiple_of(ci * c, c)
        q = q_ref[0, pl.ds(r0, c), :].astype(F32)
        k = k_ref[0, pl.ds(r0, c), :].astype(F32)
        v = v_ref[0, pl.ds(r0, c), :]
        la = la_ref[0, pl.ds(r0, c), :]
        la_hi = la.astype(BF16)
        la_lo = (la - la_hi.astype(F32)).astype(BF16)
        b = _dot(tri, la_hi) + _dot(tri, la_lo)
        b_last = b[c - 1:c, :]
        qe = (q * jnp.exp(b)).astype(BF16)
        ke = (k * jnp.exp(-b)).astype(BF16)
        kd = (k * jnp.exp(b_last - b)).astype(BF16)
        s = lax.dot_general(qe, ke, _NT, preferred_element_type=F32)
        p = jnp.where(causal, s, 0.0).astype(BF16)
        st = st_ref[...]
        o = _dot(p, v) + lax.dot_general(qe, st.astype(BF16), _NT, preferred_element_type=F32)
        st_ref[...] = st * jnp.exp(b_last) + lax.dot_general(v, kd, _TN, preferred_element_type=F32)
        on = _rms_norm(o, g)
        o_ref[0, pl.ds(r0, c), :] = (on * r_ref[0, pl.ds(r0, c), :].astype(F32)).astype(BF16)
        return carry

    lax.fori_loop(0, q_ref.shape[1] // c, chunk, 0)


def _gla_call(qg, kg, vg, rg, la, g):
    bsz, seq, _ = qg.shape
    kspec = pl.BlockSpec((1, seq, GLA_HK), lambda b, h: (b, 0, h))
    vspec = pl.BlockSpec((1, seq, GLA_HV), lambda b, h: (b, 0, h))
    return pl.pallas_call(
        _gla_kernel,
        grid=(bsz, GLA_HEADS),
        in_specs=[kspec, kspec, vspec, vspec, kspec,
                  pl.BlockSpec((1, 1, GLA_HV), lambda b, h: (h, 0, 0))],
        out_specs=vspec,
        out_shape=jax.ShapeDtypeStruct((bsz, seq, GLA_DV), BF16),
        scratch_shapes=[pltpu.VMEM((GLA_HV, GLA_HK), F32)],
        compiler_params=pltpu.CompilerParams(
            dimension_semantics=("parallel", "parallel"), vmem_limit_bytes=VMEM_LIMIT),
        name="gla",
    )(qg, kg, vg, rg, la, g)


def _mla_kernel(q_ref, k_ref, v_ref, o_ref):
    tq = MLA_TQ
    row = lax.broadcasted_iota(jnp.int32, (tq, tq), 0)
    col = lax.broadcasted_iota(jnp.int32, (tq, tq), 1)
    causal = row >= col
    for qi in range(q_ref.shape[2] // tq):
        q0 = qi * tq
        q = q_ref[0, 0, q0:q0 + tq, :]
        s_diag = lax.dot_general(q, k_ref[0, 0, q0:q0 + tq, :], _NT, preferred_element_type=F32)
        s_diag = jnp.where(causal, s_diag, -jnp.inf)
        m = jnp.max(s_diag, axis=-1, keepdims=True)
        if qi > 0:
            s_off = lax.dot_general(q, k_ref[0, 0, :q0, :], _NT, preferred_element_type=F32)
            m = jnp.maximum(m, jnp.max(s_off, axis=-1, keepdims=True))
        p_diag = jnp.exp(s_diag - m)
        l = jnp.sum(p_diag, axis=-1, keepdims=True)
        acc = _dot(p_diag.astype(BF16), v_ref[0, 0, q0:q0 + tq, :])
        if qi > 0:
            p_off = jnp.exp(s_off - m)
            l = l + jnp.sum(p_off, axis=-1, keepdims=True)
            acc = acc + _dot(p_off.astype(BF16), v_ref[0, 0, :q0, :])
        o_ref[0, q0:q0 + tq, :] = (acc / l).astype(BF16)


def _mla_call(q, k, v):
    bsz, nh, seq, _ = q.shape
    head = lambda width: pl.BlockSpec((1, 1, seq, width), lambda b, h: (b, h, 0, 0))
    return pl.pallas_call(
        _mla_kernel,
        grid=(bsz, nh),
        in_specs=[head(MLA_QK), head(MLA_QK), head(MLA_V)],
        out_specs=pl.BlockSpec((1, seq, MLA_V), lambda b, h: (b, 0, h)),
        out_shape=jax.ShapeDtypeStruct((bsz, seq, nh * MLA_V), BF16),
        compiler_params=pltpu.CompilerParams(
            dimension_semantics=("parallel", "parallel"), vmem_limit_bytes=VMEM_LIMIT),
        name="mla",
    )(q, k, v)


def _post_kernel(x_ref, a_ref, m_ref, gate_ref, lng_ref, lnb_ref, wog_ref, wom_ref, wout_ref,
                 ln1g_ref, ln1b_ref, h1_ref):
    y_gla = _dot(a_ref[...], wog_ref[...])
    y_mla = _dot(m_ref[...], wom_ref[...])
    gate = gate_ref[...].astype(F32)
    merged = gate[:, :D_MODEL] * y_gla + gate[:, D_MODEL:] * y_mla
    mix = _dot(merged.astype(BF16), wout_ref[...])
    h = _layer_norm(x_ref[...], lng_ref[...], lnb_ref[...])
    h1_ref[...] = _layer_norm(DEEPNORM_ALPHA * h + mix, ln1g_ref[...], ln1b_ref[...])


def _post_call(x2, act, omla, gate, lng, lnb, wog, wom, wout, ln1g, ln1b):
    n = x2.shape[0]
    tm = POST_TM
    row = lambda width: pl.BlockSpec((tm, width), lambda i: (i, 0))
    consts = (lng, lnb, wog, wom, wout, ln1g, ln1b)
    return pl.pallas_call(
        _post_kernel,
        grid=(n // tm,),
        in_specs=[row(D_MODEL), row(D_MODEL), row(D_MODEL), row(2 * D_MODEL)]
        + [_const_spec(c.shape) for c in consts],
        out_specs=row(D_MODEL),
        out_shape=jax.ShapeDtypeStruct((n, D_MODEL), F32),
        compiler_params=pltpu.CompilerParams(
            dimension_semantics=("parallel",), vmem_limit_bytes=VMEM_LIMIT),
        name="post",
    )(x2, act, omla, gate, *consts)


def _ffn_kernel(h1_ref, w1_ref, w2_ref, g_ref, b_ref, o_ref):
    h1 = h1_ref[...]
    hb = h1.astype(BF16)
    acc = jnp.zeros(h1.shape, F32)
    for f0 in range(0, D_FF, FFN_TF):
        a = jnp.maximum(_dot(hb, w1_ref[:, f0:f0 + FFN_TF]), 0.0)
        acc = acc + _dot((a * a).astype(BF16), w2_ref[f0:f0 + FFN_TF, :])
    o_ref[...] = _layer_norm(DEEPNORM_ALPHA * h1 + acc, g_ref[...], b_ref[...])


def _ffn_call(h1, w1, w2, g, b):
    n = h1.shape[0]
    tm = FFN_TM
    row = pl.BlockSpec((tm, D_MODEL), lambda i: (i, 0))
    consts = (w1, w2, g, b)
    return pl.pallas_call(
        _ffn_kernel,
        grid=(n // tm,),
        in_specs=[row] + [_const_spec(c.shape) for c in consts],
        out_specs=row,
        out_shape=jax.ShapeDtypeStruct((n, D_MODEL), F32),
        compiler_params=pltpu.CompilerParams(
            dimension_semantics=("parallel",), vmem_limit_bytes=VMEM_LIMIT),
        name="ffn",
    )(h1, *consts)


def _rot_cols(w):
    half = w.shape[-1] // 2
    return jnp.concatenate([-w[..., half:], w[..., :half]], axis=-1)


def kernel(x, positions, ln_in_g, ln_in_b, w_in, w_gla_a2, b_gla_a2, gla_norm_g, w_o_gla,
           q_a_norm_g, w_q_b, kv_a_norm_g, w_kv_b, w_o_mla, b_gate, w_out,
           ln1_g, ln1_b, w_ff1, w_ff2, ln2_g, ln2_b):
    assert DEPTH == 1 and w_in.shape[0] == 1
    bsz, seq, _ = x.shape
    n = bsz * seq
    row2 = lambda a: a.reshape(1, -1)

    inv_freq = 1.0 / (ROPE_THETA ** (jnp.arange(0, MLA_ROPE, 2, dtype=F32) / MLA_ROPE))
    ang = positions.astype(F32)[..., None] * inv_freq
    cos_t = jnp.tile(jnp.cos(ang), (1, 1, LANE // (MLA_ROPE // 2)))
    sin_t = jnp.tile(jnp.sin(ang), (1, 1, LANE // (MLA_ROPE // 2)))

    wi = w_in[0]
    k_rope_w = wi[:, _OFF_KROPE:_OFF_GATE]
    w_main = wi[:, :_OFF_ALR].astype(BF16)
    w_gate = wi[:, _OFF_GATE:].astype(BF16)
    w_misc = jnp.concatenate(
        [wi[:, _OFF_QLAT:_OFF_KROPE], k_rope_w, _rot_cols(k_rope_w)], axis=-1).astype(BF16)
    w_alr = jnp.pad(wi[:, _OFF_ALR:_OFF_QLAT], ((0, 0), (0, ALR_PAD - GLA_GATE_RANK))).astype(BF16)
    w_a2 = jnp.pad(w_gla_a2[0], ((0, ALR_PAD - GLA_GATE_RANK), (0, 0))).astype(BF16)

    wq = w_q_b[0].reshape(MLA_Q_RANK, MLA_HEADS, MLA_QK)
    wq_nope = wq[:, :, :MLA_NOPE].reshape(MLA_Q_RANK, -1).astype(BF16)
    wq_rope = wq[:, :, MLA_NOPE:].reshape(MLA_Q_RANK, -1).astype(BF16)
    wq_rot = _rot_cols(wq[:, :, MLA_NOPE:]).reshape(MLA_Q_RANK, -1).astype(BF16)
    wkv = w_kv_b[0].reshape(MLA_KV_RANK, MLA_HEADS, MLA_NOPE + MLA_V)
    wkv_k = wkv[:, :, :MLA_NOPE].reshape(MLA_KV_RANK, -1).astype(BF16)
    wkv_v = wkv[:, :, MLA_NOPE:].reshape(MLA_KV_RANK, -1).astype(BF16)

    qg, kg, vg, rg, gate, la, q, k, v = _pre_call(
        x, cos_t, sin_t, row2(ln_in_g), row2(ln_in_b), w_main, w_gate, w_misc, w_alr, w_a2,
        row2(b_gla_a2[0]), row2(b_gate[0]), row2(q_a_norm_g[0]), row2(kv_a_norm_g[0]),
        wq_nope, wq_rope, wq_rot, wkv_k, wkv_v)

    act = _gla_call(qg, kg, vg, rg, la, gla_norm_g[0].reshape(GLA_HEADS, 1, GLA_HV))
    omla = _mla_call(q, k, v)

    h1 = _post_call(
        x.reshape(n, D_MODEL), act.reshape(n, GLA_DV), omla.reshape(n, MLA_HEADS * MLA_V),
        gate.reshape(n, 2 * D_MODEL), row2(ln_in_g), row2(ln_in_b),
        w_o_gla[0].astype(BF16), w_o_mla[0].astype(BF16), w_out[0].astype(BF16),
        row2(ln1_g[0]), row2(ln1_b[0]))

    out = _ffn_call(h1, w_ff1[0].astype(BF16), w_ff2[0].astype(BF16),
                    row2(ln2_g[0]), row2(ln2_b[0]))
    return out.reshape(bsz, seq, D_MODEL)
```

```python
import jax
import jax.numpy as jnp
from jax import lax
from jax.experimental import pallas as pl
from jax.experimental.pallas import tpu as pltpu

D_MODEL = 1024
DEPTH = 1
LN_EPS = 1e-5
RMS_EPS = 1e-6

GLA_HEADS = 4
GLA_DK = D_MODEL // 2
GLA_DV = D_MODEL
GLA_HK = GLA_DK // GLA_HEADS
GLA_HV = GLA_DV // GLA_HEADS
GLA_GATE_RANK = 16
GLA_TAU = 16.0

MLA_HEADS = 8
MLA_Q_RANK = 384
MLA_KV_RANK = 256
MLA_NOPE = 128
MLA_ROPE = 64
MLA_V = 128
MLA_QK = MLA_NOPE + MLA_ROPE
ROPE_THETA = 10000.0

D_FF = 4 * D_MODEL
DEEPNORM_ALPHA = (2.0 * DEPTH) ** 0.25

_OFF_QG = 0
_OFF_KG = _OFF_QG + GLA_DK
_OFF_VG = _OFF_KG + GLA_DK
_OFF_RG = _OFF_VG + GLA_DV
_OFF_ALR = _OFF_RG + GLA_DV
_OFF_QLAT = _OFF_ALR + GLA_GATE_RANK
_OFF_KVLAT = _OFF_QLAT + MLA_Q_RANK
_OFF_KROPE = _OFF_KVLAT + MLA_KV_RANK
_OFF_GATE = _OFF_KROPE + MLA_ROPE
D_IN = _OFF_GATE + 2 * D_MODEL

LANE = 128
SUBLANE = 8
ALR_PAD = LANE
VMEM_LIMIT = 56 * 1024 * 1024

PRE_TM = 256
GLA_CHUNK = 64
_CHUNK_SHIFT = GLA_CHUNK.bit_length() - 1
assert 1 << _CHUNK_SHIFT == GLA_CHUNK
GLA_TB = 512
MLA_TQ = 256
POST_TM = 512
FFN_TM = 512
FFN_TF = 1024

_NT = (((1,), (1,)), ((), ()))
_TN = (((0,), (0,)), ((), ()))

BF16 = jnp.bfloat16
F32 = jnp.float32


def _dot(a, b):
    return jnp.dot(a, b, preferred_element_type=F32)


def _layer_norm(x, g, b):
    mu = jnp.mean(x, axis=-1, keepdims=True)
    xc = x - mu
    var = jnp.mean(xc * xc, axis=-1, keepdims=True)
    return xc * lax.rsqrt(var + LN_EPS) * g + b


def _rms_norm(x, g):
    return x * lax.rsqrt(jnp.mean(x * x, axis=-1, keepdims=True) + RMS_EPS) * g


def _sigmoid(x):
    return 1.0 / (1.0 + jnp.exp(-x))


def _const_spec(shape):
    zeros = (0,) * len(shape)
    return pl.BlockSpec(shape, lambda *_: zeros, pipeline_mode=pl.Buffered(1))


def _pre_kernel(x_ref, cos_ref, sin_ref, lng_ref, lnb_ref,
                w_main_ref, w_gate_ref, w_misc_ref, w_alr_ref, w_a2_ref, b_a2_ref,
                b_gate_ref, qn_g_ref, kvn_g_ref, wq_nope_ref, wq_rope_ref, wq_rot_ref,
                wkv_k_ref, wkv_v_ref,
                qe_ref, ke_ref, kd_ref, dec_ref, vg_ref, rg_ref, gate_ref, q_ref, k_ref, v_ref):
    h = _layer_norm(x_ref[0], lng_ref[...], lnb_ref[...])
    hb = h.astype(BF16)

    main = _dot(hb, w_main_ref[...])
    vg_ref[0] = main[:, _OFF_VG:_OFF_RG].astype(BF16)
    r = main[:, _OFF_RG:_OFF_ALR]
    rg_ref[0] = (r * _sigmoid(r)).astype(BF16)

    gate_ref[0] = _sigmoid(_dot(hb, w_gate_ref[...]) + b_gate_ref[...]).astype(BF16)

    alr = _dot(hb, w_alr_ref[...]).astype(BF16)
    z = _dot(alr, w_a2_ref[...]) + b_a2_ref[...]
    la = (jnp.minimum(z, 0.0) - jnp.log(1.0 + jnp.exp(-jnp.abs(z)))) / GLA_TAU

    tm = la.shape[0]
    row_chunk = lax.shift_right_logical(lax.broadcasted_iota(jnp.int32, (tm, tm), 0), _CHUNK_SHIFT)
    col_chunk = lax.shift_right_logical(lax.broadcasted_iota(jnp.int32, (tm, tm), 1), _CHUNK_SHIFT)
    same_chunk = row_chunk == col_chunk
    lower = lax.broadcasted_iota(jnp.int32, (tm, tm), 0) >= lax.broadcasted_iota(jnp.int32, (tm, tm), 1)
    tri = (same_chunk & lower).astype(BF16)
    ones = same_chunk.astype(BF16)
    la_hi = la.astype(BF16)
    la_lo = (la - la_hi.astype(F32)).astype(BF16)
    b = _dot(tri, la_hi) + _dot(tri, la_lo)
    bl = _dot(ones, la_hi) + _dot(ones, la_lo)
    qs = main[:, _OFF_QG:_OFF_KG] * GLA_HK ** -0.5
    kk = main[:, _OFF_KG:_OFF_VG]
    qe_ref[0] = (qs * jnp.exp(b)).astype(BF16)
    ke_ref[0] = (kk * jnp.exp(-b)).astype(BF16)
    kd_ref[0] = (kk * jnp.exp(bl - b)).astype(BF16)
    for ci in range(tm // GLA_CHUNK):
        dec_ref[0, ci] = jnp.exp(bl[ci * GLA_CHUNK:ci * GLA_CHUNK + SUBLANE, :])

    misc = _dot(hb, w_misc_ref[...])
    c64 = cos_ref[0][:, :MLA_ROPE]
    s64 = sin_ref[0][:, :MLA_ROPE]
    qn = _rms_norm(misc[:, :MLA_Q_RANK], qn_g_ref[...]).astype(BF16)
    ckv = _rms_norm(misc[:, MLA_Q_RANK:MLA_Q_RANK + MLA_KV_RANK], kvn_g_ref[...]).astype(BF16)
    kr0 = MLA_Q_RANK + MLA_KV_RANK
    k_rope = misc[:, kr0:kr0 + MLA_ROPE] * c64 + misc[:, kr0 + MLA_ROPE:kr0 + 2 * MLA_ROPE] * s64
    k_rope = k_rope.astype(BF16)

    scale = MLA_QK ** -0.5
    q_nope = _dot(qn, wq_nope_ref[...]) * scale
    cos_h = jnp.concatenate([cos_ref[0]] * (MLA_HEADS * MLA_ROPE // LANE), axis=-1)
    sin_h = jnp.concatenate([sin_ref[0]] * (MLA_HEADS * MLA_ROPE // LANE), axis=-1)
    q_rope = (_dot(qn, wq_rope_ref[...]) * cos_h + _dot(qn, wq_rot_ref[...]) * sin_h) * scale
    k_nope = _dot(ckv, wkv_k_ref[...])
    v = _dot(ckv, wkv_v_ref[...])
    for hd in range(MLA_HEADS):
        q_ref[0, hd, :, :MLA_NOPE] = q_nope[:, hd * MLA_NOPE:(hd + 1) * MLA_NOPE].astype(BF16)
        q_ref[0, hd, :, MLA_NOPE:] = q_rope[:, hd * MLA_ROPE:(hd + 1) * MLA_ROPE].astype(BF16)
        k_ref[0, hd, :, :MLA_NOPE] = k_nope[:, hd * MLA_NOPE:(hd + 1) * MLA_NOPE].astype(BF16)
        k_ref[0, hd, :, MLA_NOPE:] = k_rope
        v_ref[0, hd] = v[:, hd * MLA_V:(hd + 1) * MLA_V].astype(BF16)


def _pre_call(x, cos_t, sin_t, lng, lnb, w_main, w_gate, w_misc, w_alr, w_a2, b_a2, b_gate,
              qn_g, kvn_g, wq_nope, wq_rope, wq_rot, wkv_k, wkv_v):
    bsz, seq, _ = x.shape
    tm = PRE_TM
    row = lambda width: pl.BlockSpec((1, tm, width), lambda b, i: (b, i, 0))
    head = lambda width: pl.BlockSpec((1, MLA_HEADS, tm, width), lambda b, i: (b, 0, i, 0))
    consts = (lng, lnb, w_main, w_gate, w_misc, w_alr, w_a2, b_a2, b_gate, qn_g, kvn_g,
              wq_nope, wq_rope, wq_rot, wkv_k, wkv_v)
    out_shape = (
        jax.ShapeDtypeStruct((bsz, seq, GLA_DK), BF16),
        jax.ShapeDtypeStruct((bsz, seq, GLA_DK), BF16),
        jax.ShapeDtypeStruct((bsz, seq, GLA_DK), BF16),
        jax.ShapeDtypeStruct((bsz, seq // GLA_CHUNK, SUBLANE, GLA_DK), F32),
        jax.ShapeDtypeStruct((bsz, seq, GLA_DV), BF16),
        jax.ShapeDtypeStruct((bsz, seq, GLA_DV), BF16),
        jax.ShapeDtypeStruct((bsz, seq, 2 * D_MODEL), BF16),
        jax.ShapeDtypeStruct((bsz, MLA_HEADS, seq, MLA_QK), BF16),
        jax.ShapeDtypeStruct((bsz, MLA_HEADS, seq, MLA_QK), BF16),
        jax.ShapeDtypeStruct((bsz, MLA_HEADS, seq, MLA_V), BF16),
    )
    dec_spec = pl.BlockSpec((1, tm // GLA_CHUNK, SUBLANE, GLA_DK), lambda b, i: (b, i, 0, 0))
    out_specs = (row(GLA_DK), row(GLA_DK), row(GLA_DK), dec_spec, row(GLA_DV), row(GLA_DV),
                 row(2 * D_MODEL), head(MLA_QK), head(MLA_QK), head(MLA_V))
    return pl.pallas_call(
        _pre_kernel,
        grid=(bsz, seq // tm),
        in_specs=[row(D_MODEL), row(LANE), row(LANE)] + [_const_spec(c.shape) for c in consts],
        out_specs=out_specs,
        out_shape=out_shape,
        compiler_params=pltpu.CompilerParams(
            dimension_semantics=("parallel", "parallel"), vmem_limit_bytes=VMEM_LIMIT),
        name="pre",
    )(x, cos_t, sin_t, *consts)


def _gla_kernel(qe_ref, ke_ref, kd_ref, dec_ref, v_ref, r_ref, g_ref, o_ref, st_ref):
    @pl.when(pl.program_id(1) == 0)
    def _():
        st_ref[...] = jnp.zeros_like(st_ref)

    c = GLA_CHUNK
    row = lax.broadcasted_iota(jnp.int32, (c, c), 0)
    col = lax.broadcasted_iota(jnp.int32, (c, c), 1)
    causal = row >= col

    def chunk(ci, carry):
        rows = pl.ds(pl.multiple_of(ci * c, c), c)
        for hd in range(GLA_HEADS):
            kcols = slice(hd * GLA_HK, (hd + 1) * GLA_HK)
            vcols = slice(hd * GLA_HV, (hd + 1) * GLA_HV)
            qe = qe_ref[0, rows, kcols]
            v = v_ref[0, rows, vcols]
            s = lax.dot_general(qe, ke_ref[0, rows, kcols], _NT, preferred_element_type=F32)
            p = jnp.where(causal, s, 0.0).astype(BF16)
            st = st_ref[hd]
            o = _dot(p, v) + lax.dot_general(qe, st.astype(BF16), _NT, preferred_element_type=F32)
            st_ref[hd] = st * dec_ref[0, ci, 0:1, kcols] + lax.dot_general(
                v, kd_ref[0, rows, kcols], _TN, preferred_element_type=F32)
            on = _rms_norm(o, g_ref[:, vcols])
            o_ref[0, rows, vcols] = (on * r_ref[0, rows, vcols].astype(F32)).astype(BF16)
        return carry

    lax.fori_loop(0, qe_ref.shape[1] // c, chunk, 0, unroll=8)


def _gla_call(qe, ke, kd, dec, vg, rg, g):
    bsz, seq, _ = qe.shape
    tb = GLA_TB
    kspec = pl.BlockSpec((1, tb, GLA_DK), lambda b, i: (b, i, 0))
    vspec = pl.BlockSpec((1, tb, GLA_DV), lambda b, i: (b, i, 0))
    dspec = pl.BlockSpec((1, tb // GLA_CHUNK, SUBLANE, GLA_DK), lambda b, i: (b, i, 0, 0))
    return pl.pallas_call(
        _gla_kernel,
        grid=(bsz, seq // tb),
        in_specs=[kspec, kspec, kspec, dspec, vspec, vspec, _const_spec(g.shape)],
        out_specs=vspec,
        out_shape=jax.ShapeDtypeStruct((bsz, seq, GLA_DV), BF16),
        scratch_shapes=[pltpu.VMEM((GLA_HEADS, GLA_HV, GLA_HK), F32)],
        compiler_params=pltpu.CompilerParams(
            dimension_semantics=("parallel", "arbitrary"), vmem_limit_bytes=VMEM_LIMIT),
        name="gla",
    )(qe, ke, kd, dec, vg, rg, g)


def _mla_kernel(q_ref, k_ref, v_ref, o_ref):
    tq = MLA_TQ
    row = lax.broadcasted_iota(jnp.int32, (tq, tq), 0)
    col = lax.broadcasted_iota(jnp.int32, (tq, tq), 1)
    causal = row >= col
    for qi in range(q_ref.shape[2] // tq):
        q0 = qi * tq
        q = q_ref[0, 0, q0:q0 + tq, :]
        s_diag = lax.dot_general(q, k_ref[0, 0, q0:q0 + tq, :], _NT, preferred_element_type=F32)
        s_diag = jnp.where(causal, s_diag, -jnp.inf)
        m = jnp.max(s_diag, axis=-1, keepdims=True)
        if qi > 0:
            s_off = lax.dot_general(q, k_ref[0, 0, :q0, :], _NT, preferred_element_type=F32)
            m = jnp.maximum(m, jnp.max(s_off, axis=-1, keepdims=True))
        p_diag = jnp.exp(s_diag - m)
        l = jnp.sum(p_diag, axis=-1, keepdims=True)
        acc = _dot(p_diag.astype(BF16), v_ref[0, 0, q0:q0 + tq, :])
        if qi > 0:
            p_off = jnp.exp(s_off - m)
            l = l + jnp.sum(p_off, axis=-1, keepdims=True)
            acc = acc + _dot(p_off.astype(BF16), v_ref[0, 0, :q0, :])
        o_ref[0, q0:q0 + tq, :] = (acc / l).astype(BF16)


def _mla_call(q, k, v):
    bsz, nh, seq, _ = q.shape
    head = lambda width: pl.BlockSpec((1, 1, seq, width), lambda b, h: (b, h, 0, 0))
    return pl.pallas_call(
        _mla_kernel,
        grid=(bsz, nh),
        in_specs=[head(MLA_QK), head(MLA_QK), head(MLA_V)],
        out_specs=pl.BlockSpec((1, seq, MLA_V), lambda b, h: (b, 0, h)),
        out_shape=jax.ShapeDtypeStruct((bsz, seq, nh * MLA_V), BF16),
        compiler_params=pltpu.CompilerParams(
            dimension_semantics=("parallel", "parallel"), vmem_limit_bytes=VMEM_LIMIT),
        name="mla",
    )(q, k, v)


def _post_kernel(x_ref, a_ref, m_ref, gate_ref, lng_ref, lnb_ref, wog_ref, wom_ref, wout_ref,
                 ln1g_ref, ln1b_ref, h1_ref):
    y_gla = _dot(a_ref[...], wog_ref[...])
    y_mla = _dot(m_ref[...], wom_ref[...])
    gate = gate_ref[...].astype(F32)
    merged = gate[:, :D_MODEL] * y_gla + gate[:, D_MODEL:] * y_mla
    mix = _dot(merged.astype(BF16), wout_ref[...])
    h = _layer_norm(x_ref[...], lng_ref[...], lnb_ref[...])
    h1_ref[...] = _layer_norm(DEEPNORM_ALPHA * h + mix, ln1g_ref[...], ln1b_ref[...])


def _post_call(x2, act, omla, gate, lng, lnb, wog, wom, wout, ln1g, ln1b):
    n = x2.shape[0]
    tm = POST_TM
    row = lambda width: pl.BlockSpec((tm, width), lambda i: (i, 0))
    consts = (lng, lnb, wog, wom, wout, ln1g, ln1b)
    return pl.pallas_call(
        _post_kernel,
        grid=(n // tm,),
        in_specs=[row(D_MODEL), row(D_MODEL), row(D_MODEL), row(2 * D_MODEL)]
        + [_const_spec(c.shape) for c in consts],
        out_specs=row(D_MODEL),
        out_shape=jax.ShapeDtypeStruct((n, D_MODEL), F32),
        compiler_params=pltpu.CompilerParams(
            dimension_semantics=("parallel",), vmem_limit_bytes=VMEM_LIMIT),
        name="post",
    )(x2, act, omla, gate, *consts)


def _ffn_kernel(h1_ref, w1_ref, w2_ref, g_ref, b_ref, o_ref):
    h1 = h1_ref[...]
    hb = h1.astype(BF16)
    acc = jnp.zeros(h1.shape, F32)
    for f0 in range(0, D_FF, FFN_TF):
        a = jnp.maximum(_dot(hb, w1_ref[:, f0:f0 + FFN_TF]), 0.0)
        acc = acc + _dot((a * a).astype(BF16), w2_ref[f0:f0 + FFN_TF, :])
    o_ref[...] = _layer_norm(DEEPNORM_ALPHA * h1 + acc, g_ref[...], b_ref[...])


def _ffn_call(h1, w1, w2, g, b):
    n = h1.shape[0]
    tm = FFN_TM
    row = pl.BlockSpec((tm, D_MODEL), lambda i: (i, 0))
    consts = (w1, w2, g, b)
    return pl.pallas_call(
        _ffn_kernel,
        grid=(n // tm,),
        in_specs=[row] + [_const_spec(c.shape) for c in consts],
        out_specs=row,
        out_shape=jax.ShapeDtypeStruct((n, D_MODEL), F32),
        compiler_params=pltpu.CompilerParams(
            dimension_semantics=("parallel",), vmem_limit_bytes=VMEM_LIMIT),
        name="ffn",
    )(h1, *consts)


def _rot_cols(w):
    half = w.shape[-1] // 2
    return jnp.concatenate([-w[..., half:], w[..., :half]], axis=-1)


def kernel(x, positions, ln_in_g, ln_in_b, w_in, w_gla_a2, b_gla_a2, gla_norm_g, w_o_gla,
           q_a_norm_g, w_q_b, kv_a_norm_g, w_kv_b, w_o_mla, b_gate, w_out,
           ln1_g, ln1_b, w_ff1, w_ff2, ln2_g, ln2_b):
    assert DEPTH == 1 and w_in.shape[0] == 1
    bsz, seq, _ = x.shape
    n = bsz * seq
    row2 = lambda a: a.reshape(1, -1)

    inv_freq = 1.0 / (ROPE_THETA ** (jnp.arange(0, MLA_ROPE, 2, dtype=F32) / MLA_ROPE))
    ang = positions.astype(F32)[..., None] * inv_freq
    cos_t = jnp.tile(jnp.cos(ang), (1, 1, LANE // (MLA_ROPE // 2)))
    sin_t = jnp.tile(jnp.sin(ang), (1, 1, LANE // (MLA_ROPE // 2)))

    wi = w_in[0]
    k_rope_w = wi[:, _OFF_KROPE:_OFF_GATE]
    w_main = wi[:, :_OFF_ALR].astype(BF16)
    w_gate = wi[:, _OFF_GATE:].astype(BF16)
    w_misc = jnp.concatenate(
        [wi[:, _OFF_QLAT:_OFF_KROPE], k_rope_w, _rot_cols(k_rope_w)], axis=-1).astype(BF16)
    w_alr = jnp.pad(wi[:, _OFF_ALR:_OFF_QLAT], ((0, 0), (0, ALR_PAD - GLA_GATE_RANK))).astype(BF16)
    w_a2 = jnp.pad(w_gla_a2[0], ((0, ALR_PAD - GLA_GATE_RANK), (0, 0))).astype(BF16)

    wq = w_q_b[0].reshape(MLA_Q_RANK, MLA_HEADS, MLA_QK)
    wq_nope = wq[:, :, :MLA_NOPE].reshape(MLA_Q_RANK, -1).astype(BF16)
    wq_rope = wq[:, :, MLA_NOPE:].reshape(MLA_Q_RANK, -1).astype(BF16)
    wq_rot = _rot_cols(wq[:, :, MLA_NOPE:]).reshape(MLA_Q_RANK, -1).astype(BF16)
    wkv = w_kv_b[0].reshape(MLA_KV_RANK, MLA_HEADS, MLA_NOPE + MLA_V)
    wkv_k = wkv[:, :, :MLA_NOPE].reshape(MLA_KV_RANK, -1).astype(BF16)
    wkv_v = wkv[:, :, MLA_NOPE:].reshape(MLA_KV_RANK, -1).astype(BF16)

    qe, ke, kd, dec, vg, rg, gate, q, k, v = _pre_call(
        x, cos_t, sin_t, row2(ln_in_g), row2(ln_in_b), w_main, w_gate, w_misc, w_alr, w_a2,
        row2(b_gla_a2[0]), row2(b_gate[0]), row2(q_a_norm_g[0]), row2(kv_a_norm_g[0]),
        wq_nope, wq_rope, wq_rot, wkv_k, wkv_v)

    act = _gla_call(qe, ke, kd, dec, vg, rg, row2(gla_norm_g[0]))
    omla = _mla_call(q, k, v)

    h1 = _post_call(
        x.reshape(n, D_MODEL), act.reshape(n, GLA_DV), omla.reshape(n, MLA_HEADS * MLA_V),
        gate.reshape(n, 2 * D_MODEL), row2(ln_in_g), row2(ln_in_b),
        w_o_gla[0].astype(BF16), w_o_mla[0].astype(BF16), w_out[0].astype(BF16),
        row2(ln1_g[0]), row2(ln1_b[0]))

    out = _ffn_call(h1, w_ff1[0].astype(BF16), w_ff2[0].astype(BF16),
                    row2(ln2_g[0]), row2(ln2_b[0]))
    return out.reshape(bsz, seq, D_MODEL)
```

```python
import jax
import jax.numpy as jnp
from jax import lax
from jax.experimental import pallas as pl
from jax.experimental.pallas import tpu as pltpu

D_MODEL = 1024
DEPTH = 1
LN_EPS = 1e-5
RMS_EPS = 1e-6

GLA_HEADS = 4
GLA_DK = D_MODEL // 2
GLA_DV = D_MODEL
GLA_HK = GLA_DK // GLA_HEADS
GLA_HV = GLA_DV // GLA_HEADS
GLA_GATE_RANK = 16
GLA_TAU = 16.0

MLA_HEADS = 8
MLA_Q_RANK = 384
MLA_KV_RANK = 256
MLA_NOPE = 128
MLA_ROPE = 64
MLA_V = 128
MLA_QK = MLA_NOPE + MLA_ROPE
ROPE_THETA = 10000.0

LOG2_E = 1.4426950408889634

D_FF = 4 * D_MODEL
DEEPNORM_ALPHA = (2.0 * DEPTH) ** 0.25

_OFF_QG = 0
_OFF_KG = _OFF_QG + GLA_DK
_OFF_VG = _OFF_KG + GLA_DK
_OFF_RG = _OFF_VG + GLA_DV
_OFF_ALR = _OFF_RG + GLA_DV
_OFF_QLAT = _OFF_ALR + GLA_GATE_RANK
_OFF_KVLAT = _OFF_QLAT + MLA_Q_RANK
_OFF_KROPE = _OFF_KVLAT + MLA_KV_RANK
_OFF_GATE = _OFF_KROPE + MLA_ROPE
D_IN = _OFF_GATE + 2 * D_MODEL

LANE = 128
SUBLANE = 8
VMEM_LIMIT = 56 * 1024 * 1024

_SLAB0 = MLA_Q_RANK + MLA_KV_RANK
_SLAB_ALR = MLA_ROPE
assert _SLAB0 % LANE == 0 and _SLAB_ALR + GLA_GATE_RANK <= LANE

PRE_TM = 512
CUMSUM_ROWS = 256
GLA_CHUNK = 64
_CHUNK_SHIFT = GLA_CHUNK.bit_length() - 1
assert 1 << _CHUNK_SHIFT == GLA_CHUNK
GLA_TB = 512
MLA_TQ = 256
MLA_HPS = 4
POST_TM = 512
FFN_TM = 512
FFN_TF = 1024

_NT = (((1,), (1,)), ((), ()))
_TN = (((0,), (0,)), ((), ()))

BF16 = jnp.bfloat16
F32 = jnp.float32


def _dot(a, b):
    return jnp.dot(a, b, preferred_element_type=F32)


def _layer_norm(x, g, b):
    mu = jnp.mean(x, axis=-1, keepdims=True)
    xc = x - mu
    var = jnp.mean(xc * xc, axis=-1, keepdims=True)
    return xc * lax.rsqrt(var + LN_EPS) * g + b


def _rms_norm(x, g):
    return x * lax.rsqrt(jnp.mean(x * x, axis=-1, keepdims=True) + RMS_EPS) * g


def _sigmoid(x):
    return 1.0 / (1.0 + jnp.exp(-x))


def _rope_rot(x):
    width = x.shape[-1]
    half = MLA_ROPE // 2
    lane = lax.broadcasted_iota(jnp.int32, x.shape, 1)
    first_half = (lane & (MLA_ROPE - 1)) < half
    ahead = pltpu.roll(x, width - half, 1)
    behind = pltpu.roll(x, half, 1)
    return jnp.where(first_half, -ahead, behind)


def _const_spec(shape):
    zeros = (0,) * len(shape)
    return pl.BlockSpec(shape, lambda *_: zeros, pipeline_mode=pl.Buffered(1))


def _pre_a_kernel(x_ref, lng_ref, lnb_ref, w_vr_ref, w_gate_ref, b_gate_ref,
                  vg_ref, rg_ref, gate_ref):
    hb = _layer_norm(x_ref[0], lng_ref[...], lnb_ref[...]).astype(BF16)
    vr = _dot(hb, w_vr_ref[...])
    vg_ref[0] = vr[:, :GLA_DV].astype(BF16)
    r = vr[:, GLA_DV:]
    rg_ref[0] = (r * _sigmoid(r)).astype(BF16)
    gate_ref[0] = _sigmoid(_dot(hb, w_gate_ref[...]) + b_gate_ref[...]).astype(BF16)


def _pre_a_call(x, lng, lnb, w_vr, w_gate, b_gate):
    bsz, seq, _ = x.shape
    tm = PRE_TM
    row = lambda width: pl.BlockSpec((1, tm, width), lambda b, i: (b, i, 0))
    consts = (lng, lnb, w_vr, w_gate, b_gate)
    return pl.pallas_call(
        _pre_a_kernel,
        grid=(bsz, seq // tm),
        in_specs=[row(D_MODEL)] + [_const_spec(c.shape) for c in consts],
        out_specs=(row(GLA_DV), row(GLA_DV), row(2 * D_MODEL)),
        out_shape=(jax.ShapeDtypeStruct((bsz, seq, GLA_DV), BF16),
                   jax.ShapeDtypeStruct((bsz, seq, GLA_DV), BF16),
                   jax.ShapeDtypeStruct((bsz, seq, 2 * D_MODEL), BF16)),
        compiler_params=pltpu.CompilerParams(
            dimension_semantics=("parallel", "parallel"), vmem_limit_bytes=VMEM_LIMIT),
        name="pre_a",
    )(x, *consts)


def _pre_b_kernel(x_ref, cos_ref, sin_ref, lng_ref, lnb_ref, w_qk_ref, w_lat_ref, w_a2_ref,
                  b_a2_ref, qn_g_ref, kvn_g_ref, wq_nope_ref, wq_rope_ref, wkv_k_ref, wkv_v_ref,
                  qe_ref, ke_ref, kd_ref, dec_ref, q_ref, k_ref, v_ref):
    hb = _layer_norm(x_ref[0], lng_ref[...], lnb_ref[...]).astype(BF16)
    tm = hb.shape[0]
    qk = _dot(hb, w_qk_ref[...])
    lat = _dot(hb, w_lat_ref[...])
    slab = lat[:, _SLAB0:]

    z = _dot(slab.astype(BF16), w_a2_ref[...]) + b_a2_ref[...]
    la = (jnp.minimum(z, 0.0) - jnp.log(1.0 + jnp.exp(-jnp.abs(z)))) / GLA_TAU

    n = CUMSUM_ROWS
    row_i = lax.broadcasted_iota(jnp.int32, (n, n), 0)
    col_i = lax.broadcasted_iota(jnp.int32, (n, n), 1)
    same_chunk = (lax.shift_right_logical(row_i, _CHUNK_SHIFT)
                  == lax.shift_right_logical(col_i, _CHUNK_SHIFT))
    tri = (same_chunk & (row_i >= col_i)).astype(BF16)
    la_hi = la.astype(BF16)
    la_lo = (la - la_hi.astype(F32)).astype(BF16)
    b = jnp.concatenate(
        [_dot(tri, la_hi[r0:r0 + n]) + _dot(tri, la_lo[r0:r0 + n]) for r0 in range(0, tm, n)],
        axis=0)
    chunk_ends = range(GLA_CHUNK, tm + 1, GLA_CHUNK)
    bl = jnp.concatenate(
        [jnp.broadcast_to(b[e - 1:e], (GLA_CHUNK, GLA_DK)) for e in chunk_ends], axis=0)
    qs = qk[:, :GLA_DK] * GLA_HK ** -0.5
    kk = qk[:, GLA_DK:]
    qe_ref[0] = (qs * jnp.exp(b)).astype(BF16)
    ke_ref[0] = (kk * jnp.exp(-b)).astype(BF16)
    kd_ref[0] = (kk * jnp.exp(bl - b)).astype(BF16)
    for ci, e in enumerate(chunk_ends):
        dec_ref[0, ci] = jnp.exp(jnp.broadcast_to(b[e - 1:e], (SUBLANE, GLA_DK)))

    qn = _rms_norm(lat[:, :MLA_Q_RANK], qn_g_ref[...]).astype(BF16)
    ckv = _rms_norm(lat[:, MLA_Q_RANK:_SLAB0], kvn_g_ref[...]).astype(BF16)
    cos_t = cos_ref[0]
    sin_t = sin_ref[0]
    k_rope = (slab * cos_t + _rope_rot(slab) * sin_t)[:, :MLA_ROPE].astype(BF16)

    scale = MLA_QK ** -0.5 * LOG2_E
    q_nope = _dot(qn, wq_nope_ref[...]) * scale
    q_rope = _dot(qn, wq_rope_ref[...])
    reps = q_rope.shape[-1] // LANE
    q_rope = (q_rope * jnp.concatenate([cos_t] * reps, axis=-1)
              + _rope_rot(q_rope) * jnp.concatenate([sin_t] * reps, axis=-1)) * scale
    k_nope = _dot(ckv, wkv_k_ref[...])
    v = _dot(ckv, wkv_v_ref[...])
    for hd in range(MLA_HEADS):
        q_ref[0, hd, :, :MLA_NOPE] = q_nope[:, hd * MLA_NOPE:(hd + 1) * MLA_NOPE].astype(BF16)
        q_ref[0, hd, :, MLA_NOPE:] = q_rope[:, hd * MLA_ROPE:(hd + 1) * MLA_ROPE].astype(BF16)
        k_ref[0, hd, :, :MLA_NOPE] = k_nope[:, hd * MLA_NOPE:(hd + 1) * MLA_NOPE].astype(BF16)
        k_ref[0, hd, :, MLA_NOPE:] = k_rope
        v_ref[0, hd] = v[:, hd * MLA_V:(hd + 1) * MLA_V].astype(BF16)


def _pre_b_call(x, cos_t, sin_t, lng, lnb, w_qk, w_lat, w_a2, b_a2, qn_g, kvn_g,
                wq_nope, wq_rope, wkv_k, wkv_v):
    bsz, seq, _ = x.shape
    tm = PRE_TM
    assert tm % CUMSUM_ROWS == 0 and CUMSUM_ROWS % GLA_CHUNK == 0
    row = lambda width: pl.BlockSpec((1, tm, width), lambda b, i: (b, i, 0))
    head = lambda width: pl.BlockSpec((1, MLA_HEADS, tm, width), lambda b, i: (b, 0, i, 0))
    consts = (lng, lnb, w_qk, w_lat, w_a2, b_a2, qn_g, kvn_g, wq_nope, wq_rope, wkv_k, wkv_v)
    out_shape = (
        jax.ShapeDtypeStruct((bsz, seq, GLA_DK), BF16),
        jax.ShapeDtypeStruct((bsz, seq, GLA_DK), BF16),
        jax.ShapeDtypeStruct((bsz, seq, GLA_DK), BF16),
        jax.ShapeDtypeStruct((bsz, seq // GLA_CHUNK, SUBLANE, GLA_DK), F32),
        jax.ShapeDtypeStruct((bsz, MLA_HEADS, seq, MLA_QK), BF16),
        jax.ShapeDtypeStruct((bsz, MLA_HEADS, seq, MLA_QK), BF16),
        jax.ShapeDtypeStruct((bsz, MLA_HEADS, seq, MLA_V), BF16),
    )
    dec_spec = pl.BlockSpec((1, tm // GLA_CHUNK, SUBLANE, GLA_DK), lambda b, i: (b, i, 0, 0))
    out_specs = (row(GLA_DK), row(GLA_DK), row(GLA_DK), dec_spec,
                 head(MLA_QK), head(MLA_QK), head(MLA_V))
    return pl.pallas_call(
        _pre_b_kernel,
        grid=(bsz, seq // tm),
        in_specs=[row(D_MODEL), row(LANE), row(LANE)] + [_const_spec(c.shape) for c in consts],
        out_specs=out_specs,
        out_shape=out_shape,
        compiler_params=pltpu.CompilerParams(
            dimension_semantics=("parallel", "parallel"), vmem_limit_bytes=VMEM_LIMIT),
        name="pre_b",
    )(x, cos_t, sin_t, *consts)


def _gla_kernel(qe_ref, ke_ref, kd_ref, dec_ref, v_ref, r_ref, g_ref, o_ref, st_ref):
    @pl.when(pl.program_id(1) == 0)
    def _():
        st_ref[...] = jnp.zeros_like(st_ref)

    c = GLA_CHUNK
    row = lax.broadcasted_iota(jnp.int32, (c, c), 0)
    col = lax.broadcasted_iota(jnp.int32, (c, c), 1)
    causal = row >= col

    def chunk(ci, carry):
        rows = pl.ds(pl.multiple_of(ci * c, c), c)
        for hd in range(GLA_HEADS):
            kcols = slice(hd * GLA_HK, (hd + 1) * GLA_HK)
            vcols = slice(hd * GLA_HV, (hd + 1) * GLA_HV)
            qe = qe_ref[0, rows, kcols]
            v = v_ref[0, rows, vcols]
            s = lax.dot_general(qe, ke_ref[0, rows, kcols], _NT, preferred_element_type=F32)
            p = jnp.where(causal, s, 0.0).astype(BF16)
            st = st_ref[hd]
            o = _dot(p, v) + lax.dot_general(qe, st.astype(BF16), _NT, preferred_element_type=F32)
            st_ref[hd] = st * dec_ref[0, ci, 0:1, kcols] + lax.dot_general(
                v, kd_ref[0, rows, kcols], _TN, preferred_element_type=F32)
            on = _rms_norm(o, g_ref[:, vcols])
            o_ref[0, rows, vcols] = (on * r_ref[0, rows, vcols].astype(F32)).astype(BF16)
        return carry

    n_chunks = qe_ref.shape[1] // c
    lax.fori_loop(0, n_chunks, chunk, 0, unroll=n_chunks)


def _gla_call(qe, ke, kd, dec, vg, rg, g):
    bsz, seq, _ = qe.shape
    tb = GLA_TB
    kspec = pl.BlockSpec((1, tb, GLA_DK), lambda b, i: (b, i, 0))
    vspec = pl.BlockSpec((1, tb, GLA_DV), lambda b, i: (b, i, 0))
    dspec = pl.BlockSpec((1, tb // GLA_CHUNK, SUBLANE, GLA_DK), lambda b, i: (b, i, 0, 0))
    return pl.pallas_call(
        _gla_kernel,
        grid=(bsz, seq // tb),
        in_specs=[kspec, kspec, kspec, dspec, vspec, vspec, _const_spec(g.shape)],
        out_specs=vspec,
        out_shape=jax.ShapeDtypeStruct((bsz, seq, GLA_DV), BF16),
        scratch_shapes=[pltpu.VMEM((GLA_HEADS, GLA_HV, GLA_HK), F32)],
        compiler_params=pltpu.CompilerParams(
            dimension_semantics=("parallel", "arbitrary"), vmem_limit_bytes=VMEM_LIMIT),
        name="gla",
    )(qe, ke, kd, dec, vg, rg, g)


def _mla_kernel(q_ref, k_ref, v_ref, o_ref, vx_ref):
    tq = MLA_TQ
    seq = v_ref.shape[2]
    ones_col = lax.broadcasted_iota(jnp.int32, (seq, MLA_V), 1) == 0
    for hd in range(MLA_HPS):
        vx_ref[hd, :, :MLA_V] = v_ref[0, hd]
        vx_ref[hd, :, MLA_V:] = jnp.where(ones_col, 1.0, 0.0).astype(BF16)

    row = lax.broadcasted_iota(jnp.int32, (tq, tq), 0)
    col = lax.broadcasted_iota(jnp.int32, (tq, tq), 1)
    causal = row >= col
    for qi in range(seq // tq):
        q0 = qi * tq
        for hd in range(MLA_HPS):
            q = q_ref[0, hd, q0:q0 + tq, :]
            s_diag = lax.dot_general(q, k_ref[0, hd, q0:q0 + tq, :], _NT,
                                     preferred_element_type=F32)
            s_diag = jnp.where(causal, s_diag, -jnp.inf)
            m = jnp.max(s_diag, axis=-1, keepdims=True)
            if qi > 0:
                s_off = lax.dot_general(q, k_ref[0, hd, :q0, :], _NT, preferred_element_type=F32)
                m = jnp.maximum(m, jnp.max(s_off, axis=-1, keepdims=True))
            acc = _dot(jnp.exp2(s_diag - m).astype(BF16), vx_ref[hd, q0:q0 + tq, :])
            if qi > 0:
                acc = acc + _dot(jnp.exp2(s_off - m).astype(BF16), vx_ref[hd, :q0, :])
            o_ref[0, q0:q0 + tq, hd * MLA_V:(hd + 1) * MLA_V] = (
                acc[:, :MLA_V] / acc[:, MLA_V:MLA_V + 1]).astype(BF16)


def _mla_call(q, k, v):
    bsz, nh, seq, _ = q.shape
    hps = MLA_HPS
    head = lambda width: pl.BlockSpec((1, hps, seq, width), lambda b, h: (b, h, 0, 0))
    return pl.pallas_call(
        _mla_kernel,
        grid=(bsz, nh // hps),
        in_specs=[head(MLA_QK), head(MLA_QK), head(MLA_V)],
        out_specs=pl.BlockSpec((1, seq, hps * MLA_V), lambda b, h: (b, 0, h)),
        out_shape=jax.ShapeDtypeStruct((bsz, seq, nh * MLA_V), BF16),
        scratch_shapes=[pltpu.VMEM((hps, seq, 2 * MLA_V), BF16)],
        compiler_params=pltpu.CompilerParams(
            dimension_semantics=("parallel", "parallel"), vmem_limit_bytes=VMEM_LIMIT),
        name="mla",
    )(q, k, v)


def _post_kernel(x_ref, a_ref, m_ref, gate_ref, lng_ref, lnb_ref, wog_ref, wom_ref, wout_ref,
                 ln1g_ref, ln1b_ref, h1_ref):
    y_gla = _dot(a_ref[...], wog_ref[...])
    y_mla = _dot(m_ref[...], wom_ref[...])
    gate = gate_ref[...].astype(F32)
    merged = gate[:, :D_MODEL] * y_gla + gate[:, D_MODEL:] * y_mla
    mix = _dot(merged.astype(BF16), wout_ref[...])
    h = _layer_norm(x_ref[...], lng_ref[...], lnb_ref[...])
    h1_ref[...] = _layer_norm(DEEPNORM_ALPHA * h + mix, ln1g_ref[...], ln1b_ref[...])


def _post_call(x2, act, omla, gate, lng, lnb, wog, wom, wout, ln1g, ln1b):
    n = x2.shape[0]
    tm = POST_TM
    row = lambda width: pl.BlockSpec((tm, width), lambda i: (i, 0))
    consts = (lng, lnb, wog, wom, wout, ln1g, ln1b)
    return pl.pallas_call(
        _post_kernel,
        grid=(n // tm,),
        in_specs=[row(D_MODEL), row(D_MODEL), row(D_MODEL), row(2 * D_MODEL)]
        + [_const_spec(c.shape) for c in consts],
        out_specs=row(D_MODEL),
        out_shape=jax.ShapeDtypeStruct((n, D_MODEL), F32),
        compiler_params=pltpu.CompilerParams(
            dimension_semantics=("parallel",), vmem_limit_bytes=VMEM_LIMIT),
        name="post",
    )(x2, act, omla, gate, *consts)


def _ffn_kernel(h1_ref, w1_ref, w2_ref, g_ref, b_ref, o_ref):
    h1 = h1_ref[...]
    hb = h1.astype(BF16)
    acc = jnp.zeros(h1.shape, F32)
    for f0 in range(0, D_FF, FFN_TF):
        a = jnp.maximum(_dot(hb, w1_ref[:, f0:f0 + FFN_TF]), 0.0)
        acc = acc + _dot((a * a).astype(BF16), w2_ref[f0:f0 + FFN_TF, :])
    o_ref[...] = _layer_norm(DEEPNORM_ALPHA * h1 + acc, g_ref[...], b_ref[...])


def _ffn_call(h1, w1, w2, g, b):
    n = h1.shape[0]
    tm = FFN_TM
    row = pl.BlockSpec((tm, D_MODEL), lambda i: (i, 0))
    consts = (w1, w2, g, b)
    return pl.pallas_call(
        _ffn_kernel,
        grid=(n // tm,),
        in_specs=[row] + [_const_spec(c.shape) for c in consts],
        out_specs=row,
        out_shape=jax.ShapeDtypeStruct((n, D_MODEL), F32),
        compiler_params=pltpu.CompilerParams(
            dimension_semantics=("parallel",), vmem_limit_bytes=VMEM_LIMIT),
        name="ffn",
    )(h1, *consts)


def kernel(x, positions, ln_in_g, ln_in_b, w_in, w_gla_a2, b_gla_a2, gla_norm_g, w_o_gla,
           q_a_norm_g, w_q_b, kv_a_norm_g, w_kv_b, w_o_mla, b_gate, w_out,
           ln1_g, ln1_b, w_ff1, w_ff2, ln2_g, ln2_b):
    assert DEPTH == 1 and w_in.shape[0] == 1
    bsz, seq, _ = x.shape
    n = bsz * seq
    row2 = lambda a: a.reshape(1, -1)

    inv_freq = 1.0 / (ROPE_THETA ** (jnp.arange(0, MLA_ROPE, 2, dtype=F32) / MLA_ROPE))
    ang = positions.astype(F32)[..., None] * inv_freq
    cos_t = jnp.tile(jnp.cos(ang), (1, 1, LANE // (MLA_ROPE // 2)))
    sin_t = jnp.tile(jnp.sin(ang), (1, 1, LANE // (MLA_ROPE // 2)))

    wi = w_in[0]
    w_qk = wi[:, _OFF_QG:_OFF_VG].astype(BF16)
    w_vr = wi[:, _OFF_VG:_OFF_ALR].astype(BF16)
    w_gate = wi[:, _OFF_GATE:].astype(BF16)
    slab_pad = LANE - MLA_ROPE - GLA_GATE_RANK
    w_lat = jnp.concatenate(
        [wi[:, _OFF_QLAT:_OFF_GATE], wi[:, _OFF_ALR:_OFF_QLAT],
         jnp.zeros((D_MODEL, slab_pad), F32)], axis=-1).astype(BF16)
    w_a2 = jnp.pad(w_gla_a2[0], ((_SLAB_ALR, slab_pad), (0, 0))).astype(BF16)

    wq = w_q_b[0].reshape(MLA_Q_RANK, MLA_HEADS, MLA_QK)
    wq_nope = wq[:, :, :MLA_NOPE].reshape(MLA_Q_RANK, -1).astype(BF16)
    wq_rope = wq[:, :, MLA_NOPE:].reshape(MLA_Q_RANK, -1).astype(BF16)
    wkv = w_kv_b[0].reshape(MLA_KV_RANK, MLA_HEADS, MLA_NOPE + MLA_V)
    wkv_k = wkv[:, :, :MLA_NOPE].reshape(MLA_KV_RANK, -1).astype(BF16)
    wkv_v = wkv[:, :, MLA_NOPE:].reshape(MLA_KV_RANK, -1).astype(BF16)

    lng, lnb = row2(ln_in_g), row2(ln_in_b)
    vg, rg, gate = _pre_a_call(x, lng, lnb, w_vr, w_gate, row2(b_gate[0]))
    qe, ke, kd, dec, q, k, v = _pre_b_call(
        x, cos_t, sin_t, lng, lnb, w_qk, w_lat, w_a2, row2(b_gla_a2[0]),
        row2(q_a_norm_g[0]), row2(kv_a_norm_g[0]), wq_nope, wq_rope, wkv_k, wkv_v)

    act = _gla_call(qe, ke, kd, dec, vg, rg, row2(gla_norm_g[0]))
    omla = _mla_call(q, k, v)

    h1 = _post_call(
        x.reshape(n, D_MODEL), act.reshape(n, GLA_DV), omla.reshape(n, MLA_HEADS * MLA_V),
        gate.reshape(n, 2 * D_MODEL), lng, lnb,
        w_o_gla[0].astype(BF16), w_o_mla[0].astype(BF16), w_out[0].astype(BF16),
        row2(ln1_g[0]), row2(ln1_b[0]))

    out = _ffn_call(h1, w_ff1[0].astype(BF16), w_ff2[0].astype(BF16),
                    row2(ln2_g[0]), row2(ln2_b[0]))
    return out.reshape(bsz, seq, D_MODEL)
```

```python
import jax
import jax.numpy as jnp
from jax import lax
from jax.experimental import pallas as pl
from jax.experimental.pallas import tpu as pltpu

D_MODEL = 1024
DEPTH = 1
LN_EPS = 1e-5
RMS_EPS = 1e-6

GLA_HEADS = 4
GLA_DK = D_MODEL // 2
GLA_DV = D_MODEL
GLA_HK = GLA_DK // GLA_HEADS
GLA_HV = GLA_DV // GLA_HEADS
GLA_GATE_RANK = 16
GLA_TAU = 16.0

MLA_HEADS = 8
MLA_Q_RANK = 384
MLA_KV_RANK = 256
MLA_NOPE = 128
MLA_ROPE = 64
MLA_V = 128
MLA_QK = MLA_NOPE + MLA_ROPE
ROPE_THETA = 10000.0

LOG2_E = 1.4426950408889634

D_FF = 4 * D_MODEL
DEEPNORM_ALPHA = (2.0 * DEPTH) ** 0.25

_OFF_QG = 0
_OFF_KG = _OFF_QG + GLA_DK
_OFF_VG = _OFF_KG + GLA_DK
_OFF_RG = _OFF_VG + GLA_DV
_OFF_ALR = _OFF_RG + GLA_DV
_OFF_QLAT = _OFF_ALR + GLA_GATE_RANK
_OFF_KVLAT = _OFF_QLAT + MLA_Q_RANK
_OFF_KROPE = _OFF_KVLAT + MLA_KV_RANK
_OFF_GATE = _OFF_KROPE + MLA_ROPE
D_IN = _OFF_GATE + 2 * D_MODEL

LANE = 128
SUBLANE = 8
VMEM_LIMIT = 56 * 1024 * 1024

_SLAB0 = MLA_Q_RANK + MLA_KV_RANK
_SLAB_ALR = MLA_ROPE
assert _SLAB0 % LANE == 0 and _SLAB_ALR + GLA_GATE_RANK <= LANE
_LAT_W = _SLAB0 + LANE
_PACK_A = 2 * GLA_DV + 2 * D_MODEL
_PACK_B = _PACK_A // 2
assert 2 * GLA_DK + _LAT_W <= _PACK_B

PRE_A_TM = 1024
PRE_B_TM = 1024
CUMSUM_ROWS = 256
GLA_CHUNK = 64
_CHUNK_SHIFT = GLA_CHUNK.bit_length() - 1
assert 1 << _CHUNK_SHIFT == GLA_CHUNK
GLA_TB = 512
MLA_TQ = 256
MLA_HPS = 4
POST_TM = 1024
FFN_TM = 1024
FFN_TF = 1024

_NT = (((1,), (1,)), ((), ()))
_TN = (((0,), (0,)), ((), ()))

BF16 = jnp.bfloat16
F32 = jnp.float32


def _dot(a, b):
    return jnp.dot(a, b, preferred_element_type=F32)


def _layer_norm(x, g, b):
    mu = jnp.mean(x, axis=-1, keepdims=True)
    xc = x - mu
    var = jnp.mean(xc * xc, axis=-1, keepdims=True)
    return xc * lax.rsqrt(var + LN_EPS) * g + b


def _rms_norm(x, g):
    return x * lax.rsqrt(jnp.mean(x * x, axis=-1, keepdims=True) + RMS_EPS) * g


def _sigmoid(x):
    return 1.0 / (1.0 + jnp.exp(-x))


def _rope_rot(x):
    width = x.shape[-1]
    half = MLA_ROPE // 2
    lane = lax.broadcasted_iota(jnp.int32, x.shape, 1)
    first_half = (lane & (MLA_ROPE - 1)) < half
    ahead = pltpu.roll(x, width - half, 1)
    behind = pltpu.roll(x, half, 1)
    return jnp.where(first_half, -ahead, behind)


def _const_spec(shape, index=None):
    index = (0,) * len(shape) if index is None else index
    return pl.BlockSpec(shape, lambda *_: index, pipeline_mode=pl.Buffered(1))


def _pre_a_kernel(x_ref, lng_ref, lnb_ref, b_gate_ref, w_ref, vg_ref, rg_ref, gate_ref):
    hb = _layer_norm(x_ref[0], lng_ref[...], lnb_ref[...]).astype(BF16)
    vr = _dot(hb, w_ref[:, :2 * GLA_DV])
    vg_ref[0] = vr[:, :GLA_DV].astype(BF16)
    r = vr[:, GLA_DV:]
    rg_ref[0] = (r * _sigmoid(r)).astype(BF16)
    gate_ref[0] = _sigmoid(_dot(hb, w_ref[:, 2 * GLA_DV:]) + b_gate_ref[...]).astype(BF16)


def _pre_a_call(x, lng, lnb, b_gate, w_packed):
    bsz, seq, _ = x.shape
    tm = PRE_A_TM
    row = lambda width: pl.BlockSpec((1, tm, width), lambda b, i: (b, i, 0))
    consts = (lng, lnb, b_gate)
    return pl.pallas_call(
        _pre_a_kernel,
        grid=(bsz, seq // tm),
        in_specs=[row(D_MODEL)] + [_const_spec(c.shape) for c in consts]
        + [_const_spec((D_MODEL, _PACK_A), (0, 0))],
        out_specs=(row(GLA_DV), row(GLA_DV), row(2 * D_MODEL)),
        out_shape=(jax.ShapeDtypeStruct((bsz, seq, GLA_DV), BF16),
                   jax.ShapeDtypeStruct((bsz, seq, GLA_DV), BF16),
                   jax.ShapeDtypeStruct((bsz, seq, 2 * D_MODEL), BF16)),
        compiler_params=pltpu.CompilerParams(
            dimension_semantics=("parallel", "parallel"), vmem_limit_bytes=VMEM_LIMIT),
        name="pre_a",
    )(x, *consts, w_packed)


def _pre_b_kernel(x_ref, cos_ref, sin_ref, lng_ref, lnb_ref, w_a2_ref,
                  b_a2_ref, qn_g_ref, kvn_g_ref, wq_nope_ref, wq_rope_ref, wkv_k_ref, wkv_v_ref,
                  w_ref, qe_ref, ke_ref, kd_ref, dec_ref, q_ref, k_ref, v_ref):
    hb = _layer_norm(x_ref[0], lng_ref[...], lnb_ref[...]).astype(BF16)
    tm = hb.shape[0]
    qk = _dot(hb, w_ref[:, :2 * GLA_DK])
    lat = _dot(hb, w_ref[:, 2 * GLA_DK:2 * GLA_DK + _LAT_W])
    slab = lat[:, _SLAB0:]

    z = _dot(slab.astype(BF16), w_a2_ref[...]) + b_a2_ref[...]
    la = (jnp.minimum(z, 0.0) - jnp.log(1.0 + jnp.exp(-jnp.abs(z)))) / GLA_TAU

    n = CUMSUM_ROWS
    row_i = lax.broadcasted_iota(jnp.int32, (n, n), 0)
    col_i = lax.broadcasted_iota(jnp.int32, (n, n), 1)
    same_chunk = (lax.shift_right_logical(row_i, _CHUNK_SHIFT)
                  == lax.shift_right_logical(col_i, _CHUNK_SHIFT))
    tri = (same_chunk & (row_i >= col_i)).astype(BF16)
    la_hi = la.astype(BF16)
    la_lo = (la - la_hi.astype(F32)).astype(BF16)
    b = jnp.concatenate(
        [_dot(tri, la_hi[r0:r0 + n]) + _dot(tri, la_lo[r0:r0 + n]) for r0 in range(0, tm, n)],
        axis=0)
    chunk_ends = range(GLA_CHUNK, tm + 1, GLA_CHUNK)
    bl = jnp.concatenate(
        [jnp.broadcast_to(b[e - 1:e], (GLA_CHUNK, GLA_DK)) for e in chunk_ends], axis=0)
    qs = qk[:, :GLA_DK] * GLA_HK ** -0.5
    kk = qk[:, GLA_DK:]
    qe_ref[0] = (qs * jnp.exp(b)).astype(BF16)
    ke_ref[0] = (kk * jnp.exp(-b)).astype(BF16)
    kd_ref[0] = (kk * jnp.exp(bl - b)).astype(BF16)
    for ci, e in enumerate(chunk_ends):
        dec_ref[0, ci] = jnp.exp(jnp.broadcast_to(b[e - 1:e], (SUBLANE, GLA_DK)))

    qn = _rms_norm(lat[:, :MLA_Q_RANK], qn_g_ref[...]).astype(BF16)
    ckv = _rms_norm(lat[:, MLA_Q_RANK:_SLAB0], kvn_g_ref[...]).astype(BF16)
    cos_t = cos_ref[0]
    sin_t = sin_ref[0]
    k_rope = (slab * cos_t + _rope_rot(slab) * sin_t)[:, :MLA_ROPE].astype(BF16)

    scale = MLA_QK ** -0.5 * LOG2_E
    q_nope = _dot(qn, wq_nope_ref[...]) * scale
    q_rope = _dot(qn, wq_rope_ref[...])
    reps = q_rope.shape[-1] // LANE
    q_rope = (q_rope * jnp.concatenate([cos_t] * reps, axis=-1)
              + _rope_rot(q_rope) * jnp.concatenate([sin_t] * reps, axis=-1)) * scale
    k_nope = _dot(ckv, wkv_k_ref[...])
    v = _dot(ckv, wkv_v_ref[...])
    for hd in range(MLA_HEADS):
        q_ref[0, hd, :, :MLA_NOPE] = q_nope[:, hd * MLA_NOPE:(hd + 1) * MLA_NOPE].astype(BF16)
        q_ref[0, hd, :, MLA_NOPE:] = q_rope[:, hd * MLA_ROPE:(hd + 1) * MLA_ROPE].astype(BF16)
        k_ref[0, hd, :, :MLA_NOPE] = k_nope[:, hd * MLA_NOPE:(hd + 1) * MLA_NOPE].astype(BF16)
        k_ref[0, hd, :, MLA_NOPE:] = k_rope
        v_ref[0, hd] = v[:, hd * MLA_V:(hd + 1) * MLA_V].astype(BF16)


def _pre_b_call(x, cos_t, sin_t, lng, lnb, w_a2, b_a2, qn_g, kvn_g,
                wq_nope, wq_rope, wkv_k, wkv_v, w_packed):
    bsz, seq, _ = x.shape
    tm = PRE_B_TM
    assert tm % CUMSUM_ROWS == 0 and CUMSUM_ROWS % GLA_CHUNK == 0
    row = lambda width: pl.BlockSpec((1, tm, width), lambda b, i: (b, i, 0))
    head = lambda width: pl.BlockSpec((1, MLA_HEADS, tm, width), lambda b, i: (b, 0, i, 0))
    consts = (lng, lnb, w_a2, b_a2, qn_g, kvn_g, wq_nope, wq_rope, wkv_k, wkv_v)
    out_shape = (
        jax.ShapeDtypeStruct((bsz, seq, GLA_DK), BF16),
        jax.ShapeDtypeStruct((bsz, seq, GLA_DK), BF16),
        jax.ShapeDtypeStruct((bsz, seq, GLA_DK), BF16),
        jax.ShapeDtypeStruct((bsz, seq // GLA_CHUNK, SUBLANE, GLA_DK), F32),
        jax.ShapeDtypeStruct((bsz, MLA_HEADS, seq, MLA_QK), BF16),
        jax.ShapeDtypeStruct((bsz, MLA_HEADS, seq, MLA_QK), BF16),
        jax.ShapeDtypeStruct((bsz, MLA_HEADS, seq, MLA_V), BF16),
    )
    dec_spec = pl.BlockSpec((1, tm // GLA_CHUNK, SUBLANE, GLA_DK), lambda b, i: (b, i, 0, 0))
    out_specs = (row(GLA_DK), row(GLA_DK), row(GLA_DK), dec_spec,
                 head(MLA_QK), head(MLA_QK), head(MLA_V))
    return pl.pallas_call(
        _pre_b_kernel,
        grid=(bsz, seq // tm),
        in_specs=[row(D_MODEL), row(LANE), row(LANE)] + [_const_spec(c.shape) for c in consts]
        + [_const_spec((D_MODEL, _PACK_B), (0, _PACK_A // _PACK_B))],
        out_specs=out_specs,
        out_shape=out_shape,
        compiler_params=pltpu.CompilerParams(
            dimension_semantics=("parallel", "parallel"), vmem_limit_bytes=VMEM_LIMIT),
        name="pre_b",
    )(x, cos_t, sin_t, *consts, w_packed)


def _gla_kernel(qe_ref, ke_ref, kd_ref, dec_ref, v_ref, r_ref, g_ref, o_ref, st_ref):
    @pl.when(pl.program_id(1) == 0)
    def _():
        st_ref[...] = jnp.zeros_like(st_ref)

    c = GLA_CHUNK
    row = lax.broadcasted_iota(jnp.int32, (c, c), 0)
    col = lax.broadcasted_iota(jnp.int32, (c, c), 1)
    causal = row >= col

    def chunk(ci, carry):
        rows = pl.ds(pl.multiple_of(ci * c, c), c)
        for hd in range(GLA_HEADS):
            kcols = slice(hd * GLA_HK, (hd + 1) * GLA_HK)
            vcols = slice(hd * GLA_HV, (hd + 1) * GLA_HV)
            qe = qe_ref[0, rows, kcols]
            v = v_ref[0, rows, vcols]
            s = lax.dot_general(qe, ke_ref[0, rows, kcols], _NT, preferred_element_type=F32)
            p = jnp.where(causal, s, 0.0).astype(BF16)
            st = st_ref[hd]
            o = _dot(p, v) + lax.dot_general(qe, st.astype(BF16), _NT, preferred_element_type=F32)
            st_ref[hd] = st * dec_ref[0, ci, 0:1, kcols] + lax.dot_general(
                v, kd_ref[0, rows, kcols], _TN, preferred_element_type=F32)
            on = _rms_norm(o, g_ref[:, vcols])
            o_ref[0, rows, vcols] = (on * r_ref[0, rows, vcols].astype(F32)).astype(BF16)
        return carry

    n_chunks = qe_ref.shape[1] // c
    lax.fori_loop(0, n_chunks, chunk, 0, unroll=n_chunks)


def _gla_call(qe, ke, kd, dec, vg, rg, g):
    bsz, seq, _ = qe.shape
    tb = GLA_TB
    kspec = pl.BlockSpec((1, tb, GLA_DK), lambda b, i: (b, i, 0))
    vspec = pl.BlockSpec((1, tb, GLA_DV), lambda b, i: (b, i, 0))
    dspec = pl.BlockSpec((1, tb // GLA_CHUNK, SUBLANE, GLA_DK), lambda b, i: (b, i, 0, 0))
    return pl.pallas_call(
        _gla_kernel,
        grid=(bsz, seq // tb),
        in_specs=[kspec, kspec, kspec, dspec, vspec, vspec, _const_spec(g.shape)],
        out_specs=vspec,
        out_shape=jax.ShapeDtypeStruct((bsz, seq, GLA_DV), BF16),
        scratch_shapes=[pltpu.VMEM((GLA_HEADS, GLA_HV, GLA_HK), F32)],
        compiler_params=pltpu.CompilerParams(
            dimension_semantics=("parallel", "arbitrary"), vmem_limit_bytes=VMEM_LIMIT),
        name="gla",
    )(qe, ke, kd, dec, vg, rg, g)


def _mla_kernel(q_ref, k_ref, v_ref, o_ref, vx_ref):
    tq = MLA_TQ
    seq = v_ref.shape[2]
    ones_col = lax.broadcasted_iota(jnp.int32, (seq, MLA_V), 1) == 0
    for hd in range(MLA_HPS):
        vx_ref[hd, :, :MLA_V] = v_ref[0, hd]
        vx_ref[hd, :, MLA_V:] = jnp.where(ones_col, 1.0, 0.0).astype(BF16)

    row = lax.broadcasted_iota(jnp.int32, (tq, tq), 0)
    col = lax.broadcasted_iota(jnp.int32, (tq, tq), 1)
    causal = row >= col
    for qi in range(seq // tq):
        q0 = qi * tq
        for hd in range(MLA_HPS):
            q = q_ref[0, hd, q0:q0 + tq, :]
            s_diag = lax.dot_general(q, k_ref[0, hd, q0:q0 + tq, :], _NT,
                                     preferred_element_type=F32)
            s_diag = jnp.where(causal, s_diag, -jnp.inf)
            m = jnp.max(s_diag, axis=-1, keepdims=True)
            if qi > 0:
                s_off = lax.dot_general(q, k_ref[0, hd, :q0, :], _NT, preferred_element_type=F32)
                m = jnp.maximum(m, jnp.max(s_off, axis=-1, keepdims=True))
            acc = _dot(jnp.exp2(s_diag - m).astype(BF16), vx_ref[hd, q0:q0 + tq, :])
            if qi > 0:
                acc = acc + _dot(jnp.exp2(s_off - m).astype(BF16), vx_ref[hd, :q0, :])
            o_ref[0, q0:q0 + tq, hd * MLA_V:(hd + 1) * MLA_V] = (
                acc[:, :MLA_V] / acc[:, MLA_V:MLA_V + 1]).astype(BF16)


def _mla_call(q, k, v):
    bsz, nh, seq, _ = q.shape
    hps = MLA_HPS
    head = lambda width: pl.BlockSpec((1, hps, seq, width), lambda b, h: (b, h, 0, 0))
    return pl.pallas_call(
        _mla_kernel,
        grid=(bsz, nh // hps),
        in_specs=[head(MLA_QK), head(MLA_QK), head(MLA_V)],
        out_specs=pl.BlockSpec((1, seq, hps * MLA_V), lambda b, h: (b, 0, h)),
        out_shape=jax.ShapeDtypeStruct((bsz, seq, nh * MLA_V), BF16),
        scratch_shapes=[pltpu.VMEM((hps, seq, 2 * MLA_V), BF16)],
        compiler_params=pltpu.CompilerParams(
            dimension_semantics=("parallel", "parallel"), vmem_limit_bytes=VMEM_LIMIT),
        name="mla",
    )(q, k, v)


def _post_kernel(x_ref, a_ref, m_ref, gate_ref, lng_ref, lnb_ref, wog_ref, wom_ref, wout_ref,
                 ln1g_ref, ln1b_ref, h1_ref):
    y_gla = _dot(a_ref[...], wog_ref[...])
    y_mla = _dot(m_ref[...], wom_ref[...])
    gate = gate_ref[...].astype(F32)
    merged = gate[:, :D_MODEL] * y_gla + gate[:, D_MODEL:] * y_mla
    mix = _dot(merged.astype(BF16), wout_ref[...])
    h = _layer_norm(x_ref[...], lng_ref[...], lnb_ref[...])
    h1_ref[...] = _layer_norm(DEEPNORM_ALPHA * h + mix, ln1g_ref[...], ln1b_ref[...])


def _post_call(x2, act, omla, gate, lng, lnb, wog, wom, wout, ln1g, ln1b):
    n = x2.shape[0]
    tm = POST_TM
    row = lambda width: pl.BlockSpec((tm, width), lambda i: (i, 0))
    consts = (lng, lnb, wog, wom, wout, ln1g, ln1b)
    return pl.pallas_call(
        _post_kernel,
        grid=(n // tm,),
        in_specs=[row(D_MODEL), row(D_MODEL), row(D_MODEL), row(2 * D_MODEL)]
        + [_const_spec(c.shape) for c in consts],
        out_specs=row(D_MODEL),
        out_shape=jax.ShapeDtypeStruct((n, D_MODEL), F32),
        compiler_params=pltpu.CompilerParams(
            dimension_semantics=("parallel",), vmem_limit_bytes=VMEM_LIMIT),
        name="post",
    )(x2, act, omla, gate, *consts)


def _ffn_kernel(h1_ref, w1_ref, w2_ref, g_ref, b_ref, o_ref):
    h1 = h1_ref[...]
    hb = h1.astype(BF16)
    acc = jnp.zeros(h1.shape, F32)
    for f0 in range(0, D_FF, FFN_TF):
        a = jnp.maximum(_dot(hb, w1_ref[:, f0:f0 + FFN_TF]), 0.0)
        acc = acc + _dot((a * a).astype(BF16), w2_ref[f0:f0 + FFN_TF, :])
    o_ref[...] = _layer_norm(DEEPNORM_ALPHA * h1 + acc, g_ref[...], b_ref[...])


def _ffn_call(h1, w1, w2, g, b):
    n = h1.shape[0]
    tm = FFN_TM
    row = pl.BlockSpec((tm, D_MODEL), lambda i: (i, 0))
    consts = (w1, w2, g, b)
    return pl.pallas_call(
        _ffn_kernel,
        grid=(n // tm,),
        in_specs=[row] + [_const_spec(c.shape) for c in consts],
        out_specs=row,
        out_shape=jax.ShapeDtypeStruct((n, D_MODEL), F32),
        compiler_params=pltpu.CompilerParams(
            dimension_semantics=("parallel",), vmem_limit_bytes=VMEM_LIMIT),
        name="ffn",
    )(h1, *consts)


def kernel(x, positions, ln_in_g, ln_in_b, w_in, w_gla_a2, b_gla_a2, gla_norm_g, w_o_gla,
           q_a_norm_g, w_q_b, kv_a_norm_g, w_kv_b, w_o_mla, b_gate, w_out,
           ln1_g, ln1_b, w_ff1, w_ff2, ln2_g, ln2_b):
    assert DEPTH == 1 and w_in.shape[0] == 1
    bsz, seq, _ = x.shape
    n = bsz * seq
    row2 = lambda a: a.reshape(1, -1)

    inv_freq = 1.0 / (ROPE_THETA ** (jnp.arange(0, MLA_ROPE, 2, dtype=F32) / MLA_ROPE))
    ang = positions.astype(F32)[..., None] * jnp.tile(inv_freq, LANE // (MLA_ROPE // 2))
    cos_t = jnp.cos(ang)
    sin_t = jnp.sin(ang)

    wi = w_in[0]
    slab_pad = LANE - MLA_ROPE - GLA_GATE_RANK
    pack_pad = _PACK_B - 2 * GLA_DK - _LAT_W + slab_pad
    w_packed = jnp.concatenate(
        [wi[:, _OFF_VG:_OFF_ALR], wi[:, _OFF_GATE:], wi[:, _OFF_QG:_OFF_VG],
         wi[:, _OFF_QLAT:_OFF_GATE], wi[:, _OFF_ALR:_OFF_QLAT],
         jnp.zeros((D_MODEL, pack_pad), F32)], axis=-1).astype(BF16)
    w_a2 = jnp.pad(w_gla_a2[0], ((_SLAB_ALR, slab_pad), (0, 0))).astype(BF16)

    wq = w_q_b[0].reshape(MLA_Q_RANK, MLA_HEADS, MLA_QK)
    wq_nope = wq[:, :, :MLA_NOPE].reshape(MLA_Q_RANK, -1).astype(BF16)
    wq_rope = wq[:, :, MLA_NOPE:].reshape(MLA_Q_RANK, -1).astype(BF16)
    wkv = w_kv_b[0].reshape(MLA_KV_RANK, MLA_HEADS, MLA_NOPE + MLA_V)
    wkv_k = wkv[:, :, :MLA_NOPE].reshape(MLA_KV_RANK, -1).astype(BF16)
    wkv_v = wkv[:, :, MLA_NOPE:].reshape(MLA_KV_RANK, -1).astype(BF16)

    lng, lnb = row2(ln_in_g), row2(ln_in_b)
    vg, rg, gate = _pre_a_call(x, lng, lnb, row2(b_gate[0]), w_packed)
    qe, ke, kd, dec, q, k, v = _pre_b_call(
        x, cos_t, sin_t, lng, lnb, w_a2, row2(b_gla_a2[0]),
        row2(q_a_norm_g[0]), row2(kv_a_norm_g[0]), wq_nope, wq_rope, wkv_k, wkv_v, w_packed)

    act = _gla_call(qe, ke, kd, dec, vg, rg, row2(gla_norm_g[0]))
    omla = _mla_call(q, k, v)

    h1 = _post_call(
        x.reshape(n, D_MODEL), act.reshape(n, GLA_DV), omla.reshape(n, MLA_HEADS * MLA_V),
        gate.reshape(n, 2 * D_MODEL), lng, lnb,
        w_o_gla[0].astype(BF16), w_o_mla[0].astype(BF16), w_out[0].astype(BF16),
        row2(ln1_g[0]), row2(ln1_b[0]))

    out = _ffn_call(h1, w_ff1[0].astype(BF16), w_ff2[0].astype(BF16),
                    row2(ln2_g[0]), row2(ln2_b[0]))
    return out.reshape(bsz, seq, D_MODEL)
```

```python
import jax
import jax.numpy as jnp
from jax import lax
from jax.experimental import pallas as pl
from jax.experimental.pallas import tpu as pltpu

D_MODEL = 1024
DEPTH = 1
LN_EPS = 1e-5
RMS_EPS = 1e-6

GLA_HEADS = 4
GLA_DK = D_MODEL // 2
GLA_DV = D_MODEL
GLA_HK = GLA_DK // GLA_HEADS
GLA_HV = GLA_DV // GLA_HEADS
GLA_GATE_RANK = 16
GLA_TAU = 16.0

MLA_HEADS = 8
MLA_Q_RANK = 384
MLA_KV_RANK = 256
MLA_NOPE = 128
MLA_ROPE = 64
MLA_V = 128
MLA_QK = MLA_NOPE + MLA_ROPE
ROPE_THETA = 10000.0

LOG2_E = 1.4426950408889634

D_FF = 4 * D_MODEL
DEEPNORM_ALPHA = (2.0 * DEPTH) ** 0.25

_OFF_QG = 0
_OFF_KG = _OFF_QG + GLA_DK
_OFF_VG = _OFF_KG + GLA_DK
_OFF_RG = _OFF_VG + GLA_DV
_OFF_ALR = _OFF_RG + GLA_DV
_OFF_QLAT = _OFF_ALR + GLA_GATE_RANK
_OFF_KVLAT = _OFF_QLAT + MLA_Q_RANK
_OFF_KROPE = _OFF_KVLAT + MLA_KV_RANK
_OFF_GATE = _OFF_KROPE + MLA_ROPE
D_IN = _OFF_GATE + 2 * D_MODEL

LANE = 128
SUBLANE = 8
VMEM_LIMIT = 56 * 1024 * 1024

_SLAB0 = MLA_Q_RANK + MLA_KV_RANK
_SLAB_ALR = MLA_ROPE
assert _SLAB0 % LANE == 0 and _SLAB_ALR + GLA_GATE_RANK <= LANE
_LAT_W = _SLAB0 + LANE
_PACK_A = 2 * GLA_DV + 2 * D_MODEL
_PACK_B = _PACK_A // 2
assert 2 * GLA_DK + _LAT_W <= _PACK_B

PACK_ROWS = 256
PRE_A_TM = 1024
PRE_B_TM = 1024
CUMSUM_ROWS = 256
GLA_CHUNK = 64
_CHUNK_SHIFT = GLA_CHUNK.bit_length() - 1
assert 1 << _CHUNK_SHIFT == GLA_CHUNK
GLA_TB = 512
MLA_TQ = 256
MLA_HPS = 4
POST_TM = 1024
FFN_TM = 1024
FFN_TF = 1024

_NT = (((1,), (1,)), ((), ()))
_TN = (((0,), (0,)), ((), ()))

BF16 = jnp.bfloat16
F32 = jnp.float32


def _dot(a, b):
    return jnp.dot(a, b, preferred_element_type=F32)


def _layer_norm(x, g, b):
    mu = jnp.mean(x, axis=-1, keepdims=True)
    xc = x - mu
    var = jnp.mean(xc * xc, axis=-1, keepdims=True)
    return xc * lax.rsqrt(var + LN_EPS) * g + b


def _rms_norm(x, g):
    return x * lax.rsqrt(jnp.mean(x * x, axis=-1, keepdims=True) + RMS_EPS) * g


def _sigmoid(x):
    return 1.0 / (1.0 + jnp.exp(-x))


def _rope_rot(x):
    width = x.shape[-1]
    half = MLA_ROPE // 2
    lane = lax.broadcasted_iota(jnp.int32, x.shape, 1)
    first_half = (lane & (MLA_ROPE - 1)) < half
    ahead = pltpu.roll(x, width - half, 1)
    behind = pltpu.roll(x, half, 1)
    return jnp.where(first_half, -ahead, behind)


def _const_spec(shape, index=None):
    index = (0,) * len(shape) if index is None else index
    return pl.BlockSpec(shape, lambda *_: index, pipeline_mode=pl.Buffered(1))


def _pack_kernel(w_ref, o_ref):
    w = w_ref[0]
    pieces = (w[:, _OFF_VG:_OFF_ALR], w[:, _OFF_GATE:], w[:, _OFF_QG:_OFF_VG],
              w[:, _OFF_QLAT:_OFF_GATE], w[:, _OFF_ALR:_OFF_QLAT])
    off = 0
    for piece in pieces:
        o_ref[:, off:off + piece.shape[1]] = piece.astype(BF16)
        off += piece.shape[1]
    o_ref[:, off:] = jnp.zeros((w.shape[0], o_ref.shape[1] - off), BF16)


def _pack_call(w_in):
    rows = PACK_ROWS
    return pl.pallas_call(
        _pack_kernel,
        grid=(D_MODEL // rows,),
        in_specs=[pl.BlockSpec((1, rows, D_IN), lambda i: (0, i, 0))],
        out_specs=pl.BlockSpec((rows, _PACK_A + _PACK_B), lambda i: (i, 0)),
        out_shape=jax.ShapeDtypeStruct((D_MODEL, _PACK_A + _PACK_B), BF16),
        compiler_params=pltpu.CompilerParams(
            dimension_semantics=("parallel",), vmem_limit_bytes=VMEM_LIMIT),
        name="pack",
    )(w_in)


def _pre_a_kernel(x_ref, lng_ref, lnb_ref, b_gate_ref, w_ref, vg_ref, rg_ref, gate_ref):
    hb = _layer_norm(x_ref[0], lng_ref[...], lnb_ref[...]).astype(BF16)
    vr = _dot(hb, w_ref[:, :2 * GLA_DV])
    vg_ref[0] = vr[:, :GLA_DV].astype(BF16)
    r = vr[:, GLA_DV:]
    rg_ref[0] = (r * _sigmoid(r)).astype(BF16)
    gate_ref[0] = _sigmoid(_dot(hb, w_ref[:, 2 * GLA_DV:]) + b_gate_ref[...]).astype(BF16)


def _pre_a_call(x, lng, lnb, b_gate, w_packed):
    bsz, seq, _ = x.shape
    tm = PRE_A_TM
    row = lambda width: pl.BlockSpec((1, tm, width), lambda b, i: (b, i, 0))
    consts = (lng, lnb, b_gate)
    return pl.pallas_call(
        _pre_a_kernel,
        grid=(bsz, seq // tm),
        in_specs=[row(D_MODEL)] + [_const_spec(c.shape) for c in consts]
        + [_const_spec((D_MODEL, _PACK_A), (0, 0))],
        out_specs=(row(GLA_DV), row(GLA_DV), row(2 * D_MODEL)),
        out_shape=(jax.ShapeDtypeStruct((bsz, seq, GLA_DV), BF16),
                   jax.ShapeDtypeStruct((bsz, seq, GLA_DV), BF16),
                   jax.ShapeDtypeStruct((bsz, seq, 2 * D_MODEL), BF16)),
        compiler_params=pltpu.CompilerParams(
            dimension_semantics=("parallel", "parallel"), vmem_limit_bytes=VMEM_LIMIT),
        name="pre_a",
    )(x, *consts, w_packed)


def _pre_b_kernel(x_ref, cs_ref, lng_ref, lnb_ref, w_a2_ref,
                  b_a2_ref, qn_g_ref, kvn_g_ref, wq_nope_ref, wq_rope_ref, wkv_k_ref, wkv_v_ref,
                  w_ref, qe_ref, ke_ref, kd_ref, dec_ref, q_ref, k_ref, v_ref):
    hb = _layer_norm(x_ref[0], lng_ref[...], lnb_ref[...]).astype(BF16)
    tm = hb.shape[0]
    qk = _dot(hb, w_ref[:, :2 * GLA_DK])
    lat = _dot(hb, w_ref[:, 2 * GLA_DK:2 * GLA_DK + _LAT_W])
    slab = lat[:, _SLAB0:]

    z = _dot(slab.astype(BF16), w_a2_ref[...]) + b_a2_ref[...]
    la = (jnp.minimum(z, 0.0) - jnp.log(1.0 + jnp.exp(-jnp.abs(z)))) / GLA_TAU

    n = CUMSUM_ROWS
    row_i = lax.broadcasted_iota(jnp.int32, (n, n), 0)
    col_i = lax.broadcasted_iota(jnp.int32, (n, n), 1)
    same_chunk = (lax.shift_right_logical(row_i, _CHUNK_SHIFT)
                  == lax.shift_right_logical(col_i, _CHUNK_SHIFT))
    tri = (same_chunk & (row_i >= col_i)).astype(BF16)
    la_hi = la.astype(BF16)
    la_lo = (la - la_hi.astype(F32)).astype(BF16)
    b = jnp.concatenate(
        [_dot(tri, la_hi[r0:r0 + n]) + _dot(tri, la_lo[r0:r0 + n]) for r0 in range(0, tm, n)],
        axis=0)
    chunk_ends = range(GLA_CHUNK, tm + 1, GLA_CHUNK)
    bl = jnp.concatenate(
        [jnp.broadcast_to(b[e - 1:e], (GLA_CHUNK, GLA_DK)) for e in chunk_ends], axis=0)
    qs = qk[:, :GLA_DK] * GLA_HK ** -0.5
    kk = qk[:, GLA_DK:]
    qe_ref[0] = (qs * jnp.exp(b)).astype(BF16)
    ke_ref[0] = (kk * jnp.exp(-b)).astype(BF16)
    kd_ref[0] = (kk * jnp.exp(bl - b)).astype(BF16)
    for ci, e in enumerate(chunk_ends):
        dec_ref[0, ci] = jnp.exp(jnp.broadcast_to(b[e - 1:e], (SUBLANE, GLA_DK)))

    qn = _rms_norm(lat[:, :MLA_Q_RANK], qn_g_ref[...]).astype(BF16)
    ckv = _rms_norm(lat[:, MLA_Q_RANK:_SLAB0], kvn_g_ref[...]).astype(BF16)
    cs = cs_ref[0]
    half = MLA_ROPE // 2
    cos_t = jnp.concatenate([cs[:, :half]] * (LANE // half), axis=-1)
    sin_t = jnp.concatenate([cs[:, half:]] * (LANE // half), axis=-1)
    k_rope = (slab * cos_t + _rope_rot(slab) * sin_t)[:, :MLA_ROPE].astype(BF16)

    scale = MLA_QK ** -0.5 * LOG2_E
    q_nope = _dot(qn, wq_nope_ref[...]) * scale
    q_rope = _dot(qn, wq_rope_ref[...])
    reps = q_rope.shape[-1] // LANE
    q_rope = (q_rope * jnp.concatenate([cos_t] * reps, axis=-1)
              + _rope_rot(q_rope) * jnp.concatenate([sin_t] * reps, axis=-1)) * scale
    k_nope = _dot(ckv, wkv_k_ref[...])
    v = _dot(ckv, wkv_v_ref[...])
    for hd in range(MLA_HEADS):
        q_ref[0, hd, :, :MLA_NOPE] = q_nope[:, hd * MLA_NOPE:(hd + 1) * MLA_NOPE].astype(BF16)
        q_ref[0, hd, :, MLA_NOPE:] = q_rope[:, hd * MLA_ROPE:(hd + 1) * MLA_ROPE].astype(BF16)
        k_ref[0, hd, :, :MLA_NOPE] = k_nope[:, hd * MLA_NOPE:(hd + 1) * MLA_NOPE].astype(BF16)
        k_ref[0, hd, :, MLA_NOPE:] = k_rope
        v_ref[0, hd] = v[:, hd * MLA_V:(hd + 1) * MLA_V].astype(BF16)


def _pre_b_call(x, cos_sin, lng, lnb, w_a2, b_a2, qn_g, kvn_g,
                wq_nope, wq_rope, wkv_k, wkv_v, w_packed):
    bsz, seq, _ = x.shape
    tm = PRE_B_TM
    assert tm % CUMSUM_ROWS == 0 and CUMSUM_ROWS % GLA_CHUNK == 0
    row = lambda width: pl.BlockSpec((1, tm, width), lambda b, i: (b, i, 0))
    head = lambda width: pl.BlockSpec((1, MLA_HEADS, tm, width), lambda b, i: (b, 0, i, 0))
    consts = (lng, lnb, w_a2, b_a2, qn_g, kvn_g, wq_nope, wq_rope, wkv_k, wkv_v)
    out_shape = (
        jax.ShapeDtypeStruct((bsz, seq, GLA_DK), BF16),
        jax.ShapeDtypeStruct((bsz, seq, GLA_DK), BF16),
        jax.ShapeDtypeStruct((bsz, seq, GLA_DK), BF16),
        jax.ShapeDtypeStruct((bsz, seq // GLA_CHUNK, SUBLANE, GLA_DK), F32),
        jax.ShapeDtypeStruct((bsz, MLA_HEADS, seq, MLA_QK), BF16),
        jax.ShapeDtypeStruct((bsz, MLA_HEADS, seq, MLA_QK), BF16),
        jax.ShapeDtypeStruct((bsz, MLA_HEADS, seq, MLA_V), BF16),
    )
    dec_spec = pl.BlockSpec((1, tm // GLA_CHUNK, SUBLANE, GLA_DK), lambda b, i: (b, i, 0, 0))
    out_specs = (row(GLA_DK), row(GLA_DK), row(GLA_DK), dec_spec,
                 head(MLA_QK), head(MLA_QK), head(MLA_V))
    return pl.pallas_call(
        _pre_b_kernel,
        grid=(bsz, seq // tm),
        in_specs=[row(D_MODEL), row(MLA_ROPE)] + [_const_spec(c.shape) for c in consts]
        + [_const_spec((D_MODEL, _PACK_B), (0, _PACK_A // _PACK_B))],
        out_specs=out_specs,
        out_shape=out_shape,
        compiler_params=pltpu.CompilerParams(
            dimension_semantics=("parallel", "parallel"), vmem_limit_bytes=VMEM_LIMIT),
        name="pre_b",
    )(x, cos_sin, *consts, w_packed)


def _gla_kernel(qe_ref, ke_ref, kd_ref, dec_ref, v_ref, r_ref, g_ref, o_ref, st_ref):
    @pl.when(pl.program_id(1) == 0)
    def _():
        st_ref[...] = jnp.zeros_like(st_ref)

    c = GLA_CHUNK
    row = lax.broadcasted_iota(jnp.int32, (c, c), 0)
    col = lax.broadcasted_iota(jnp.int32, (c, c), 1)
    causal = row >= col

    def chunk(ci, carry):
        rows = pl.ds(pl.multiple_of(ci * c, c), c)
        for hd in range(GLA_HEADS):
            kcols = slice(hd * GLA_HK, (hd + 1) * GLA_HK)
            vcols = slice(hd * GLA_HV, (hd + 1) * GLA_HV)
            qe = qe_ref[0, rows, kcols]
            v = v_ref[0, rows, vcols]
            s = lax.dot_general(qe, ke_ref[0, rows, kcols], _NT, preferred_element_type=F32)
            p = jnp.where(causal, s, 0.0).astype(BF16)
            st = st_ref[hd]
            o = _dot(p, v) + lax.dot_general(qe, st.astype(BF16), _NT, preferred_element_type=F32)
            st_ref[hd] = st * dec_ref[0, ci, 0:1, kcols] + lax.dot_general(
                v, kd_ref[0, rows, kcols], _TN, preferred_element_type=F32)
            on = _rms_norm(o, g_ref[:, vcols])
            o_ref[0, rows, vcols] = (on * r_ref[0, rows, vcols].astype(F32)).astype(BF16)
        return carry

    n_chunks = qe_ref.shape[1] // c
    lax.fori_loop(0, n_chunks, chunk, 0, unroll=n_chunks)


def _gla_call(qe, ke, kd, dec, vg, rg, g):
    bsz, seq, _ = qe.shape
    tb = GLA_TB
    kspec = pl.BlockSpec((1, tb, GLA_DK), lambda b, i: (b, i, 0))
    vspec = pl.BlockSpec((1, tb, GLA_DV), lambda b, i: (b, i, 0))
    dspec = pl.BlockSpec((1, tb // GLA_CHUNK, SUBLANE, GLA_DK), lambda b, i: (b, i, 0, 0))
    return pl.pallas_call(
        _gla_kernel,
        grid=(bsz, seq // tb),
        in_specs=[kspec, kspec, kspec, dspec, vspec, vspec, _const_spec(g.shape)],
        out_specs=vspec,
        out_shape=jax.ShapeDtypeStruct((bsz, seq, GLA_DV), BF16),
        scratch_shapes=[pltpu.VMEM((GLA_HEADS, GLA_HV, GLA_HK), F32)],
        compiler_params=pltpu.CompilerParams(
            dimension_semantics=("parallel", "arbitrary"), vmem_limit_bytes=VMEM_LIMIT),
        name="gla",
    )(qe, ke, kd, dec, vg, rg, g)


def _mla_kernel(q_ref, k_ref, v_ref, o_ref, vx_ref):
    tq = MLA_TQ
    seq = v_ref.shape[2]
    ones_col = lax.broadcasted_iota(jnp.int32, (seq, MLA_V), 1) == 0
    for hd in range(MLA_HPS):
        vx_ref[hd, :, :MLA_V] = v_ref[0, hd]
        vx_ref[hd, :, MLA_V:] = jnp.where(ones_col, 1.0, 0.0).astype(BF16)

    row = lax.broadcasted_iota(jnp.int32, (tq, tq), 0)
    col = lax.broadcasted_iota(jnp.int32, (tq, tq), 1)
    causal = row >= col
    for qi in range(seq // tq):
        q0 = qi * tq
        for hd in range(MLA_HPS):
            q = q_ref[0, hd, q0:q0 + tq, :]
            s_diag = lax.dot_general(q, k_ref[0, hd, q0:q0 + tq, :], _NT,
                                     preferred_element_type=F32)
            s_diag = jnp.where(causal, s_diag, -jnp.inf)
            m = jnp.max(s_diag, axis=-1, keepdims=True)
            if qi > 0:
                s_off = lax.dot_general(q, k_ref[0, hd, :q0, :], _NT, preferred_element_type=F32)
                m = jnp.maximum(m, jnp.max(s_off, axis=-1, keepdims=True))
            acc = _dot(jnp.exp2(s_diag - m).astype(BF16), vx_ref[hd, q0:q0 + tq, :])
            if qi > 0:
                acc = acc + _dot(jnp.exp2(s_off - m).astype(BF16), vx_ref[hd, :q0, :])
            o_ref[0, q0:q0 + tq, hd * MLA_V:(hd + 1) * MLA_V] = (
                acc[:, :MLA_V] / acc[:, MLA_V:MLA_V + 1]).astype(BF16)


def _mla_call(q, k, v):
    bsz, nh, seq, _ = q.shape
    hps = MLA_HPS
    head = lambda width: pl.BlockSpec((1, hps, seq, width), lambda b, h: (b, h, 0, 0))
    return pl.pallas_call(
        _mla_kernel,
        grid=(bsz, nh // hps),
        in_specs=[head(MLA_QK), head(MLA_QK), head(MLA_V)],
        out_specs=pl.BlockSpec((1, seq, hps * MLA_V), lambda b, h: (b, 0, h)),
        out_shape=jax.ShapeDtypeStruct((bsz, seq, nh * MLA_V), BF16),
        scratch_shapes=[pltpu.VMEM((hps, seq, 2 * MLA_V), BF16)],
        compiler_params=pltpu.CompilerParams(
            dimension_semantics=("parallel", "parallel"), vmem_limit_bytes=VMEM_LIMIT),
        name="mla",
    )(q, k, v)


def _post_kernel(x_ref, a_ref, m_ref, gate_ref, lng_ref, lnb_ref, wog_ref, wom_ref, wout_ref,
                 ln1g_ref, ln1b_ref, h1_ref):
    y_gla = _dot(a_ref[...], wog_ref[...])
    y_mla = _dot(m_ref[...], wom_ref[...])
    gate = gate_ref[...].astype(F32)
    merged = gate[:, :D_MODEL] * y_gla + gate[:, D_MODEL:] * y_mla
    mix = _dot(merged.astype(BF16), wout_ref[...])
    h = _layer_norm(x_ref[...], lng_ref[...], lnb_ref[...])
    h1_ref[...] = _layer_norm(DEEPNORM_ALPHA * h + mix, ln1g_ref[...], ln1b_ref[...])


def _post_call(x2, act, omla, gate, lng, lnb, wog, wom, wout, ln1g, ln1b):
    n = x2.shape[0]
    tm = POST_TM
    row = lambda width: pl.BlockSpec((tm, width), lambda i: (i, 0))
    consts = (lng, lnb, wog, wom, wout, ln1g, ln1b)
    return pl.pallas_call(
        _post_kernel,
        grid=(n // tm,),
        in_specs=[row(D_MODEL), row(D_MODEL), row(D_MODEL), row(2 * D_MODEL)]
        + [_const_spec(c.shape) for c in consts],
        out_specs=row(D_MODEL),
        out_shape=jax.ShapeDtypeStruct((n, D_MODEL), F32),
        compiler_params=pltpu.CompilerParams(
            dimension_semantics=("parallel",), vmem_limit_bytes=VMEM_LIMIT),
        name="post",
    )(x2, act, omla, gate, *consts)


def _ffn_kernel(h1_ref, w1_ref, w2_ref, g_ref, b_ref, o_ref):
    h1 = h1_ref[...]
    hb = h1.astype(BF16)
    acc = jnp.zeros(h1.shape, F32)
    for f0 in range(0, D_FF, FFN_TF):
        a = jnp.maximum(_dot(hb, w1_ref[:, f0:f0 + FFN_TF]), 0.0)
        acc = acc + _dot((a * a).astype(BF16), w2_ref[f0:f0 + FFN_TF, :])
    o_ref[...] = _layer_norm(DEEPNORM_ALPHA * h1 + acc, g_ref[...], b_ref[...])


def _ffn_call(h1, w1, w2, g, b):
    n = h1.shape[0]
    tm = FFN_TM
    row = pl.BlockSpec((tm, D_MODEL), lambda i: (i, 0))
    consts = (w1, w2, g, b)
    return pl.pallas_call(
        _ffn_kernel,
        grid=(n // tm,),
        in_specs=[row] + [_const_spec(c.shape) for c in consts],
        out_specs=row,
        out_shape=jax.ShapeDtypeStruct((n, D_MODEL), F32),
        compiler_params=pltpu.CompilerParams(
            dimension_semantics=("parallel",), vmem_limit_bytes=VMEM_LIMIT),
        name="ffn",
    )(h1, *consts)


def kernel(x, positions, ln_in_g, ln_in_b, w_in, w_gla_a2, b_gla_a2, gla_norm_g, w_o_gla,
           q_a_norm_g, w_q_b, kv_a_norm_g, w_kv_b, w_o_mla, b_gate, w_out,
           ln1_g, ln1_b, w_ff1, w_ff2, ln2_g, ln2_b):
    assert DEPTH == 1 and w_in.shape[0] == 1
    bsz, seq, _ = x.shape
    n = bsz * seq
    row2 = lambda a: a.reshape(1, -1)

    inv_freq = 1.0 / (ROPE_THETA ** (jnp.arange(0, MLA_ROPE, 2, dtype=F32) / MLA_ROPE))
    ang = positions.astype(F32)[..., None] * inv_freq
    cos_sin = jnp.concatenate([jnp.cos(ang), jnp.sin(ang)], axis=-1)

    w_packed = _pack_call(w_in)
    slab_pad = LANE - MLA_ROPE - GLA_GATE_RANK
    w_a2 = jnp.pad(w_gla_a2[0], ((_SLAB_ALR, slab_pad), (0, 0))).astype(BF16)

    wq = w_q_b[0].reshape(MLA_Q_RANK, MLA_HEADS, MLA_QK)
    wq_nope = wq[:, :, :MLA_NOPE].reshape(MLA_Q_RANK, -1).astype(BF16)
    wq_rope = wq[:, :, MLA_NOPE:].reshape(MLA_Q_RANK, -1).astype(BF16)
    wkv = w_kv_b[0].reshape(MLA_KV_RANK, MLA_HEADS, MLA_NOPE + MLA_V)
    wkv_k = wkv[:, :, :MLA_NOPE].reshape(MLA_KV_RANK, -1).astype(BF16)
    wkv_v = wkv[:, :, MLA_NOPE:].reshape(MLA_KV_RANK, -1).astype(BF16)

    lng, lnb = row2(ln_in_g), row2(ln_in_b)
    vg, rg, gate = _pre_a_call(x, lng, lnb, row2(b_gate[0]), w_packed)
    qe, ke, kd, dec, q, k, v = _pre_b_call(
        x, cos_sin, lng, lnb, w_a2, row2(b_gla_a2[0]),
        row2(q_a_norm_g[0]), row2(kv_a_norm_g[0]), wq_nope, wq_rope, wkv_k, wkv_v, w_packed)

    act = _gla_call(qe, ke, kd, dec, vg, rg, row2(gla_norm_g[0]))
    omla = _mla_call(q, k, v)

    h1 = _post_call(
        x.reshape(n, D_MODEL), act.reshape(n, GLA_DV), omla.reshape(n, MLA_HEADS * MLA_V),
        gate.reshape(n, 2 * D_MODEL), lng, lnb,
        w_o_gla[0].astype(BF16), w_o_mla[0].astype(BF16), w_out[0].astype(BF16),
        row2(ln1_g[0]), row2(ln1_b[0]))

    out = _ffn_call(h1, w_ff1[0].astype(BF16), w_ff2[0].astype(BF16),
                    row2(ln2_g[0]), row2(ln2_b[0]))
    return out.reshape(bsz, seq, D_MODEL)
```

```python
import jax
import jax.numpy as jnp
from jax import lax
from jax.experimental import pallas as pl
from jax.experimental.pallas import tpu as pltpu

D_MODEL = 1024
DEPTH = 1
LN_EPS = 1e-5
RMS_EPS = 1e-6

GLA_HEADS = 4
GLA_DK = D_MODEL // 2
GLA_DV = D_MODEL
GLA_HK = GLA_DK // GLA_HEADS
GLA_HV = GLA_DV // GLA_HEADS
GLA_GATE_RANK = 16
GLA_TAU = 16.0

MLA_HEADS = 8
MLA_Q_RANK = 384
MLA_KV_RANK = 256
MLA_NOPE = 128
MLA_ROPE = 64
MLA_V = 128
MLA_QK = MLA_NOPE + MLA_ROPE
ROPE_THETA = 10000.0

LOG2_E = 1.4426950408889634

D_FF = 4 * D_MODEL
DEEPNORM_ALPHA = (2.0 * DEPTH) ** 0.25

_OFF_QG = 0
_OFF_KG = _OFF_QG + GLA_DK
_OFF_VG = _OFF_KG + GLA_DK
_OFF_RG = _OFF_VG + GLA_DV
_OFF_ALR = _OFF_RG + GLA_DV
_OFF_QLAT = _OFF_ALR + GLA_GATE_RANK
_OFF_KVLAT = _OFF_QLAT + MLA_Q_RANK
_OFF_KROPE = _OFF_KVLAT + MLA_KV_RANK
_OFF_GATE = _OFF_KROPE + MLA_ROPE
D_IN = _OFF_GATE + 2 * D_MODEL

LANE = 128
SUBLANE = 8
VMEM_LIMIT = 56 * 1024 * 1024

_SLAB0 = MLA_Q_RANK + MLA_KV_RANK
_SLAB_ALR = MLA_ROPE
assert _SLAB0 % LANE == 0 and _SLAB_ALR + GLA_GATE_RANK <= LANE
_LAT_W = _SLAB0 + LANE
_PACK_A = 2 * GLA_DV + 2 * D_MODEL
_PACK_B = _PACK_A // 2
assert 2 * GLA_DK + _LAT_W <= _PACK_B

PACK_COLS = 256
PRE_A_TM = 1024
PRE_B_TM = 1024
CUMSUM_ROWS = 256
GLA_CHUNK = 64
_CHUNK_SHIFT = GLA_CHUNK.bit_length() - 1
assert 1 << _CHUNK_SHIFT == GLA_CHUNK
GLA_TB = 512
MLA_TQ = 256
MLA_HPS = 4
POST_TM = 1024
FFN_TM = 1024
FFN_TF = 1024

_NT = (((1,), (1,)), ((), ()))
_TN = (((0,), (0,)), ((), ()))

BF16 = jnp.bfloat16
F32 = jnp.float32


def _dot(a, b):
    return jnp.dot(a, b, preferred_element_type=F32)


def _dot_t(a, b_t):
    return lax.dot_general(a, b_t, _NT, preferred_element_type=F32)


def _layer_norm(x, g, b):
    mu = jnp.mean(x, axis=-1, keepdims=True)
    xc = x - mu
    var = jnp.mean(xc * xc, axis=-1, keepdims=True)
    return xc * lax.rsqrt(var + LN_EPS) * g + b


def _rms_norm(x, g):
    return x * lax.rsqrt(jnp.mean(x * x, axis=-1, keepdims=True) + RMS_EPS) * g


def _sigmoid(x):
    return 1.0 / (1.0 + jnp.exp(-x))


def _rope_rot(x):
    width = x.shape[-1]
    half = MLA_ROPE // 2
    lane = lax.broadcasted_iota(jnp.int32, x.shape, 1)
    first_half = (lane & (MLA_ROPE - 1)) < half
    ahead = pltpu.roll(x, width - half, 1)
    behind = pltpu.roll(x, half, 1)
    return jnp.where(first_half, -ahead, behind)


def _const_spec(shape, index=None):
    index = (0,) * len(shape) if index is None else index
    return pl.BlockSpec(shape, lambda *_: index, pipeline_mode=pl.Buffered(1))


def _pack_kernel(wt_ref, o_ref):
    pieces = ((_OFF_VG, _OFF_ALR), (_OFF_GATE, D_IN), (_OFF_QG, _OFF_VG),
              (_OFF_QLAT, _OFF_GATE), (_OFF_ALR, _OFF_QLAT))
    off = 0
    for lo, hi in pieces:
        o_ref[off:off + hi - lo, :] = wt_ref[lo:hi, :].astype(BF16)
        off += hi - lo
    o_ref[off:, :] = jnp.zeros((o_ref.shape[0] - off, o_ref.shape[1]), BF16)


def _pack_call(w_t):
    cols = PACK_COLS
    return pl.pallas_call(
        _pack_kernel,
        grid=(D_MODEL // cols,),
        in_specs=[pl.BlockSpec((D_IN, cols), lambda i: (0, i))],
        out_specs=pl.BlockSpec((_PACK_A + _PACK_B, cols), lambda i: (0, i)),
        out_shape=jax.ShapeDtypeStruct((_PACK_A + _PACK_B, D_MODEL), BF16),
        compiler_params=pltpu.CompilerParams(
            dimension_semantics=("parallel",), vmem_limit_bytes=VMEM_LIMIT),
        name="pack",
    )(w_t)


def _pre_a_kernel(x_ref, lng_ref, lnb_ref, b_gate_ref, w_ref, vg_ref, rg_ref, gate_ref):
    hb = _layer_norm(x_ref[0], lng_ref[...], lnb_ref[...]).astype(BF16)
    vr = _dot_t(hb, w_ref[:2 * GLA_DV, :])
    vg_ref[0] = vr[:, :GLA_DV].astype(BF16)
    r = vr[:, GLA_DV:]
    rg_ref[0] = (r * _sigmoid(r)).astype(BF16)
    gate_ref[0] = _sigmoid(_dot_t(hb, w_ref[2 * GLA_DV:, :]) + b_gate_ref[...]).astype(BF16)


def _pre_a_call(x, lng, lnb, b_gate, w_packed):
    bsz, seq, _ = x.shape
    tm = PRE_A_TM
    row = lambda width: pl.BlockSpec((1, tm, width), lambda b, i: (b, i, 0))
    consts = (lng, lnb, b_gate)
    return pl.pallas_call(
        _pre_a_kernel,
        grid=(bsz, seq // tm),
        in_specs=[row(D_MODEL)] + [_const_spec(c.shape) for c in consts]
        + [_const_spec((_PACK_A, D_MODEL), (0, 0))],
        out_specs=(row(GLA_DV), row(GLA_DV), row(2 * D_MODEL)),
        out_shape=(jax.ShapeDtypeStruct((bsz, seq, GLA_DV), BF16),
                   jax.ShapeDtypeStruct((bsz, seq, GLA_DV), BF16),
                   jax.ShapeDtypeStruct((bsz, seq, 2 * D_MODEL), BF16)),
        compiler_params=pltpu.CompilerParams(
            dimension_semantics=("parallel", "parallel"), vmem_limit_bytes=VMEM_LIMIT),
        name="pre_a",
    )(x, *consts, w_packed)


def _pre_b_kernel(x_ref, cs_ref, lng_ref, lnb_ref, w_a2_ref,
                  b_a2_ref, qn_g_ref, kvn_g_ref, wq_nope_ref, wq_rope_ref, wkv_k_ref, wkv_v_ref,
                  w_ref, qe_ref, ke_ref, kd_ref, dec_ref, q_ref, k_ref, v_ref):
    hb = _layer_norm(x_ref[0], lng_ref[...], lnb_ref[...]).astype(BF16)
    tm = hb.shape[0]
    qk = _dot_t(hb, w_ref[:2 * GLA_DK, :])
    lat = _dot_t(hb, w_ref[2 * GLA_DK:2 * GLA_DK + _LAT_W, :])
    slab = lat[:, _SLAB0:]

    z = _dot(slab.astype(BF16), w_a2_ref[...]) + b_a2_ref[...]
    la = (jnp.minimum(z, 0.0) - jnp.log(1.0 + jnp.exp(-jnp.abs(z)))) / GLA_TAU

    n = CUMSUM_ROWS
    row_i = lax.broadcasted_iota(jnp.int32, (n, n), 0)
    col_i = lax.broadcasted_iota(jnp.int32, (n, n), 1)
    same_chunk = (lax.shift_right_logical(row_i, _CHUNK_SHIFT)
                  == lax.shift_right_logical(col_i, _CHUNK_SHIFT))
    tri = (same_chunk & (row_i >= col_i)).astype(BF16)
    la_hi = la.astype(BF16)
    la_lo = (la - la_hi.astype(F32)).astype(BF16)
    b = jnp.concatenate(
        [_dot(tri, la_hi[r0:r0 + n]) + _dot(tri, la_lo[r0:r0 + n]) for r0 in range(0, tm, n)],
        axis=0)
    chunk_ends = range(GLA_CHUNK, tm + 1, GLA_CHUNK)
    bl = jnp.concatenate(
        [jnp.broadcast_to(b[e - 1:e], (GLA_CHUNK, GLA_DK)) for e in chunk_ends], axis=0)
    qs = qk[:, :GLA_DK] * GLA_HK ** -0.5
    kk = qk[:, GLA_DK:]
    qe_ref[0] = (qs * jnp.exp(b)).astype(BF16)
    ke_ref[0] = (kk * jnp.exp(-b)).astype(BF16)
    kd_ref[0] = (kk * jnp.exp(bl - b)).astype(BF16)
    for ci, e in enumerate(chunk_ends):
        dec_ref[0, ci] = jnp.exp(jnp.broadcast_to(b[e - 1:e], (SUBLANE, GLA_DK)))

    qn = _rms_norm(lat[:, :MLA_Q_RANK], qn_g_ref[...]).astype(BF16)
    ckv = _rms_norm(lat[:, MLA_Q_RANK:_SLAB0], kvn_g_ref[...]).astype(BF16)
    half = MLA_ROPE // 2
    cs = cs_ref[0].T
    group = lax.shift_right_logical(lax.broadcasted_iota(jnp.int32, cs.shape, 1),
                                    half.bit_length() - 1)
    r1, r2, r3 = (pltpu.roll(cs, k * half, 1) for k in (1, 2, 3))
    cos_t = jnp.where(group == 0, cs, jnp.where(group == 1, r1, jnp.where(group == 2, r2, r3)))
    sin_t = jnp.where(group == 0, r3, jnp.where(group == 1, cs, jnp.where(group == 2, r1, r2)))
    k_rope = (slab * cos_t + _rope_rot(slab) * sin_t)[:, :MLA_ROPE].astype(BF16)

    scale = MLA_QK ** -0.5 * LOG2_E
    q_nope = _dot(qn, wq_nope_ref[...]) * scale
    q_rope = _dot(qn, wq_rope_ref[...])
    reps = q_rope.shape[-1] // LANE
    q_rope = (q_rope * jnp.concatenate([cos_t] * reps, axis=-1)
              + _rope_rot(q_rope) * jnp.concatenate([sin_t] * reps, axis=-1)) * scale
    k_nope = _dot(ckv, wkv_k_ref[...])
    v = _dot(ckv, wkv_v_ref[...])
    for hd in range(MLA_HEADS):
        q_ref[0, hd, :, :MLA_NOPE] = q_nope[:, hd * MLA_NOPE:(hd + 1) * MLA_NOPE].astype(BF16)
        q_ref[0, hd, :, MLA_NOPE:] = q_rope[:, hd * MLA_ROPE:(hd + 1) * MLA_ROPE].astype(BF16)
        k_ref[0, hd, :, :MLA_NOPE] = k_nope[:, hd * MLA_NOPE:(hd + 1) * MLA_NOPE].astype(BF16)
        k_ref[0, hd, :, MLA_NOPE:] = k_rope
        v_ref[0, hd] = v[:, hd * MLA_V:(hd + 1) * MLA_V].astype(BF16)


def _pre_b_call(x, cos_sin, lng, lnb, w_a2, b_a2, qn_g, kvn_g,
                wq_nope, wq_rope, wkv_k, wkv_v, w_packed):
    bsz, seq, _ = x.shape
    tm = PRE_B_TM
    assert tm % CUMSUM_ROWS == 0 and CUMSUM_ROWS % GLA_CHUNK == 0
    row = lambda width: pl.BlockSpec((1, tm, width), lambda b, i: (b, i, 0))
    head = lambda width: pl.BlockSpec((1, MLA_HEADS, tm, width), lambda b, i: (b, 0, i, 0))
    consts = (lng, lnb, w_a2, b_a2, qn_g, kvn_g, wq_nope, wq_rope, wkv_k, wkv_v)
    out_shape = (
        jax.ShapeDtypeStruct((bsz, seq, GLA_DK), BF16),
        jax.ShapeDtypeStruct((bsz, seq, GLA_DK), BF16),
        jax.ShapeDtypeStruct((bsz, seq, GLA_DK), BF16),
        jax.ShapeDtypeStruct((bsz, seq // GLA_CHUNK, SUBLANE, GLA_DK), F32),
        jax.ShapeDtypeStruct((bsz, MLA_HEADS, seq, MLA_QK), BF16),
        jax.ShapeDtypeStruct((bsz, MLA_HEADS, seq, MLA_QK), BF16),
        jax.ShapeDtypeStruct((bsz, MLA_HEADS, seq, MLA_V), BF16),
    )
    dec_spec = pl.BlockSpec((1, tm // GLA_CHUNK, SUBLANE, GLA_DK), lambda b, i: (b, i, 0, 0))
    out_specs = (row(GLA_DK), row(GLA_DK), row(GLA_DK), dec_spec,
                 head(MLA_QK), head(MLA_QK), head(MLA_V))
    return pl.pallas_call(
        _pre_b_kernel,
        grid=(bsz, seq // tm),
        in_specs=[row(D_MODEL), pl.BlockSpec((1, LANE, tm), lambda b, i: (b, 0, i))]
        + [_const_spec(c.shape) for c in consts]
        + [_const_spec((_PACK_B, D_MODEL), (_PACK_A // _PACK_B, 0))],
        out_specs=out_specs,
        out_shape=out_shape,
        compiler_params=pltpu.CompilerParams(
            dimension_semantics=("parallel", "parallel"), vmem_limit_bytes=VMEM_LIMIT),
        name="pre_b",
    )(x, cos_sin, *consts, w_packed)


def _gla_kernel(qe_ref, ke_ref, kd_ref, dec_ref, v_ref, r_ref, g_ref, o_ref, st_ref):
    @pl.when(pl.program_id(1) == 0)
    def _():
        st_ref[...] = jnp.zeros_like(st_ref)

    c = GLA_CHUNK
    row = lax.broadcasted_iota(jnp.int32, (c, c), 0)
    col = lax.broadcasted_iota(jnp.int32, (c, c), 1)
    causal = row >= col

    def chunk(ci, carry):
        rows = pl.ds(pl.multiple_of(ci * c, c), c)
        for hd in range(GLA_HEADS):
            kcols = slice(hd * GLA_HK, (hd + 1) * GLA_HK)
            vcols = slice(hd * GLA_HV, (hd + 1) * GLA_HV)
            qe = qe_ref[0, rows, kcols]
            v = v_ref[0, rows, vcols]
            s = lax.dot_general(qe, ke_ref[0, rows, kcols], _NT, preferred_element_type=F32)
            p = jnp.where(causal, s, 0.0).astype(BF16)
            st = st_ref[hd]
            o = _dot(p, v) + lax.dot_general(qe, st.astype(BF16), _NT, preferred_element_type=F32)
            st_ref[hd] = st * dec_ref[0, ci, 0:1, kcols] + lax.dot_general(
                v, kd_ref[0, rows, kcols], _TN, preferred_element_type=F32)
            on = _rms_norm(o, g_ref[:, vcols])
            o_ref[0, rows, vcols] = (on * r_ref[0, rows, vcols].astype(F32)).astype(BF16)
        return carry

    n_chunks = qe_ref.shape[1] // c
    lax.fori_loop(0, n_chunks, chunk, 0, unroll=n_chunks)


def _gla_call(qe, ke, kd, dec, vg, rg, g):
    bsz, seq, _ = qe.shape
    tb = GLA_TB
    kspec = pl.BlockSpec((1, tb, GLA_DK), lambda b, i: (b, i, 0))
    vspec = pl.BlockSpec((1, tb, GLA_DV), lambda b, i: (b, i, 0))
    dspec = pl.BlockSpec((1, tb // GLA_CHUNK, SUBLANE, GLA_DK), lambda b, i: (b, i, 0, 0))
    return pl.pallas_call(
        _gla_kernel,
        grid=(bsz, seq // tb),
        in_specs=[kspec, kspec, kspec, dspec, vspec, vspec, _const_spec(g.shape)],
        out_specs=vspec,
        out_shape=jax.ShapeDtypeStruct((bsz, seq, GLA_DV), BF16),
        scratch_shapes=[pltpu.VMEM((GLA_HEADS, GLA_HV, GLA_HK), F32)],
        compiler_params=pltpu.CompilerParams(
            dimension_semantics=("parallel", "arbitrary"), vmem_limit_bytes=VMEM_LIMIT),
        name="gla",
    )(qe, ke, kd, dec, vg, rg, g)


def _mla_kernel(q_ref, k_ref, v_ref, o_ref, vx_ref):
    tq = MLA_TQ
    seq = v_ref.shape[2]
    ones_col = lax.broadcasted_iota(jnp.int32, (seq, MLA_V), 1) == 0
    for hd in range(MLA_HPS):
        vx_ref[hd, :, :MLA_V] = v_ref[0, hd]
        vx_ref[hd, :, MLA_V:] = jnp.where(ones_col, 1.0, 0.0).astype(BF16)

    row = lax.broadcasted_iota(jnp.int32, (tq, tq), 0)
    col = lax.broadcasted_iota(jnp.int32, (tq, tq), 1)
    causal = row >= col
    for qi in range(seq // tq):
        q0 = qi * tq
        for hd in range(MLA_HPS):
            q = q_ref[0, hd, q0:q0 + tq, :]
            s_diag = lax.dot_general(q, k_ref[0, hd, q0:q0 + tq, :], _NT,
                                     preferred_element_type=F32)
            s_diag = jnp.where(causal, s_diag, -jnp.inf)
            m = jnp.max(s_diag, axis=-1, keepdims=True)
            if qi > 0:
                s_off = lax.dot_general(q, k_ref[0, hd, :q0, :], _NT, preferred_element_type=F32)
                m = jnp.maximum(m, jnp.max(s_off, axis=-1, keepdims=True))
            acc = _dot(jnp.exp2(s_diag - m).astype(BF16), vx_ref[hd, q0:q0 + tq, :])
            if qi > 0:
                acc = acc + _dot(jnp.exp2(s_off - m).astype(BF16), vx_ref[hd, :q0, :])
            o_ref[0, q0:q0 + tq, hd * MLA_V:(hd + 1) * MLA_V] = (
                acc[:, :MLA_V] / acc[:, MLA_V:MLA_V + 1]).astype(BF16)


def _mla_call(q, k, v):
    bsz, nh, seq, _ = q.shape
    hps = MLA_HPS
    head = lambda width: pl.BlockSpec((1, hps, seq, width), lambda b, h: (b, h, 0, 0))
    return pl.pallas_call(
        _mla_kernel,
        grid=(bsz, nh // hps),
        in_specs=[head(MLA_QK), head(MLA_QK), head(MLA_V)],
        out_specs=pl.BlockSpec((1, seq, hps * MLA_V), lambda b, h: (b, 0, h)),
        out_shape=jax.ShapeDtypeStruct((bsz, seq, nh * MLA_V), BF16),
        scratch_shapes=[pltpu.VMEM((hps, seq, 2 * MLA_V), BF16)],
        compiler_params=pltpu.CompilerParams(
            dimension_semantics=("parallel", "parallel"), vmem_limit_bytes=VMEM_LIMIT),
        name="mla",
    )(q, k, v)


def _post_kernel(x_ref, a_ref, m_ref, gate_ref, lng_ref, lnb_ref, wog_ref, wom_ref, wout_ref,
                 ln1g_ref, ln1b_ref, h1_ref):
    y_gla = _dot(a_ref[...], wog_ref[...])
    y_mla = _dot(m_ref[...], wom_ref[...])
    gate = gate_ref[...].astype(F32)
    merged = gate[:, :D_MODEL] * y_gla + gate[:, D_MODEL:] * y_mla
    mix = _dot(merged.astype(BF16), wout_ref[...])
    h = _layer_norm(x_ref[...], lng_ref[...], lnb_ref[...])
    h1_ref[...] = _layer_norm(DEEPNORM_ALPHA * h + mix, ln1g_ref[...], ln1b_ref[...])


def _post_call(x2, act, omla, gate, lng, lnb, wog, wom, wout, ln1g, ln1b):
    n = x2.shape[0]
    tm = POST_TM
    row = lambda width: pl.BlockSpec((tm, width), lambda i: (i, 0))
    consts = (lng, lnb, wog, wom, wout, ln1g, ln1b)
    return pl.pallas_call(
        _post_kernel,
        grid=(n // tm,),
        in_specs=[row(D_MODEL), row(D_MODEL), row(D_MODEL), row(2 * D_MODEL)]
        + [_const_spec(c.shape) for c in consts],
        out_specs=row(D_MODEL),
        out_shape=jax.ShapeDtypeStruct((n, D_MODEL), F32),
        compiler_params=pltpu.CompilerParams(
            dimension_semantics=("parallel",), vmem_limit_bytes=VMEM_LIMIT),
        name="post",
    )(x2, act, omla, gate, *consts)


def _ffn_kernel(h1_ref, w1_ref, w2_ref, g_ref, b_ref, o_ref):
    h1 = h1_ref[...]
    hb = h1.astype(BF16)
    acc = jnp.zeros(h1.shape, F32)
    for f0 in range(0, D_FF, FFN_TF):
        a = jnp.maximum(_dot(hb, w1_ref[:, f0:f0 + FFN_TF]), 0.0)
        acc = acc + _dot((a * a).astype(BF16), w2_ref[f0:f0 + FFN_TF, :])
    o_ref[...] = _layer_norm(DEEPNORM_ALPHA * h1 + acc, g_ref[...], b_ref[...])


def _ffn_call(h1, w1, w2, g, b):
    n = h1.shape[0]
    tm = FFN_TM
    row = pl.BlockSpec((tm, D_MODEL), lambda i: (i, 0))
    consts = (w1, w2, g, b)
    return pl.pallas_call(
        _ffn_kernel,
        grid=(n // tm,),
        in_specs=[row] + [_const_spec(c.shape) for c in consts],
        out_specs=row,
        out_shape=jax.ShapeDtypeStruct((n, D_MODEL), F32),
        compiler_params=pltpu.CompilerParams(
            dimension_semantics=("parallel",), vmem_limit_bytes=VMEM_LIMIT),
        name="ffn",
    )(h1, *consts)


def kernel(x, positions, ln_in_g, ln_in_b, w_in, w_gla_a2, b_gla_a2, gla_norm_g, w_o_gla,
           q_a_norm_g, w_q_b, kv_a_norm_g, w_kv_b, w_o_mla, b_gate, w_out,
           ln1_g, ln1_b, w_ff1, w_ff2, ln2_g, ln2_b):
    assert DEPTH == 1 and w_in.shape[0] == 1
    bsz, seq, _ = x.shape
    n = bsz * seq
    row2 = lambda a: a.reshape(1, -1)

    inv_freq = 1.0 / (ROPE_THETA ** (jnp.arange(0, MLA_ROPE, 2, dtype=F32) / MLA_ROPE))
    ang = positions.astype(F32)[:, None, :] * inv_freq[None, :, None]
    cos_sin = jnp.concatenate(
        [jnp.cos(ang), jnp.sin(ang), jnp.zeros((bsz, LANE - MLA_ROPE, seq), F32)], axis=1)

    w_packed = _pack_call(w_in[0].T)
    slab_pad = LANE - MLA_ROPE - GLA_GATE_RANK
    w_a2 = jnp.pad(w_gla_a2[0], ((_SLAB_ALR, slab_pad), (0, 0))).astype(BF16)

    wq = w_q_b[0].reshape(MLA_Q_RANK, MLA_HEADS, MLA_QK)
    wq_nope = wq[:, :, :MLA_NOPE].reshape(MLA_Q_RANK, -1).astype(BF16)
    wq_rope = wq[:, :, MLA_NOPE:].reshape(MLA_Q_RANK, -1).astype(BF16)
    wkv = w_kv_b[0].reshape(MLA_KV_RANK, MLA_HEADS, MLA_NOPE + MLA_V)
    wkv_k = wkv[:, :, :MLA_NOPE].reshape(MLA_KV_RANK, -1).astype(BF16)
    wkv_v = wkv[:, :, MLA_NOPE:].reshape(MLA_KV_RANK, -1).astype(BF16)

    lng, lnb = row2(ln_in_g), row2(ln_in_b)
    vg, rg, gate = _pre_a_call(x, lng, lnb, row2(b_gate[0]), w_packed)
    qe, ke, kd, dec, q, k, v = _pre_b_call(
        x, cos_sin, lng, lnb, w_a2, row2(b_gla_a2[0]),
        row2(q_a_norm_g[0]), row2(kv_a_norm_g[0]), wq_nope, wq_rope, wkv_k, wkv_v, w_packed)

    act = _gla_call(qe, ke, kd, dec, vg, rg, row2(gla_norm_g[0]))
    omla = _mla_call(q, k, v)

    h1 = _post_call(
        x.reshape(n, D_MODEL), act.reshape(n, GLA_DV), omla.reshape(n, MLA_HEADS * MLA_V),
        gate.reshape(n, 2 * D_MODEL), lng, lnb,
        w_o_gla[0].astype(BF16), w_o_mla[0].astype(BF16), w_out[0].astype(BF16),
        row2(ln1_g[0]), row2(ln1_b[0]))

    out = _ffn_call(h1, w_ff1[0].astype(BF16), w_ff2[0].astype(BF16),
                    row2(ln2_g[0]), row2(ln2_b[0]))
    return out.reshape(bsz, seq, D_MODEL)
```

```python
import jax
import jax.numpy as jnp
from jax import lax
from jax.experimental import pallas as pl
from jax.experimental.pallas import tpu as pltpu

D_MODEL = 1024
DEPTH = 1
LN_EPS = 1e-5
RMS_EPS = 1e-6

GLA_HEADS = 4
GLA_DK = D_MODEL // 2
GLA_DV = D_MODEL
GLA_HK = GLA_DK // GLA_HEADS
GLA_HV = GLA_DV // GLA_HEADS
GLA_GATE_RANK = 16
GLA_TAU = 16.0

MLA_HEADS = 8
MLA_Q_RANK = 384
MLA_KV_RANK = 256
MLA_NOPE = 128
MLA_ROPE = 64
MLA_V = 128
MLA_QK = MLA_NOPE + MLA_ROPE
ROPE_THETA = 10000.0

LOG2_E = 1.4426950408889634

D_FF = 4 * D_MODEL
DEEPNORM_ALPHA = (2.0 * DEPTH) ** 0.25

_OFF_QG = 0
_OFF_KG = _OFF_QG + GLA_DK
_OFF_VG = _OFF_KG + GLA_DK
_OFF_RG = _OFF_VG + GLA_DV
_OFF_ALR = _OFF_RG + GLA_DV
_OFF_QLAT = _OFF_ALR + GLA_GATE_RANK
_OFF_KVLAT = _OFF_QLAT + MLA_Q_RANK
_OFF_KROPE = _OFF_KVLAT + MLA_KV_RANK
_OFF_GATE = _OFF_KROPE + MLA_ROPE
D_IN = _OFF_GATE + 2 * D_MODEL

LANE = 128
SUBLANE = 8
VMEM_LIMIT = 56 * 1024 * 1024

_SLAB0 = MLA_Q_RANK + MLA_KV_RANK
_SLAB_ALR = MLA_ROPE
assert _SLAB0 % LANE == 0 and _SLAB_ALR + GLA_GATE_RANK <= LANE
_LAT_W = _SLAB0 + LANE
_PACK_A = 2 * GLA_DV + 2 * D_MODEL
_PACK_B = _PACK_A // 2
assert 2 * GLA_DK + _LAT_W <= _PACK_B

PACK_COLS = 256
PRE_A_TM = 1024
PRE_B_TM = 1024
PRE_B_SPLIT = 1
CUMSUM_ROWS = 256
GLA_CHUNK = 64
_CHUNK_SHIFT = GLA_CHUNK.bit_length() - 1
assert 1 << _CHUNK_SHIFT == GLA_CHUNK
GLA_TB = 512
GLA_UNROLL = 8
MLA_TQ = 256
MLA_HPS = 4
POST_TM = 1024
POST_SPLIT = 2
FFN_TM = 1024
FFN_TF = 1024

_NT = (((1,), (1,)), ((), ()))
_TN = (((0,), (0,)), ((), ()))

BF16 = jnp.bfloat16
F32 = jnp.float32


def _dot(a, b):
    return jnp.dot(a, b, preferred_element_type=F32)


def _dot_t(a, b_t):
    return lax.dot_general(a, b_t, _NT, preferred_element_type=F32)


def _layer_norm(x, g, b):
    mu = jnp.mean(x, axis=-1, keepdims=True)
    xc = x - mu
    var = jnp.mean(xc * xc, axis=-1, keepdims=True)
    return xc * lax.rsqrt(var + LN_EPS) * g + b


def _rms_norm(x, g):
    return x * lax.rsqrt(jnp.mean(x * x, axis=-1, keepdims=True) + RMS_EPS) * g


def _sigmoid(x):
    return 1.0 / (1.0 + jnp.exp(-x))


def _rope_rot(x):
    width = x.shape[-1]
    half = MLA_ROPE // 2
    lane = lax.broadcasted_iota(jnp.int32, x.shape, 1)
    first_half = (lane & (MLA_ROPE - 1)) < half
    ahead = pltpu.roll(x, width - half, 1)
    behind = pltpu.roll(x, half, 1)
    return jnp.where(first_half, -ahead, behind)


def _const_spec(shape, index=None):
    index = (0,) * len(shape) if index is None else index
    return pl.BlockSpec(shape, lambda *_: index, pipeline_mode=pl.Buffered(1))


def _pack_kernel(wt_ref, o_ref):
    pieces = ((_OFF_VG, _OFF_ALR), (_OFF_GATE, D_IN), (_OFF_QG, _OFF_VG),
              (_OFF_QLAT, _OFF_GATE), (_OFF_ALR, _OFF_QLAT))
    off = 0
    for lo, hi in pieces:
        o_ref[off:off + hi - lo, :] = wt_ref[lo:hi, :].astype(BF16)
        off += hi - lo
    o_ref[off:, :] = jnp.zeros((o_ref.shape[0] - off, o_ref.shape[1]), BF16)


def _pack_call(w_t):
    cols = PACK_COLS
    return pl.pallas_call(
        _pack_kernel,
        grid=(D_MODEL // cols,),
        in_specs=[pl.BlockSpec((D_IN, cols), lambda i: (0, i))],
        out_specs=pl.BlockSpec((_PACK_A + _PACK_B, cols), lambda i: (0, i)),
        out_shape=jax.ShapeDtypeStruct((_PACK_A + _PACK_B, D_MODEL), BF16),
        compiler_params=pltpu.CompilerParams(
            dimension_semantics=("parallel",), vmem_limit_bytes=VMEM_LIMIT),
        name="pack",
    )(w_t)


def _pre_a_kernel(x_ref, lng_ref, lnb_ref, b_gate_ref, w_ref, hb_ref, vg_ref, rg_ref, gate_ref):
    hb = _layer_norm(x_ref[0], lng_ref[...], lnb_ref[...]).astype(BF16)
    hb_ref[0] = hb
    vr = _dot_t(hb, w_ref[:2 * GLA_DV, :])
    vg_ref[0] = vr[:, :GLA_DV].astype(BF16)
    r = vr[:, GLA_DV:]
    rg_ref[0] = (r * _sigmoid(r)).astype(BF16)
    gate_ref[0] = _sigmoid(_dot_t(hb, w_ref[2 * GLA_DV:, :]) + b_gate_ref[...]).astype(BF16)


def _pre_a_call(x, lng, lnb, b_gate, w_packed):
    bsz, seq, _ = x.shape
    tm = PRE_A_TM
    row = lambda width: pl.BlockSpec((1, tm, width), lambda b, i: (b, i, 0))
    consts = (lng, lnb, b_gate)
    return pl.pallas_call(
        _pre_a_kernel,
        grid=(bsz, seq // tm),
        in_specs=[row(D_MODEL)] + [_const_spec(c.shape) for c in consts]
        + [_const_spec((_PACK_A, D_MODEL), (0, 0))],
        out_specs=(row(D_MODEL), row(GLA_DV), row(GLA_DV), row(2 * D_MODEL)),
        out_shape=(jax.ShapeDtypeStruct((bsz, seq, D_MODEL), BF16),
                   jax.ShapeDtypeStruct((bsz, seq, GLA_DV), BF16),
                   jax.ShapeDtypeStruct((bsz, seq, GLA_DV), BF16),
                   jax.ShapeDtypeStruct((bsz, seq, 2 * D_MODEL), BF16)),
        compiler_params=pltpu.CompilerParams(
            dimension_semantics=("parallel", "parallel"), vmem_limit_bytes=VMEM_LIMIT),
        name="pre_a",
    )(x, *consts, w_packed)


def _pre_b_kernel(hb_ref, cs_ref, w_a2_ref,
                  b_a2_ref, qn_g_ref, kvn_g_ref, wq_nope_ref, wq_rope_ref, wkv_k_ref, wkv_v_ref,
                  w_ref, qe_ref, ke_ref, kd_ref, dec_ref, q_ref, k_ref, v_ref):
    n = CUMSUM_ROWS
    row_i = lax.broadcasted_iota(jnp.int32, (n, n), 0)
    col_i = lax.broadcasted_iota(jnp.int32, (n, n), 1)
    same_chunk = (lax.shift_right_logical(row_i, _CHUNK_SHIFT)
                  == lax.shift_right_logical(col_i, _CHUNK_SHIFT))
    tri = (same_chunk & (row_i >= col_i)).astype(BF16)
    half = MLA_ROPE // 2
    scale = MLA_QK ** -0.5 * LOG2_E
    sub = hb_ref.shape[1] // PRE_B_SPLIT

    def project(r0):
        hb = hb_ref[0, r0:r0 + sub, :]
        qk = _dot_t(hb, w_ref[:2 * GLA_DK, :])
        lat = _dot_t(hb, w_ref[2 * GLA_DK:2 * GLA_DK + _LAT_W, :])
        slab = lat[:, _SLAB0:]

        qn =(_rms_norm(lat[:, :MLA_Q_RANK], qn_g_ref[...]) * scale).astype(BF16)
        ckv = _rms_norm(lat[:, MLA_Q_RANK:_SLAB0], kvn_g_ref[...]).astype(BF16)
        q_nope = _dot(qn, wq_nope_ref[...])
        q_rope = _dot(qn, wq_rope_ref[...])
        k_nope = _dot(ckv, wkv_k_ref[...])
        v = _dot(ckv, wkv_v_ref[...])
        return qk, slab, q_nope, q_rope, k_nope, v

    def finish(r0, qk, slab, q_nope, q_rope, k_nope, v):
        rows = slice(r0, r0 + sub)
        z = _dot(slab.astype(BF16), w_a2_ref[...]) + b_a2_ref[...]
        la = (jnp.minimum(z, 0.0) - jnp.log(1.0 + jnp.exp(-jnp.abs(z)))) / GLA_TAU

        la_hi = la.astype(BF16)
        la_lo = (la - la_hi.astype(F32)).astype(BF16)
        b = jnp.concatenate(
            [_dot(tri, la_hi[c0:c0 + n]) + _dot(tri, la_lo[c0:c0 + n]) for c0 in range(0, sub, n)],
            axis=0)
        chunk_ends = range(GLA_CHUNK, sub + 1, GLA_CHUNK)
        bl = jnp.concatenate(
            [jnp.broadcast_to(b[e - 1:e], (GLA_CHUNK, GLA_DK)) for e in chunk_ends], axis=0)
        qs = qk[:, :GLA_DK] * GLA_HK ** -0.5
        kk = qk[:, GLA_DK:]
        qe_ref[0, rows, :] = (qs * jnp.exp(b)).astype(BF16)
        ke_ref[0, rows, :] = (kk * jnp.exp(-b)).astype(BF16)
        kd_ref[0, rows, :] = (kk * jnp.exp(bl - b)).astype(BF16)
        for ci, e in enumerate(chunk_ends):
            dec_ref[0, r0 // GLA_CHUNK + ci] = jnp.exp(
                jnp.broadcast_to(b[e - 1:e], (SUBLANE, GLA_DK)))

        cs = cs_ref[0, :, rows].T
        group = lax.shift_right_logical(lax.broadcasted_iota(jnp.int32, cs.shape, 1),
                                        half.bit_length() - 1)
        r1, r2, r3 = (pltpu.roll(cs, k * half, 1) for k in (1, 2, 3))
        cos_t = jnp.where(group == 0, cs, jnp.where(group == 1, r1, jnp.where(group == 2, r2, r3)))
        sin_t = jnp.where(group == 0, r3, jnp.where(group == 1, cs, jnp.where(group == 2, r1, r2)))
        k_rope = (slab * cos_t + _rope_rot(slab) * sin_t)[:, :MLA_ROPE].astype(BF16)

        reps = q_rope.shape[-1] // LANE
        q_rope = (q_rope * jnp.concatenate([cos_t] * reps, axis=-1)
                  + _rope_rot(q_rope) * jnp.concatenate([sin_t] * reps, axis=-1))
        for hd in range(MLA_HEADS):
            q_ref[0, hd, rows, :MLA_NOPE] = q_nope[:, hd * MLA_NOPE:(hd + 1) * MLA_NOPE].astype(BF16)
            q_ref[0, hd, rows, MLA_NOPE:] = q_rope[:, hd * MLA_ROPE:(hd + 1) * MLA_ROPE].astype(BF16)
            k_ref[0, hd, rows, :MLA_NOPE] = k_nope[:, hd * MLA_NOPE:(hd + 1) * MLA_NOPE].astype(BF16)
            k_ref[0, hd, rows, MLA_NOPE:] = k_rope
            v_ref[0, hd, rows, :] = v[:, hd * MLA_V:(hd + 1) * MLA_V].astype(BF16)

    starts = list(range(0, hb_ref.shape[1], sub))
    pending = project(starts[0])
    for t, r0 in enumerate(starts):
        current = pending
        if t + 1 < len(starts):
            pending = project(starts[t + 1])
        finish(r0, *current)


def _pre_b_call(hb, cos_sin, w_a2, b_a2, qn_g, kvn_g,
                wq_nope, wq_rope, wkv_k, wkv_v, w_packed):
    bsz, seq, _ = hb.shape
    tm = PRE_B_TM
    assert (tm // PRE_B_SPLIT) % CUMSUM_ROWS == 0 and CUMSUM_ROWS % GLA_CHUNK == 0
    row = lambda width: pl.BlockSpec((1, tm, width), lambda b, i: (b, i, 0))
    head = lambda width: pl.BlockSpec((1, MLA_HEADS, tm, width), lambda b, i: (b, 0, i, 0))
    consts = (w_a2, b_a2, qn_g, kvn_g, wq_nope, wq_rope, wkv_k, wkv_v)
    out_shape = (
        jax.ShapeDtypeStruct((bsz, seq, GLA_DK), BF16),
        jax.ShapeDtypeStruct((bsz, seq, GLA_DK), BF16),
        jax.ShapeDtypeStruct((bsz, seq, GLA_DK), BF16),
        jax.ShapeDtypeStruct((bsz, seq // GLA_CHUNK, SUBLANE, GLA_DK), F32),
        jax.ShapeDtypeStruct((bsz, MLA_HEADS, seq, MLA_QK), BF16),
        jax.ShapeDtypeStruct((bsz, MLA_HEADS, seq, MLA_QK), BF16),
        jax.ShapeDtypeStruct((bsz, MLA_HEADS, seq, MLA_V), BF16),
    )
    dec_spec = pl.BlockSpec((1, tm // GLA_CHUNK, SUBLANE, GLA_DK), lambda b, i: (b, i, 0, 0))
    out_specs = (row(GLA_DK), row(GLA_DK), row(GLA_DK), dec_spec,
                 head(MLA_QK), head(MLA_QK), head(MLA_V))
    return pl.pallas_call(
        _pre_b_kernel,
        grid=(bsz, seq // tm),
        in_specs=[row(D_MODEL), pl.BlockSpec((1, LANE, tm), lambda b, i: (b, 0, i))]
        + [_const_spec(c.shape) for c in consts]
        + [_const_spec((_PACK_B, D_MODEL), (_PACK_A // _PACK_B, 0))],
        out_specs=out_specs,
        out_shape=out_shape,
        compiler_params=pltpu.CompilerParams(
            dimension_semantics=("parallel", "parallel"), vmem_limit_bytes=VMEM_LIMIT),
        name="pre_b",
    )(hb, cos_sin, *consts, w_packed)


def _gla_kernel(qe_ref, ke_ref, kd_ref, dec_ref, v_ref, r_ref, g_ref, o_ref, st_ref):
    @pl.when(pl.program_id(1) == 0)
    def _():
        st_ref[...] = jnp.zeros_like(st_ref)

    c = GLA_CHUNK
    row = lax.broadcasted_iota(jnp.int32, (c, c), 0)
    col = lax.broadcasted_iota(jnp.int32, (c, c), 1)
    causal = row >= col

    n_chunks = qe_ref.shape[1] // c
    kcols = [slice(hd * GLA_HK, (hd + 1) * GLA_HK) for hd in range(GLA_HEADS)]
    vcols = [slice(hd * GLA_HV, (hd + 1) * GLA_HV) for hd in range(GLA_HEADS)]

    def intra_scores(ci):
        rows = pl.ds(pl.multiple_of(ci * c, c), c)
        ps = []
        for hd in range(GLA_HEADS):
            s = lax.dot_general(qe_ref[0, rows, kcols[hd]], ke_ref[0, rows, kcols[hd]], _NT,
                                preferred_element_type=F32)
            ps.append(jnp.where(causal, s, 0.0).astype(BF16))
        return tuple(ps)

    def chunk(ci, ps):
        ps_next = intra_scores(jnp.minimum(ci + 1, n_chunks - 1))
        rows = pl.ds(pl.multiple_of(ci * c, c), c)
        for hd in range(GLA_HEADS):
            qe = qe_ref[0, rows, kcols[hd]]
            v = v_ref[0, rows, vcols[hd]]
            st = st_ref[hd]
            o = _dot(ps[hd], v) + lax.dot_general(qe, st.astype(BF16), _NT,
                                                  preferred_element_type=F32)
            st_ref[hd] = st * dec_ref[0, ci, 0:1, kcols[hd]] + lax.dot_general(
                v, kd_ref[0, rows, kcols[hd]], _TN, preferred_element_type=F32)
            on = _rms_norm(o, g_ref[:, vcols[hd]])
            o_ref[0, rows, vcols[hd]] = (on * r_ref[0, rows, vcols[hd]].astype(F32)).astype(BF16)
        return ps_next

    lax.fori_loop(0, n_chunks, chunk, intra_scores(0), unroll=GLA_UNROLL)


def _gla_call(qe, ke, kd, dec, vg, rg, g):
    bsz, seq, _ = qe.shape
    tb = GLA_TB
    kspec = pl.BlockSpec((1, tb, GLA_DK), lambda b, i: (b, i, 0))
    vspec = pl.BlockSpec((1, tb, GLA_DV), lambda b, i: (b, i, 0))
    dspec = pl.BlockSpec((1, tb // GLA_CHUNK, SUBLANE, GLA_DK), lambda b, i: (b, i, 0, 0))
    return pl.pallas_call(
        _gla_kernel,
        grid=(bsz, seq // tb),
        in_specs=[kspec, kspec, kspec, dspec, vspec, vspec, _const_spec(g.shape)],
        out_specs=vspec,
        out_shape=jax.ShapeDtypeStruct((bsz, seq, GLA_DV), BF16),
        scratch_shapes=[pltpu.VMEM((GLA_HEADS, GLA_HV, GLA_HK), F32)],
        compiler_params=pltpu.CompilerParams(
            dimension_semantics=("parallel", "arbitrary"), vmem_limit_bytes=VMEM_LIMIT),
        name="gla",
    )(qe, ke, kd, dec, vg, rg, g)


def _mla_kernel(q_ref, k_ref, v_ref, o_ref, vx_ref):
    tq = MLA_TQ
    seq = v_ref.shape[2]
    ones_col = lax.broadcasted_iota(jnp.int32, (seq, MLA_V), 1) == 0
    for hd in range(MLA_HPS):
        vx_ref[hd, :, :MLA_V] = v_ref[0, hd]
        vx_ref[hd, :, MLA_V:] = jnp.where(ones_col, 1.0, 0.0).astype(BF16)

    row = lax.broadcasted_iota(jnp.int32, (tq, tq), 0)
    col = lax.broadcasted_iota(jnp.int32, (tq, tq), 1)
    causal = row >= col

    def scores(qi, hd):
        q0 = qi * tq
        q = q_ref[0, hd, q0:q0 + tq, :]
        s_diag = lax.dot_general(q, k_ref[0, hd, q0:q0 + tq, :], _NT, preferred_element_type=F32)
        s_diag = jnp.where(causal, s_diag, -jnp.inf)
        if qi == 0:
            return s_diag, None
        return s_diag, lax.dot_general(q, k_ref[0, hd, :q0, :], _NT, preferred_element_type=F32)

    def finish(qi, hd, s_diag, s_off):
        q0 = qi * tq
        m = jnp.max(s_diag, axis=-1, keepdims=True)
        if s_off is not None:
            m = jnp.maximum(m, jnp.max(s_off, axis=-1, keepdims=True))
        acc = _dot(jnp.exp2(s_diag - m).astype(BF16), vx_ref[hd, q0:q0 + tq, :])
        if s_off is not None:
            acc = acc + _dot(jnp.exp2(s_off - m).astype(BF16), vx_ref[hd, :q0, :])
        o_ref[0, q0:q0 + tq, hd * MLA_V:(hd + 1) * MLA_V] = (
            acc[:, :MLA_V] / acc[:, MLA_V:MLA_V + 1]).astype(BF16)

    items = [(qi, hd) for qi in range(seq // tq) for hd in range(MLA_HPS)]
    pending = scores(*items[0])
    for t, item in enumerate(items):
        current = pending
        if t + 1 < len(items):
            pending = scores(*items[t + 1])
        finish(*item, *current)


def _mla_call(q, k, v):
    bsz, nh, seq, _ = q.shape
    hps = MLA_HPS
    head = lambda width: pl.BlockSpec((1, hps, seq, width), lambda b, h: (b, h, 0, 0))
    return pl.pallas_call(
        _mla_kernel,
        grid=(bsz, nh // hps),
        in_specs=[head(MLA_QK), head(MLA_QK), head(MLA_V)],
        out_specs=pl.BlockSpec((1, seq, hps * MLA_V), lambda b, h: (b, 0, h)),
        out_shape=jax.ShapeDtypeStruct((bsz, seq, nh * MLA_V), BF16),
        scratch_shapes=[pltpu.VMEM((hps, seq, 2 * MLA_V), BF16)],
        compiler_params=pltpu.CompilerParams(
            dimension_semantics=("parallel", "parallel"), vmem_limit_bytes=VMEM_LIMIT),
        name="mla",
    )(q, k, v)


def _post_kernel(x_ref, a_ref, m_ref, gate_ref, lng_ref, lnb_ref, wog_ref, wom_ref, wout_ref,
                 ln1g_ref, ln1b_ref, h1_ref):
    sub = x_ref.shape[0] // POST_SPLIT
    for r0 in range(0, x_ref.shape[0], sub):
        rows = slice(r0, r0 + sub)
        y_gla = _dot(a_ref[rows, :], wog_ref[...])
        y_mla = _dot(m_ref[rows, :], wom_ref[...])
        gate = gate_ref[rows, :].astype(F32)
        merged = gate[:, :D_MODEL] * y_gla + gate[:, D_MODEL:] * y_mla
        mix = _dot(merged.astype(BF16), wout_ref[...])
        h = _layer_norm(x_ref[rows, :], lng_ref[...], lnb_ref[...])
        h1_ref[rows, :] = _layer_norm(DEEPNORM_ALPHA * h + mix, ln1g_ref[...], ln1b_ref[...])


def _post_call(x2, act, omla, gate, lng, lnb, wog, wom, wout, ln1g, ln1b):
    n = x2.shape[0]
    tm = POST_TM
    row = lambda width: pl.BlockSpec((tm, width), lambda i: (i, 0))
    consts = (lng, lnb, wog, wom, wout, ln1g, ln1b)
    return pl.pallas_call(
        _post_kernel,
        grid=(n // tm,),
        in_specs=[row(D_MODEL), row(D_MODEL), row(D_MODEL), row(2 * D_MODEL)]
        + [_const_spec(c.shape) for c in consts],
        out_specs=row(D_MODEL),
        out_shape=jax.ShapeDtypeStruct((n, D_MODEL), F32),
        compiler_params=pltpu.CompilerParams(
            dimension_semantics=("parallel",), vmem_limit_bytes=VMEM_LIMIT),
        name="post",
    )(x2, act, omla, gate, *consts)


def _ffn_kernel(h1_ref, w1_ref, w2_ref, g_ref, b_ref, o_ref):
    h1 = h1_ref[...]
    hb = h1.astype(BF16)
    acc = jnp.zeros(h1.shape, F32)
    for f0 in range(0, D_FF, FFN_TF):
        a = jnp.maximum(_dot(hb, w1_ref[:, f0:f0 + FFN_TF]), 0.0)
        acc = acc + _dot((a * a).astype(BF16), w2_ref[f0:f0 + FFN_TF, :])
    o_ref[...] = _layer_norm(DEEPNORM_ALPHA * h1 + acc, g_ref[...], b_ref[...])


def _ffn_call(h1, w1, w2, g, b):
    n = h1.shape[0]
    tm = FFN_TM
    row = pl.BlockSpec((tm, D_MODEL), lambda i: (i, 0))
    consts = (w1, w2, g, b)
    return pl.pallas_call(
        _ffn_kernel,
        grid=(n // tm,),
        in_specs=[row] + [_const_spec(c.shape) for c in consts],
        out_specs=row,
        out_shape=jax.ShapeDtypeStruct((n, D_MODEL), F32),
        compiler_params=pltpu.CompilerParams(
            dimension_semantics=("parallel",), vmem_limit_bytes=VMEM_LIMIT),
        name="ffn",
    )(h1, *consts)


def kernel(x, positions, ln_in_g, ln_in_b, w_in, w_gla_a2, b_gla_a2, gla_norm_g, w_o_gla,
           q_a_norm_g, w_q_b, kv_a_norm_g, w_kv_b, w_o_mla, b_gate, w_out,
           ln1_g, ln1_b, w_ff1, w_ff2, ln2_g, ln2_b):
    assert DEPTH == 1 and w_in.shape[0] == 1
    bsz, seq, _ = x.shape
    n = bsz * seq
    row2 = lambda a: a.reshape(1, -1)

    inv_freq = 1.0 / (ROPE_THETA ** (jnp.arange(0, MLA_ROPE, 2, dtype=F32) / MLA_ROPE))
    ang = positions.astype(F32)[:, None, :] * inv_freq[None, :, None]
    cos_sin = jnp.concatenate(
        [jnp.cos(ang), jnp.sin(ang), jnp.zeros((bsz, LANE - MLA_ROPE, seq), F32)], axis=1)

    w_packed = _pack_call(w_in[0].T)
    slab_pad = LANE - MLA_ROPE - GLA_GATE_RANK
    w_a2 = jnp.pad(w_gla_a2[0], ((_SLAB_ALR, slab_pad), (0, 0))).astype(BF16)

    wq = w_q_b[0].reshape(MLA_Q_RANK, MLA_HEADS, MLA_QK)
    wq_nope = wq[:, :, :MLA_NOPE].reshape(MLA_Q_RANK, -1).astype(BF16)
    wq_rope = wq[:, :, MLA_NOPE:].reshape(MLA_Q_RANK, -1).astype(BF16)
    wkv = w_kv_b[0].reshape(MLA_KV_RANK, MLA_HEADS, MLA_NOPE + MLA_V)
    wkv_k = wkv[:, :, :MLA_NOPE].reshape(MLA_KV_RANK, -1).astype(BF16)
    wkv_v = wkv[:, :, MLA_NOPE:].reshape(MLA_KV_RANK, -1).astype(BF16)

    lng, lnb = row2(ln_in_g), row2(ln_in_b)
    hb, vg, rg, gate = _pre_a_call(x, lng, lnb, row2(b_gate[0]), w_packed)
    qe, ke, kd, dec, q, k, v = _pre_b_call(
        hb, cos_sin, w_a2, row2(b_gla_a2[0]),
        row2(q_a_norm_g[0]), row2(kv_a_norm_g[0]), wq_nope, wq_rope, wkv_k, wkv_v, w_packed)

    act = _gla_call(qe, ke, kd, dec, vg, rg, row2(gla_norm_g[0]))
    omla = _mla_call(q, k, v)

    h1 = _post_call(
        x.reshape(n, D_MODEL), act.reshape(n, GLA_DV), omla.reshape(n, MLA_HEADS * MLA_V),
        gate.reshape(n, 2 * D_MODEL), lng, lnb,
        w_o_gla[0].astype(BF16), w_o_mla[0].astype(BF16), w_out[0].astype(BF16),
        row2(ln1_g[0]), row2(ln1_b[0]))

    out = _ffn_call(h1, w_ff1[0].astype(BF16), w_ff2[0].astype(BF16),
                    row2(ln2_g[0]), row2(ln2_b[0]))
    return out.reshape(bsz, seq, D_MODEL)
```

```python
import jax
import jax.numpy as jnp
from jax import lax
from jax.experimental import pallas as pl
from jax.experimental.pallas import tpu as pltpu

D_MODEL = 1024
DEPTH = 1
LN_EPS = 1e-5
RMS_EPS = 1e-6

GLA_HEADS = 4
GLA_DK = D_MODEL // 2
GLA_DV = D_MODEL
GLA_HK = GLA_DK // GLA_HEADS
GLA_HV = GLA_DV // GLA_HEADS
GLA_GATE_RANK = 16
GLA_TAU = 16.0

MLA_HEADS = 8
MLA_Q_RANK = 384
MLA_KV_RANK = 256
MLA_NOPE = 128
MLA_ROPE = 64
MLA_V = 128
MLA_QK = MLA_NOPE + MLA_ROPE
ROPE_THETA = 10000.0

LOG2_E = 1.4426950408889634

D_FF = 4 * D_MODEL
DEEPNORM_ALPHA = (2.0 * DEPTH) ** 0.25

_OFF_QG = 0
_OFF_KG = _OFF_QG + GLA_DK
_OFF_VG = _OFF_KG + GLA_DK
_OFF_RG = _OFF_VG + GLA_DV
_OFF_ALR = _OFF_RG + GLA_DV
_OFF_QLAT = _OFF_ALR + GLA_GATE_RANK
_OFF_KVLAT = _OFF_QLAT + MLA_Q_RANK
_OFF_KROPE = _OFF_KVLAT + MLA_KV_RANK
_OFF_GATE = _OFF_KROPE + MLA_ROPE
D_IN = _OFF_GATE + 2 * D_MODEL

LANE = 128
SUBLANE = 8
VMEM_LIMIT = 56 * 1024 * 1024

_SLAB0 = MLA_Q_RANK + MLA_KV_RANK
_SLAB_ALR = MLA_ROPE
assert _SLAB0 % LANE == 0 and _SLAB_ALR + GLA_GATE_RANK <= LANE
_LAT_W = _SLAB0 + LANE
_PACK_A = 2 * GLA_DV + 2 * D_MODEL
_PACK_B = _PACK_A // 2
assert 2 * GLA_DK + _LAT_W <= _PACK_B

PACK_COLS = 256
PRE_A_TM = 1024
PRE_B_TM = 1024
PRE_B_SPLIT = 1
CUMSUM_ROWS = 256
GLA_CHUNK = 64
_CHUNK_SHIFT = GLA_CHUNK.bit_length() - 1
assert 1 << _CHUNK_SHIFT == GLA_CHUNK
GLA_TB = 1024
MLA_TQ = 256
MLA_HPS = 4
POST_TM = 1024
POST_SPLIT = 2
FFN_TM = 1024
FFN_TF = 1024

_NT = (((1,), (1,)), ((), ()))
_TN = (((0,), (0,)), ((), ()))

BF16 = jnp.bfloat16
F32 = jnp.float32


def _dot(a, b):
    return jnp.dot(a, b, preferred_element_type=F32)


def _dot_t(a, b_t):
    return lax.dot_general(a, b_t, _NT, preferred_element_type=F32)


def _layer_norm(x, g, b):
    mu = jnp.mean(x, axis=-1, keepdims=True)
    xc = x - mu
    var = jnp.mean(xc * xc, axis=-1, keepdims=True)
    return xc * lax.rsqrt(var + LN_EPS) * g + b


def _rms_norm(x, g):
    return x * lax.rsqrt(jnp.mean(x * x, axis=-1, keepdims=True) + RMS_EPS) * g


def _sigmoid(x):
    return 0.5 * jnp.tanh(0.5 * x) + 0.5


def _rope_rot(x):
    width = x.shape[-1]
    half = MLA_ROPE // 2
    lane = lax.broadcasted_iota(jnp.int32, x.shape, 1)
    first_half = (lane & (MLA_ROPE - 1)) < half
    ahead = pltpu.roll(x, width - half, 1)
    behind = pltpu.roll(x, half, 1)
    return jnp.where(first_half, -ahead, behind)


def _const_spec(shape, index=None):
    index = (0,) * len(shape) if index is None else index
    return pl.BlockSpec(shape, lambda *_: index, pipeline_mode=pl.Buffered(1))


def _pack_kernel(wt_ref, o_ref):
    pieces = ((_OFF_VG, _OFF_ALR), (_OFF_GATE, D_IN), (_OFF_QG, _OFF_VG),
              (_OFF_QLAT, _OFF_GATE), (_OFF_ALR, _OFF_QLAT))
    off = 0
    for lo, hi in pieces:
        o_ref[off:off + hi - lo, :] = wt_ref[lo:hi, :].astype(BF16)
        off += hi - lo
    o_ref[off:, :] = jnp.zeros((o_ref.shape[0] - off, o_ref.shape[1]), BF16)


def _pack_call(w_t):
    cols = PACK_COLS
    return pl.pallas_call(
        _pack_kernel,
        grid=(D_MODEL // cols,),
        in_specs=[pl.BlockSpec((D_IN, cols), lambda i: (0, i))],
        out_specs=pl.BlockSpec((_PACK_A + _PACK_B, cols), lambda i: (0, i)),
        out_shape=jax.ShapeDtypeStruct((_PACK_A + _PACK_B, D_MODEL), BF16),
        compiler_params=pltpu.CompilerParams(
            dimension_semantics=("parallel",), vmem_limit_bytes=VMEM_LIMIT),
        name="pack",
    )(w_t)


def _pre_a_kernel(x_ref, lng_ref, lnb_ref, b_gate_ref, w_ref, *refs):
    n_side = (len(refs) - 4) // 2
    side_in, (hb_ref, vg_ref, rg_ref, gate_ref), side_out = (
        refs[:n_side], refs[n_side:n_side + 4], refs[n_side + 4:])
    hb = _layer_norm(x_ref[0], lng_ref[...], lnb_ref[...]).astype(BF16)
    hb_ref[0] = hb
    vr = _dot_t(hb, w_ref[:2 * GLA_DV, :])
    vg_ref[0] = vr[:, :GLA_DV].astype(BF16)
    r = vr[:, GLA_DV:]
    rg_ref[0] = (r * _sigmoid(r)).astype(BF16)
    gate_ref[0] = _sigmoid(_dot_t(hb, w_ref[2 * GLA_DV:, :]) + b_gate_ref[...]).astype(BF16)
    for src, dst in zip(side_in, side_out):
        dst[...] = src[0].astype(BF16)


def _pre_a_call(x, lng, lnb, b_gate, w_packed, side_weights):
    bsz, seq, _ = x.shape
    tm = PRE_A_TM
    n_i = seq // tm
    steps = bsz * n_i
    row = lambda width: pl.BlockSpec((1, tm, width), lambda b, i: (b, i, 0))
    consts = (lng, lnb, b_gate)
    side_in_specs, side_out_specs, side_shapes = [], [], []
    for w in side_weights:
        _, rows, cols = w.shape
        slab = rows // steps
        assert slab * steps == rows and slab % (2 * SUBLANE) == 0
        side_in_specs.append(pl.BlockSpec((1, slab, cols), lambda b, i: (0, b * n_i + i, 0)))
        side_out_specs.append(pl.BlockSpec((slab, cols), lambda b, i: (b * n_i + i, 0)))
        side_shapes.append(jax.ShapeDtypeStruct((rows, cols), BF16))
    outs = pl.pallas_call(
        _pre_a_kernel,
        grid=(bsz, n_i),
        in_specs=[row(D_MODEL)] + [_const_spec(c.shape) for c in consts]
        + [_const_spec((_PACK_A, D_MODEL), (0, 0))] + side_in_specs,
        out_specs=[row(D_MODEL), row(GLA_DV), row(GLA_DV), row(2 * D_MODEL)] + side_out_specs,
        out_shape=[jax.ShapeDtypeStruct((bsz, seq, D_MODEL), BF16),
                   jax.ShapeDtypeStruct((bsz, seq, GLA_DV), BF16),
                   jax.ShapeDtypeStruct((bsz, seq, GLA_DV), BF16),
                   jax.ShapeDtypeStruct((bsz, seq, 2 * D_MODEL), BF16)]
        + side_shapes,
        compiler_params=pltpu.CompilerParams(
            dimension_semantics=("parallel", "parallel"), vmem_limit_bytes=VMEM_LIMIT),
        name="pre_a",
    )(x, *consts, w_packed, *side_weights)
    return outs[:4], outs[4:]


def _pre_b_kernel(hb_ref, cs_ref, w_a2_ref,
                  b_a2_ref, qn_g_ref, kvn_g_ref, wq_nope_ref, wq_rope_ref, wkv_k_ref, wkv_v_ref,
                  w_ref, qe_ref, ke_ref, kd_ref, dec_ref, q_ref, k_ref, v_ref):
    n = CUMSUM_ROWS
    row_i = lax.broadcasted_iota(jnp.int32, (n, n), 0)
    col_i = lax.broadcasted_iota(jnp.int32, (n, n), 1)
    same_chunk = (lax.shift_right_logical(row_i, _CHUNK_SHIFT)
                  == lax.shift_right_logical(col_i, _CHUNK_SHIFT))
    tri = (same_chunk & (row_i >= col_i)).astype(BF16)
    half = MLA_ROPE // 2
    scale = MLA_QK ** -0.5 * LOG2_E
    sub = hb_ref.shape[1] // PRE_B_SPLIT

    def project(r0):
        hb = hb_ref[0, r0:r0 + sub, :]
        qk = _dot_t(hb, w_ref[:2 * GLA_DK, :])
        lat = _dot_t(hb, w_ref[2 * GLA_DK:2 * GLA_DK + _LAT_W, :])
        slab = lat[:, _SLAB0:]

        qn =(_rms_norm(lat[:, :MLA_Q_RANK], qn_g_ref[...]) * scale).astype(BF16)
        ckv = _rms_norm(lat[:, MLA_Q_RANK:_SLAB0], kvn_g_ref[...]).astype(BF16)
        q_nope = _dot(qn, wq_nope_ref[...])
        q_rope = _dot(qn, wq_rope_ref[...])
        k_nope = _dot(ckv, wkv_k_ref[...])
        v = _dot(ckv, wkv_v_ref[...])
        return qk, slab, q_nope, q_rope, k_nope, v

    def finish(r0, qk, slab, q_nope, q_rope, k_nope, v):
        rows = slice(r0, r0 + sub)
        z = _dot(slab.astype(BF16), w_a2_ref[...]) + b_a2_ref[...]
        la = (jnp.minimum(z, 0.0) - jnp.log(1.0 + jnp.exp(-jnp.abs(z)))) / GLA_TAU

        la_hi = la.astype(BF16)
        la_lo = (la - la_hi.astype(F32)).astype(BF16)
        b = jnp.concatenate(
            [_dot(tri, la_hi[c0:c0 + n]) + _dot(tri, la_lo[c0:c0 + n]) for c0 in range(0, sub, n)],
            axis=0)
        chunk_ends = range(GLA_CHUNK, sub + 1, GLA_CHUNK)
        bl = jnp.concatenate(
            [jnp.broadcast_to(b[e - 1:e], (GLA_CHUNK, GLA_DK)) for e in chunk_ends], axis=0)
        qs = qk[:, :GLA_DK] * GLA_HK ** -0.5
        kk = qk[:, GLA_DK:]
        qe_ref[0, rows, :] = (qs * jnp.exp(b)).astype(BF16)
        ke_ref[0, rows, :] = (kk * jnp.exp(-b)).astype(BF16)
        kd_ref[0, rows, :] = (kk * jnp.exp(bl - b)).astype(BF16)
        for ci, e in enumerate(chunk_ends):
            dec_ref[0, r0 // GLA_CHUNK + ci] = jnp.exp(
                jnp.broadcast_to(b[e - 1:e], (SUBLANE, GLA_DK)))

        cs = cs_ref[0, :, rows].T
        group = lax.shift_right_logical(lax.broadcasted_iota(jnp.int32, cs.shape, 1),
                                        half.bit_length() - 1)
        r1, r2, r3 = (pltpu.roll(cs, k * half, 1) for k in (1, 2, 3))
        cos_t = jnp.where(group == 0, cs, jnp.where(group == 1, r1, jnp.where(group == 2, r2, r3)))
        sin_t = jnp.where(group == 0, r3, jnp.where(group == 1, cs, jnp.where(group == 2, r1, r2)))
        k_rope = (slab * cos_t + _rope_rot(slab) * sin_t)[:, :MLA_ROPE].astype(BF16)

        reps = q_rope.shape[-1] // LANE
        q_rope = (q_rope * jnp.concatenate([cos_t] * reps, axis=-1)
                  + _rope_rot(q_rope) * jnp.concatenate([sin_t] * reps, axis=-1))
        for hd in range(MLA_HEADS):
            q_ref[0, hd, rows, :MLA_NOPE] = q_nope[:, hd * MLA_NOPE:(hd + 1) * MLA_NOPE].astype(BF16)
            q_ref[0, hd, rows, MLA_NOPE:] = q_rope[:, hd * MLA_ROPE:(hd + 1) * MLA_ROPE].astype(BF16)
            k_ref[0, hd, rows, :MLA_NOPE] = k_nope[:, hd * MLA_NOPE:(hd + 1) * MLA_NOPE].astype(BF16)
            k_ref[0, hd, rows, MLA_NOPE:] = k_rope
            v_ref[0, hd, rows, :] = v[:, hd * MLA_V:(hd + 1) * MLA_V].astype(BF16)

    starts = list(range(0, hb_ref.shape[1], sub))
    pending = project(starts[0])
    for t, r0 in enumerate(starts):
        current = pending
        if t + 1 < len(starts):
            pending = project(starts[t + 1])
        finish(r0, *current)


def _pre_b_call(hb, cos_sin, w_a2, b_a2, qn_g, kvn_g,
                wq_nope, wq_rope, wkv_k, wkv_v, w_packed):
    bsz, seq, _ = hb.shape
    tm = PRE_B_TM
    assert (tm // PRE_B_SPLIT) % CUMSUM_ROWS == 0 and CUMSUM_ROWS % GLA_CHUNK == 0
    row = lambda width: pl.BlockSpec((1, tm, width), lambda b, i: (b, i, 0))
    head = lambda width: pl.BlockSpec((1, MLA_HEADS, tm, width), lambda b, i: (b, 0, i, 0))
    consts = (w_a2, b_a2, qn_g, kvn_g, wq_nope, wq_rope, wkv_k, wkv_v)
    out_shape = (
        jax.ShapeDtypeStruct((bsz, seq, GLA_DK), BF16),
        jax.ShapeDtypeStruct((bsz, seq, GLA_DK), BF16),
        jax.ShapeDtypeStruct((bsz, seq, GLA_DK), BF16),
        jax.ShapeDtypeStruct((bsz, seq // GLA_CHUNK, SUBLANE, GLA_DK), F32),
        jax.ShapeDtypeStruct((bsz, MLA_HEADS, seq, MLA_QK), BF16),
        jax.ShapeDtypeStruct((bsz, MLA_HEADS, seq, MLA_QK), BF16),
        jax.ShapeDtypeStruct((bsz, MLA_HEADS, seq, MLA_V), BF16),
    )
    dec_spec = pl.BlockSpec((1, tm // GLA_CHUNK, SUBLANE, GLA_DK), lambda b, i: (b, i, 0, 0))
    out_specs = (row(GLA_DK), row(GLA_DK), row(GLA_DK), dec_spec,
                 head(MLA_QK), head(MLA_QK), head(MLA_V))
    return pl.pallas_call(
        _pre_b_kernel,
        grid=(bsz, seq // tm),
        in_specs=[row(D_MODEL), pl.BlockSpec((1, LANE, tm), lambda b, i: (b, 0, i))]
        + [_const_spec(c.shape) for c in consts]
        + [_const_spec((_PACK_B, D_MODEL), (_PACK_A // _PACK_B, 0))],
        out_specs=out_specs,
        out_shape=out_shape,
        compiler_params=pltpu.CompilerParams(
            dimension_semantics=("parallel", "parallel"), vmem_limit_bytes=VMEM_LIMIT),
        name="pre_b",
    )(hb, cos_sin, *consts, w_packed)


def _gla_kernel(qe_ref, ke_ref, kd_ref, dec_ref, v_ref, r_ref, g_ref, o_ref, st_ref):
    @pl.when(pl.program_id(1) == 0)
    def _():
        st_ref[...] = jnp.zeros_like(st_ref)

    c = GLA_CHUNK
    row = lax.broadcasted_iota(jnp.int32, (c, c), 0)
    col = lax.broadcasted_iota(jnp.int32, (c, c), 1)
    causal = row >= col

    n_chunks = qe_ref.shape[1] // c
    kcols = [slice(hd * GLA_HK, (hd + 1) * GLA_HK) for hd in range(GLA_HEADS)]
    vcols = [slice(hd * GLA_HV, (hd + 1) * GLA_HV) for hd in range(GLA_HEADS)]

    def intra_scores(ci):
        rows = slice(ci * c, (ci + 1) * c)
        ps = []
        for hd in range(GLA_HEADS):
            s = lax.dot_general(qe_ref[0, rows, kcols[hd]], ke_ref[0, rows, kcols[hd]], _NT,
                                preferred_element_type=F32)
            ps.append(jnp.where(causal, s, 0.0).astype(BF16))
        return ps

    def outputs_and_state(ci, ps):
        rows = slice(ci * c, (ci + 1) * c)
        os = []
        for hd in range(GLA_HEADS):
            qe = qe_ref[0, rows, kcols[hd]]
            v = v_ref[0, rows, vcols[hd]]
            st = st_ref[hd]
            os.append(_dot(ps[hd], v) + lax.dot_general(qe, st.astype(BF16), _NT,
                                                        preferred_element_type=F32))
            st_ref[hd] = st * dec_ref[0, ci, 0:1, kcols[hd]] + lax.dot_general(
                v, kd_ref[0, rows, kcols[hd]], _TN, preferred_element_type=F32)
        return os

    def norm_and_store(ci, os):
        rows = slice(ci * c, (ci + 1) * c)
        for hd in range(GLA_HEADS):
            on = _rms_norm(os[hd], g_ref[:, vcols[hd]])
            o_ref[0, rows, vcols[hd]] = (on * r_ref[0, rows, vcols[hd]].astype(F32)).astype(BF16)

    ps, os = {}, {}
    for t in range(n_chunks + 2):
        if t < n_chunks:
            ps[t] = intra_scores(t)
        if 1 <= t <= n_chunks:
            os[t - 1] = outputs_and_state(t - 1, ps.pop(t - 1))
        if t >= 2:
            norm_and_store(t - 2, os.pop(t - 2))


def _gla_call(qe, ke, kd, dec, vg, rg, g):
    bsz, seq, _ = qe.shape
    tb = GLA_TB
    kspec = pl.BlockSpec((1, tb, GLA_DK), lambda b, i: (b, i, 0))
    vspec = pl.BlockSpec((1, tb, GLA_DV), lambda b, i: (b, i, 0))
    dspec = pl.BlockSpec((1, tb // GLA_CHUNK, SUBLANE, GLA_DK), lambda b, i: (b, i, 0, 0))
    return pl.pallas_call(
        _gla_kernel,
        grid=(bsz, seq // tb),
        in_specs=[kspec, kspec, kspec, dspec, vspec, vspec, _const_spec(g.shape)],
        out_specs=vspec,
        out_shape=jax.ShapeDtypeStruct((bsz, seq, GLA_DV), BF16),
        scratch_shapes=[pltpu.VMEM((GLA_HEADS, GLA_HV, GLA_HK), F32)],
        compiler_params=pltpu.CompilerParams(
            dimension_semantics=("parallel", "arbitrary"), vmem_limit_bytes=VMEM_LIMIT),
        name="gla",
    )(qe, ke, kd, dec, vg, rg, g)


def _mla_kernel(q_ref, k_ref, v_ref, o_ref, vx_ref):
    tq = MLA_TQ
    seq = v_ref.shape[2]
    ones_col = lax.broadcasted_iota(jnp.int32, (seq, MLA_V), 1) == 0
    for hd in range(MLA_HPS):
        vx_ref[hd, :, :MLA_V] = v_ref[0, hd]
        vx_ref[hd, :, MLA_V:] = jnp.where(ones_col, 1.0, 0.0).astype(BF16)

    row = lax.broadcasted_iota(jnp.int32, (tq, tq), 0)
    col = lax.broadcasted_iota(jnp.int32, (tq, tq), 1)
    causal = row >= col

    def scores(qi, hd):
        q0 = qi * tq
        q = q_ref[0, hd, q0:q0 + tq, :]
        s_diag = lax.dot_general(q, k_ref[0, hd, q0:q0 + tq, :], _NT, preferred_element_type=F32)
        s_diag = jnp.where(causal, s_diag, -jnp.inf)
        if qi == 0:
            return s_diag, None
        return s_diag, lax.dot_general(q, k_ref[0, hd, :q0, :], _NT, preferred_element_type=F32)

    def finish(qi, hd, s_diag, s_off):
        q0 = qi * tq
        m = jnp.max(s_diag, axis=-1, keepdims=True)
        if s_off is not None:
            m = jnp.maximum(m, jnp.max(s_off, axis=-1, keepdims=True))
        acc = _dot(jnp.exp2(s_diag - m).astype(BF16), vx_ref[hd, q0:q0 + tq, :])
        if s_off is not None:
            acc = acc + _dot(jnp.exp2(s_off - m).astype(BF16), vx_ref[hd, :q0, :])
        o_ref[0, q0:q0 + tq, hd * MLA_V:(hd + 1) * MLA_V] = (
            acc[:, :MLA_V] / acc[:, MLA_V:MLA_V + 1]).astype(BF16)

    items = [(qi, hd) for qi in range(seq // tq) for hd in range(MLA_HPS)]
    pending = scores(*items[0])
    for t, item in enumerate(items):
        current = pending
        if t + 1 < len(items):
            pending = scores(*items[t + 1])
        finish(*item, *current)


def _mla_call(q, k, v):
    bsz, nh, seq, _ = q.shape
    hps = MLA_HPS
    head = lambda width: pl.BlockSpec((1, hps, seq, width), lambda b, h: (b, h, 0, 0))
    return pl.pallas_call(
        _mla_kernel,
        grid=(bsz, nh // hps),
        in_specs=[head(MLA_QK), head(MLA_QK), head(MLA_V)],
        out_specs=pl.BlockSpec((1, seq, hps * MLA_V), lambda b, h: (b, 0, h)),
        out_shape=jax.ShapeDtypeStruct((bsz, seq, nh * MLA_V), BF16),
        scratch_shapes=[pltpu.VMEM((hps, seq, 2 * MLA_V), BF16)],
        compiler_params=pltpu.CompilerParams(
            dimension_semantics=("parallel", "parallel"), vmem_limit_bytes=VMEM_LIMIT),
        name="mla",
    )(q, k, v)


def _post_kernel(x_ref, a_ref, m_ref, gate_ref, lng_ref, lnb_ref, wog_ref, wom_ref, wout_ref,
                 ln1g_ref, ln1b_ref, h1_ref):
    sub = x_ref.shape[0] // POST_SPLIT
    for r0 in range(0, x_ref.shape[0], sub):
        rows = slice(r0, r0 + sub)
        y_gla = _dot(a_ref[rows, :], wog_ref[...])
        y_mla = _dot(m_ref[rows, :], wom_ref[...])
        gate = gate_ref[rows, :].astype(F32)
        merged = gate[:, :D_MODEL] * y_gla + gate[:, D_MODEL:] * y_mla
        mix = _dot(merged.astype(BF16), wout_ref[...])
        h = _layer_norm(x_ref[rows, :], lng_ref[...], lnb_ref[...])
        h1_ref[rows, :] = _layer_norm(DEEPNORM_ALPHA * h + mix, ln1g_ref[...], ln1b_ref[...])


def _post_call(x2, act, omla, gate, lng, lnb, wog, wom, wout, ln1g, ln1b):
    n = x2.shape[0]
    tm = POST_TM
    row = lambda width: pl.BlockSpec((tm, width), lambda i: (i, 0))
    consts = (lng, lnb, wog, wom, wout, ln1g, ln1b)
    return pl.pallas_call(
        _post_kernel,
        grid=(n // tm,),
        in_specs=[row(D_MODEL), row(D_MODEL), row(D_MODEL), row(2 * D_MODEL)]
        + [_const_spec(c.shape) for c in consts],
        out_specs=row(D_MODEL),
        out_shape=jax.ShapeDtypeStruct((n, D_MODEL), F32),
        compiler_params=pltpu.CompilerParams(
            dimension_semantics=("parallel",), vmem_limit_bytes=VMEM_LIMIT),
        name="post",
    )(x2, act, omla, gate, *consts)


def _ffn_kernel(h1_ref, w1_ref, w2_ref, g_ref, b_ref, o_ref):
    h1 = h1_ref[...]
    hb = h1.astype(BF16)
    acc = jnp.zeros(h1.shape, F32)
    for f0 in range(0, D_FF, FFN_TF):
        a = jnp.maximum(_dot(hb, w1_ref[:, f0:f0 + FFN_TF]), 0.0)
        acc = acc + _dot((a * a).astype(BF16), w2_ref[f0:f0 + FFN_TF, :])
    o_ref[...] = _layer_norm(DEEPNORM_ALPHA * h1 + acc, g_ref[...], b_ref[...])


def _ffn_call(h1, w1, w2, g, b):
    n = h1.shape[0]
    tm = FFN_TM
    row = pl.BlockSpec((tm, D_MODEL), lambda i: (i, 0))
    consts = (w1, w2, g, b)
    return pl.pallas_call(
        _ffn_kernel,
        grid=(n // tm,),
        in_specs=[row] + [_const_spec(c.shape) for c in consts],
        out_specs=row,
        out_shape=jax.ShapeDtypeStruct((n, D_MODEL), F32),
        compiler_params=pltpu.CompilerParams(
            dimension_semantics=("parallel",), vmem_limit_bytes=VMEM_LIMIT),
        name="ffn",
    )(h1, *consts)


def kernel(x, positions, ln_in_g, ln_in_b, w_in, w_gla_a2, b_gla_a2, gla_norm_g, w_o_gla,
           q_a_norm_g, w_q_b, kv_a_norm_g, w_kv_b, w_o_mla, b_gate, w_out,
           ln1_g, ln1_b, w_ff1, w_ff2, ln2_g, ln2_b):
    assert DEPTH == 1 and w_in.shape[0] == 1
    bsz, seq, _ = x.shape
    n = bsz * seq
    row2 = lambda a: a.reshape(1, -1)

    inv_freq = 1.0 / (ROPE_THETA ** (jnp.arange(0, MLA_ROPE, 2, dtype=F32) / MLA_ROPE))
    ang = positions.astype(F32)[:, None, :] * inv_freq[None, :, None]
    cos_sin = jnp.concatenate(
        [jnp.cos(ang), jnp.sin(ang), jnp.zeros((bsz, LANE - MLA_ROPE, seq), F32)], axis=1)

    w_packed = _pack_call(w_in[0].T)
    slab_pad = LANE - MLA_ROPE - GLA_GATE_RANK
    w_a2 = jnp.pad(w_gla_a2[0], ((_SLAB_ALR, slab_pad), (0, 0))).astype(BF16)

    wq = w_q_b[0].reshape(MLA_Q_RANK, MLA_HEADS, MLA_QK)
    wq_nope = wq[:, :, :MLA_NOPE].reshape(MLA_Q_RANK, -1).astype(BF16)
    wq_rope = wq[:, :, MLA_NOPE:].reshape(MLA_Q_RANK, -1).astype(BF16)
    wkv = w_kv_b[0].reshape(MLA_KV_RANK, MLA_HEADS, MLA_NOPE + MLA_V)
    wkv_k = wkv[:, :, :MLA_NOPE].reshape(MLA_KV_RANK, -1).astype(BF16)
    wkv_v = wkv[:, :, MLA_NOPE:].reshape(MLA_KV_RANK, -1).astype(BF16)

    lng, lnb = row2(ln_in_g), row2(ln_in_b)
    (hb, vg, rg, gate), (wog, wom, wout, w1, w2) = _pre_a_call(
        x, lng, lnb, row2(b_gate[0]), w_packed, (w_o_gla, w_o_mla, w_out, w_ff1, w_ff2))
    qe, ke, kd, dec, q, k, v = _pre_b_call(
        hb, cos_sin, w_a2, row2(b_gla_a2[0]),
        row2(q_a_norm_g[0]), row2(kv_a_norm_g[0]), wq_nope, wq_rope, wkv_k, wkv_v, w_packed)

    act = _gla_call(qe, ke, kd, dec, vg, rg, row2(gla_norm_g[0]))
    omla = _mla_call(q, k, v)

    h1 = _post_call(
        x.reshape(n, D_MODEL), act.reshape(n, GLA_DV), omla.reshape(n, MLA_HEADS * MLA_V),
        gate.reshape(n, 2 * D_MODEL), lng, lnb, wog, wom, wout,
        row2(ln1_g[0]), row2(ln1_b[0]))

    out = _ffn_call(h1, w1, w2, row2(ln2_g[0]), row2(ln2_b[0]))
    return out.reshape(bsz, seq, D_MODEL)
```

```python
import jax
import jax.numpy as jnp
from jax import lax
from jax.experimental import pallas as pl
from jax.experimental.pallas import tpu as pltpu

D_MODEL = 1024
DEPTH = 1
LN_EPS = 1e-5
RMS_EPS = 1e-6

GLA_HEADS = 4
GLA_DK = D_MODEL // 2
GLA_DV = D_MODEL
GLA_HK = GLA_DK // GLA_HEADS
GLA_HV = GLA_DV // GLA_HEADS
GLA_GATE_RANK = 16
GLA_TAU = 16.0

MLA_HEADS = 8
MLA_Q_RANK = 384
MLA_KV_RANK = 256
MLA_NOPE = 128
MLA_ROPE = 64
MLA_V = 128
MLA_QK = MLA_NOPE + MLA_ROPE
ROPE_THETA = 10000.0

LOG2_E = 1.4426950408889634

D_FF = 4 * D_MODEL
DEEPNORM_ALPHA = (2.0 * DEPTH) ** 0.25

_OFF_QG = 0
_OFF_KG = _OFF_QG + GLA_DK
_OFF_VG = _OFF_KG + GLA_DK
_OFF_RG = _OFF_VG + GLA_DV
_OFF_ALR = _OFF_RG + GLA_DV
_OFF_QLAT = _OFF_ALR + GLA_GATE_RANK
_OFF_KVLAT = _OFF_QLAT + MLA_Q_RANK
_OFF_KROPE = _OFF_KVLAT + MLA_KV_RANK
_OFF_GATE = _OFF_KROPE + MLA_ROPE
D_IN = _OFF_GATE + 2 * D_MODEL

LANE = 128
SUBLANE = 8
VMEM_LIMIT = 60 * 1024 * 1024

_SLAB0 = MLA_Q_RANK + MLA_KV_RANK
_SLAB_ALR = MLA_ROPE
assert _SLAB0 % LANE == 0 and _SLAB_ALR + GLA_GATE_RANK <= LANE
_LAT_W = _SLAB0 + LANE
_PACK_A = 2 * GLA_DV + 2 * D_MODEL
_PACK_B = _PACK_A // 2
assert 2 * GLA_DK + _LAT_W <= _PACK_B

PACK_COLS = 256
PRE_A_TM = 1024
PRE_B_TM = 1024
PRE_B_SPLIT = 1
CUMSUM_ROWS = 256
GLA_CHUNK = 64
_CHUNK_SHIFT = GLA_CHUNK.bit_length() - 1
assert 1 << _CHUNK_SHIFT == GLA_CHUNK
GLA_TB = 1024
MLA_TQ = 256
MLA_HPS = 4
MLA_AHEAD = 1
POST_TM = 1024
POST_SPLIT = 4
FFN_TM = 1024
FFN_TF = 1024

_NT = (((1,), (1,)), ((), ()))
_TN = (((0,), (0,)), ((), ()))

BF16 = jnp.bfloat16
F32 = jnp.float32


def _dot(a, b):
    return jnp.dot(a, b, preferred_element_type=F32)


def _dot_t(a, b_t):
    return lax.dot_general(a, b_t, _NT, preferred_element_type=F32)


def _layer_norm(x, g, b):
    mu = jnp.mean(x, axis=-1, keepdims=True)
    xc = x - mu
    var = jnp.mean(xc * xc, axis=-1, keepdims=True)
    return xc * lax.rsqrt(var + LN_EPS) * g + b


def _rms_norm(x, g):
    return x * lax.rsqrt(jnp.mean(x * x, axis=-1, keepdims=True) + RMS_EPS) * g


def _sigmoid(x):
    return 0.5 * jnp.tanh(0.5 * x) + 0.5


def _rope_rot(x):
    width = x.shape[-1]
    half = MLA_ROPE // 2
    lane = lax.broadcasted_iota(jnp.int32, x.shape, 1)
    first_half = (lane & (MLA_ROPE - 1)) < half
    ahead = pltpu.roll(x, width - half, 1)
    behind = pltpu.roll(x, half, 1)
    return jnp.where(first_half, -ahead, behind)


def _const_spec(shape, index=None):
    index = (0,) * len(shape) if index is None else index
    return pl.BlockSpec(shape, lambda *_: index, pipeline_mode=pl.Buffered(1))


def _pack_kernel(wt_ref, o_ref):
    pieces = ((_OFF_VG, _OFF_ALR), (_OFF_GATE, D_IN), (_OFF_QG, _OFF_VG),
              (_OFF_QLAT, _OFF_GATE), (_OFF_ALR, _OFF_QLAT))
    off = 0
    for lo, hi in pieces:
        o_ref[off:off + hi - lo, :] = wt_ref[lo:hi, :].astype(BF16)
        off += hi - lo
    o_ref[off:, :] = jnp.zeros((o_ref.shape[0] - off, o_ref.shape[1]), BF16)


def _pack_call(w_t):
    cols = PACK_COLS
    return pl.pallas_call(
        _pack_kernel,
        grid=(D_MODEL // cols,),
        in_specs=[pl.BlockSpec((D_IN, cols), lambda i: (0, i))],
        out_specs=pl.BlockSpec((_PACK_A + _PACK_B, cols), lambda i: (0, i)),
        out_shape=jax.ShapeDtypeStruct((_PACK_A + _PACK_B, D_MODEL), BF16),
        compiler_params=pltpu.CompilerParams(
            dimension_semantics=("parallel",), vmem_limit_bytes=VMEM_LIMIT),
        name="pack",
    )(w_t)


def _pre_a_kernel(x_ref, lng_ref, lnb_ref, b_gate_ref, w_ref, *refs):
    n_side = (len(refs) - 5) // 2
    side_in, (res_ref, hb_ref, vg_ref, rg_ref, gate_ref), side_out = (
        refs[:n_side], refs[n_side:n_side + 5], refs[n_side + 5:])
    h = _layer_norm(x_ref[0], lng_ref[...], lnb_ref[...])
    res_ref[0] = DEEPNORM_ALPHA * h
    hb = h.astype(BF16)
    hb_ref[0] = hb
    vr = _dot_t(hb, w_ref[:2 * GLA_DV, :])
    vg_ref[0] = vr[:, :GLA_DV].astype(BF16)
    r = vr[:, GLA_DV:]
    rg_ref[0] = (r * _sigmoid(r)).astype(BF16)
    gate_ref[0] = _sigmoid(_dot_t(hb, w_ref[2 * GLA_DV:, :]) + b_gate_ref[...]).astype(BF16)
    for src, dst in zip(side_in, side_out):
        dst[...] = src[0].astype(BF16)


def _pre_a_call(x, lng, lnb, b_gate, w_packed, side_weights):
    bsz, seq, _ = x.shape
    tm = PRE_A_TM
    n_i = seq // tm
    steps = bsz * n_i
    row = lambda width: pl.BlockSpec((1, tm, width), lambda b, i: (b, i, 0))
    consts = (lng, lnb, b_gate)
    side_in_specs, side_out_specs, side_shapes = [], [], []
    for w in side_weights:
        _, rows, cols = w.shape
        slab = rows // steps
        assert slab * steps == rows and slab % (2 * SUBLANE) == 0
        side_in_specs.append(pl.BlockSpec((1, slab, cols), lambda b, i: (0, b * n_i + i, 0)))
        side_out_specs.append(pl.BlockSpec((slab, cols), lambda b, i: (b * n_i + i, 0)))
        side_shapes.append(jax.ShapeDtypeStruct((rows, cols), BF16))
    outs = pl.pallas_call(
        _pre_a_kernel,
        grid=(bsz, n_i),
        in_specs=[row(D_MODEL)] + [_const_spec(c.shape) for c in consts]
        + [_const_spec((_PACK_A, D_MODEL), (0, 0))] + side_in_specs,
        out_specs=[row(D_MODEL), row(D_MODEL), row(GLA_DV), row(GLA_DV), row(2 * D_MODEL)]
        + side_out_specs,
        out_shape=[jax.ShapeDtypeStruct((bsz, seq, D_MODEL), F32),
                   jax.ShapeDtypeStruct((bsz, seq, D_MODEL), BF16),
                   jax.ShapeDtypeStruct((bsz, seq, GLA_DV), BF16),
                   jax.ShapeDtypeStruct((bsz, seq, GLA_DV), BF16),
                   jax.ShapeDtypeStruct((bsz, seq, 2 * D_MODEL), BF16)]
        + side_shapes,
        compiler_params=pltpu.CompilerParams(
            dimension_semantics=("parallel", "parallel"), vmem_limit_bytes=VMEM_LIMIT),
        name="pre_a",
    )(x, *consts, w_packed, *side_weights)
    return outs[:5], outs[5:]


def _pre_b_kernel(hb_ref, cs_ref, w_a2_ref,
                  b_a2_ref, qn_g_ref, kvn_g_ref, wq_nope_ref, wq_rope_ref, wkv_k_ref, wkv_v_ref,
                  w_ref, qe_ref, ke_ref, kd_ref, dec_ref, q_ref, k_ref, v_ref):
    n = CUMSUM_ROWS
    row_i = lax.broadcasted_iota(jnp.int32, (n, n), 0)
    col_i = lax.broadcasted_iota(jnp.int32, (n, n), 1)
    same_chunk = (lax.shift_right_logical(row_i, _CHUNK_SHIFT)
                  == lax.shift_right_logical(col_i, _CHUNK_SHIFT))
    tri = (same_chunk & (row_i >= col_i)).astype(BF16)
    half = MLA_ROPE // 2
    scale = MLA_QK ** -0.5 * LOG2_E
    sub = hb_ref.shape[1] // PRE_B_SPLIT

    def project(r0):
        hb = hb_ref[0, r0:r0 + sub, :]
        qk = _dot_t(hb, w_ref[:2 * GLA_DK, :])
        lat = _dot_t(hb, w_ref[2 * GLA_DK:2 * GLA_DK + _LAT_W, :])
        slab = lat[:, _SLAB0:]

        qn =(_rms_norm(lat[:, :MLA_Q_RANK], qn_g_ref[...]) * scale).astype(BF16)
        ckv = _rms_norm(lat[:, MLA_Q_RANK:_SLAB0], kvn_g_ref[...]).astype(BF16)
        q_nope = _dot(qn, wq_nope_ref[...])
        q_rope = _dot(qn, wq_rope_ref[...])
        k_nope = _dot(ckv, wkv_k_ref[...])
        v = _dot(ckv, wkv_v_ref[...])
        return qk, slab, q_nope, q_rope, k_nope, v

    def finish(r0, qk, slab, q_nope, q_rope, k_nope, v):
        rows = slice(r0, r0 + sub)
        z = _dot(slab.astype(BF16), w_a2_ref[...]) + b_a2_ref[...]
        la = (jnp.minimum(z, 0.0) - jnp.log(1.0 + jnp.exp(-jnp.abs(z)))) / GLA_TAU

        la_hi = la.astype(BF16)
        la_lo = (la - la_hi.astype(F32)).astype(BF16)
        b = jnp.concatenate(
            [_dot(tri, la_hi[c0:c0 + n]) + _dot(tri, la_lo[c0:c0 + n]) for c0 in range(0, sub, n)],
            axis=0)
        chunk_ends = range(GLA_CHUNK, sub + 1, GLA_CHUNK)
        per_chunk = lambda back: jnp.concatenate(
            [jnp.broadcast_to(b[e - back:e - back + 1], (GLA_CHUNK, GLA_DK)) for e in chunk_ends],
            axis=0)
        bl = per_chunk(1)
        bm = per_chunk(GLA_CHUNK // 2 + 1)
        qs = qk[:, :GLA_DK] * GLA_HK ** -0.5
        kk = qk[:, GLA_DK:]
        qe_ref[0, rows, :] = (qs * jnp.exp(b - bm)).astype(BF16)
        ke_ref[0, rows, :] = (kk * jnp.exp(bm - b)).astype(BF16)
        kd_ref[0, rows, :] = (kk * jnp.exp(bl - b)).astype(BF16)
        to_end = lax.broadcasted_iota(jnp.int32, (SUBLANE, GLA_DK), 0) == 0
        for ci, e in enumerate(chunk_ends):
            dec_ref[0, r0 // GLA_CHUNK + ci] = jnp.exp(jnp.where(
                to_end, b[e - 1:e], b[e - GLA_CHUNK // 2 - 1:e - GLA_CHUNK // 2]))

        cs = cs_ref[0, :, rows].T
        group = lax.shift_right_logical(lax.broadcasted_iota(jnp.int32, cs.shape, 1),
                                        half.bit_length() - 1)
        r1, r2, r3 = (pltpu.roll(cs, k * half, 1) for k in (1, 2, 3))
        cos_t = jnp.where(group == 0, cs, jnp.where(group == 1, r1, jnp.where(group == 2, r2, r3)))
        sin_t = jnp.where(group == 0, r3, jnp.where(group == 1, cs, jnp.where(group == 2, r1, r2)))
        k_rope = (slab * cos_t + _rope_rot(slab) * sin_t)[:, :MLA_ROPE].astype(BF16)

        reps = q_rope.shape[-1] // LANE
        q_rope = (q_rope * jnp.concatenate([cos_t] * reps, axis=-1)
                  + _rope_rot(q_rope) * jnp.concatenate([sin_t] * reps, axis=-1))
        for hd in range(MLA_HEADS):
            q_ref[0, hd, rows, :MLA_NOPE] = q_nope[:, hd * MLA_NOPE:(hd + 1) * MLA_NOPE].astype(BF16)
            q_ref[0, hd, rows, MLA_NOPE:] = q_rope[:, hd * MLA_ROPE:(hd + 1) * MLA_ROPE].astype(BF16)
            k_ref[0, hd, rows, :MLA_NOPE] = k_nope[:, hd * MLA_NOPE:(hd + 1) * MLA_NOPE].astype(BF16)
            k_ref[0, hd, rows, MLA_NOPE:] = k_rope
            v_ref[0, hd, rows, :] = v[:, hd * MLA_V:(hd + 1) * MLA_V].astype(BF16)

    starts = list(range(0, hb_ref.shape[1], sub))
    pending = project(starts[0])
    for t, r0 in enumerate(starts):
        current = pending
        if t + 1 < len(starts):
            pending = project(starts[t + 1])
        finish(r0, *current)


def _pre_b_call(hb, cos_sin, w_a2, b_a2, qn_g, kvn_g,
                wq_nope, wq_rope, wkv_k, wkv_v, w_packed):
    bsz, seq, _ = hb.shape
    tm = PRE_B_TM
    assert (tm // PRE_B_SPLIT) % CUMSUM_ROWS == 0 and CUMSUM_ROWS % GLA_CHUNK == 0
    row = lambda width: pl.BlockSpec((1, tm, width), lambda b, i: (b, i, 0))
    head = lambda width: pl.BlockSpec((1, MLA_HEADS, tm, width), lambda b, i: (b, 0, i, 0))
    consts = (w_a2, b_a2, qn_g, kvn_g, wq_nope, wq_rope, wkv_k, wkv_v)
    out_shape = (
        jax.ShapeDtypeStruct((bsz, seq, GLA_DK), BF16),
        jax.ShapeDtypeStruct((bsz, seq, GLA_DK), BF16),
        jax.ShapeDtypeStruct((bsz, seq, GLA_DK), BF16),
        jax.ShapeDtypeStruct((bsz, seq // GLA_CHUNK, SUBLANE, GLA_DK), F32),
        jax.ShapeDtypeStruct((bsz, MLA_HEADS, seq, MLA_QK), BF16),
        jax.ShapeDtypeStruct((bsz, MLA_HEADS, seq, MLA_QK), BF16),
        jax.ShapeDtypeStruct((bsz, MLA_HEADS, seq, MLA_V), BF16),
    )
    dec_spec = pl.BlockSpec((1, tm // GLA_CHUNK, SUBLANE, GLA_DK), lambda b, i: (b, i, 0, 0))
    out_specs = (row(GLA_DK), row(GLA_DK), row(GLA_DK), dec_spec,
                 head(MLA_QK), head(MLA_QK), head(MLA_V))
    return pl.pallas_call(
        _pre_b_kernel,
        grid=(bsz, seq // tm),
        in_specs=[row(D_MODEL), pl.BlockSpec((1, LANE, tm), lambda b, i: (b, 0, i))]
        + [_const_spec(c.shape) for c in consts]
        + [_const_spec((_PACK_B, D_MODEL), (_PACK_A // _PACK_B, 0))],
        out_specs=out_specs,
        out_shape=out_shape,
        compiler_params=pltpu.CompilerParams(
            dimension_semantics=("parallel", "parallel"), vmem_limit_bytes=VMEM_LIMIT),
        name="pre_b",
    )(hb, cos_sin, *consts, w_packed)


def _gla_kernel(qe_ref, ke_ref, kd_ref, dec_ref, v_ref, r_ref, g_ref, o_ref, st_ref):
    @pl.when(pl.program_id(1) == 0)
    def _():
        st_ref[...] = jnp.zeros_like(st_ref)

    c = GLA_CHUNK
    row = lax.broadcasted_iota(jnp.int32, (c, c), 0)
    col = lax.broadcasted_iota(jnp.int32, (c, c), 1)
    causal = row >= col

    n_chunks = qe_ref.shape[1] // c
    kcols = [slice(hd * GLA_HK, (hd + 1) * GLA_HK) for hd in range(GLA_HEADS)]
    vcols = [slice(hd * GLA_HV, (hd + 1) * GLA_HV) for hd in range(GLA_HEADS)]

    def intra_scores(ci):
        rows = slice(ci * c, (ci + 1) * c)
        ps = []
        for hd in range(GLA_HEADS):
            s = lax.dot_general(qe_ref[0, rows, kcols[hd]], ke_ref[0, rows, kcols[hd]], _NT,
                                preferred_element_type=F32)
            ps.append(jnp.where(causal, s, 0.0).astype(BF16))
        return ps

    def outputs_and_state(ci, ps):
        rows = slice(ci * c, (ci + 1) * c)
        os = []
        for hd in range(GLA_HEADS):
            qe = qe_ref[0, rows, kcols[hd]]
            v = v_ref[0, rows, vcols[hd]]
            st = st_ref[hd]
            st_mid = (st * dec_ref[0, ci, 1:2, kcols[hd]]).astype(BF16)
            os.append(_dot(ps[hd], v) + lax.dot_general(qe, st_mid, _NT,
                                                        preferred_element_type=F32))
            st_ref[hd] = st * dec_ref[0, ci, 0:1, kcols[hd]] + lax.dot_general(
                v, kd_ref[0, rows, kcols[hd]], _TN, preferred_element_type=F32)
        return os

    def norm_and_store(ci, os):
        rows = slice(ci * c, (ci + 1) * c)
        for hd in range(GLA_HEADS):
            on = _rms_norm(os[hd], g_ref[:, vcols[hd]])
            o_ref[0, rows, vcols[hd]] = (on * r_ref[0, rows, vcols[hd]].astype(F32)).astype(BF16)

    ps, os = {}, {}
    for t in range(n_chunks + 2):
        if t < n_chunks:
            ps[t] = intra_scores(t)
        if 1 <= t <= n_chunks:
            os[t - 1] = outputs_and_state(t - 1, ps.pop(t - 1))
        if t >= 2:
            norm_and_store(t - 2, os.pop(t - 2))


def _gla_call(qe, ke, kd, dec, vg, rg, g):
    bsz, seq, _ = qe.shape
    tb = GLA_TB
    kspec = pl.BlockSpec((1, tb, GLA_DK), lambda b, i: (b, i, 0))
    vspec = pl.BlockSpec((1, tb, GLA_DV), lambda b, i: (b, i, 0))
    dspec = pl.BlockSpec((1, tb // GLA_CHUNK, SUBLANE, GLA_DK), lambda b, i: (b, i, 0, 0))
    return pl.pallas_call(
        _gla_kernel,
        grid=(bsz, seq // tb),
        in_specs=[kspec, kspec, kspec, dspec, vspec, vspec, _const_spec(g.shape)],
        out_specs=vspec,
        out_shape=jax.ShapeDtypeStruct((bsz, seq, GLA_DV), BF16),
        scratch_shapes=[pltpu.VMEM((GLA_HEADS, GLA_HV, GLA_HK), F32)],
        compiler_params=pltpu.CompilerParams(
            dimension_semantics=("parallel", "arbitrary"), vmem_limit_bytes=VMEM_LIMIT),
        name="gla",
    )(qe, ke, kd, dec, vg, rg, g)


def _mla_kernel(q_ref, k_ref, v_ref, o_ref, vx_ref):
    tq = MLA_TQ
    seq = v_ref.shape[2]
    ones_col = lax.broadcasted_iota(jnp.int32, (seq, MLA_V), 1) == 0
    for hd in range(MLA_HPS):
        vx_ref[hd, :, :MLA_V] = v_ref[0, hd]
        vx_ref[hd, :, MLA_V:] = jnp.where(ones_col, 1.0, 0.0).astype(BF16)

    row = lax.broadcasted_iota(jnp.int32, (tq, tq), 0)
    col = lax.broadcasted_iota(jnp.int32, (tq, tq), 1)
    causal = row >= col

    def scores(qi, hd):
        q0 = qi * tq
        q = q_ref[0, hd, q0:q0 + tq, :]
        s_diag = lax.dot_general(q, k_ref[0, hd, q0:q0 + tq, :], _NT, preferred_element_type=F32)
        s_diag = jnp.where(causal, s_diag, -jnp.inf)
        if qi == 0:
            return s_diag, None
        return s_diag, lax.dot_general(q, k_ref[0, hd, :q0, :], _NT, preferred_element_type=F32)

    def finish(qi, hd, s_diag, s_off):
        q0 = qi * tq
        m = jnp.max(s_diag, axis=-1, keepdims=True)
        if s_off is not None:
            m = jnp.maximum(m, jnp.max(s_off, axis=-1, keepdims=True))
        acc = _dot(jnp.exp2(s_diag - m).astype(BF16), vx_ref[hd, q0:q0 + tq, :])
        if s_off is not None:
            acc = acc + _dot(jnp.exp2(s_off - m).astype(BF16), vx_ref[hd, :q0, :])
        o_ref[0, q0:q0 + tq, hd * MLA_V:(hd + 1) * MLA_V] = (
            acc[:, :MLA_V] / acc[:, MLA_V:MLA_V + 1]).astype(BF16)

    items = [(qi, hd) for qi in range(seq // tq) for hd in range(MLA_HPS)]
    pending = [scores(*item) for item in items[:MLA_AHEAD]]
    for t, item in enumerate(items):
        if t + MLA_AHEAD < len(items):
            pending.append(scores(*items[t + MLA_AHEAD]))
        finish(*item, *pending.pop(0))


def _mla_call(q, k, v):
    bsz, nh, seq, _ = q.shape
    hps = MLA_HPS
    head = lambda width: pl.BlockSpec((1, hps, seq, width), lambda b, h: (b, h, 0, 0))
    return pl.pallas_call(
        _mla_kernel,
        grid=(bsz, nh // hps),
        in_specs=[head(MLA_QK), head(MLA_QK), head(MLA_V)],
        out_specs=pl.BlockSpec((1, seq, hps * MLA_V), lambda b, h: (b, 0, h)),
        out_shape=jax.ShapeDtypeStruct((bsz, seq, nh * MLA_V), BF16),
        scratch_shapes=[pltpu.VMEM((hps, seq, 2 * MLA_V), BF16)],
        compiler_params=pltpu.CompilerParams(
            dimension_semantics=("parallel", "parallel"), vmem_limit_bytes=VMEM_LIMIT),
        name="mla",
    )(q, k, v)


def _post_kernel(res_ref, a_ref, m_ref, gate_ref, wog_ref, wom_ref, wout_ref,
                 ln1g_ref, ln1b_ref, h1_ref):
    sub = res_ref.shape[0] // POST_SPLIT
    starts = list(range(0, res_ref.shape[0], sub))

    def branch_outputs(r0):
        rows = slice(r0, r0 + sub)
        return _dot(a_ref[rows, :], wog_ref[...]), _dot(m_ref[rows, :], wom_ref[...])

    pending = branch_outputs(starts[0])
    for t, r0 in enumerate(starts):
        rows = slice(r0, r0 + sub)
        y_gla, y_mla = pending
        gate = gate_ref[rows, :].astype(F32)
        merged = gate[:, :D_MODEL] * y_gla + gate[:, D_MODEL:] * y_mla
        mix = _dot(merged.astype(BF16), wout_ref[...])
        if t + 1 < len(starts):
            pending = branch_outputs(starts[t + 1])
        h1_ref[rows, :] = _layer_norm(res_ref[rows, :] + mix, ln1g_ref[...], ln1b_ref[...])


def _post_call(res, act, omla, gate, wog, wom, wout, ln1g, ln1b):
    n = res.shape[0]
    tm = POST_TM
    row = lambda width: pl.BlockSpec((tm, width), lambda i: (i, 0))
    consts = (wog, wom, wout, ln1g, ln1b)
    return pl.pallas_call(
        _post_kernel,
        grid=(n // tm,),
        in_specs=[row(D_MODEL), row(D_MODEL), row(D_MODEL), row(2 * D_MODEL)]
        + [_const_spec(c.shape) for c in consts],
        out_specs=row(D_MODEL),
        out_shape=jax.ShapeDtypeStruct((n, D_MODEL), F32),
        compiler_params=pltpu.CompilerParams(
            dimension_semantics=("parallel",), vmem_limit_bytes=VMEM_LIMIT),
        name="post",
    )(res, act, omla, gate, *consts)


def _ffn_kernel(h1_ref, w1_ref, w2_ref, g_ref, b_ref, o_ref):
    h1 = h1_ref[...]
    hb = h1.astype(BF16)
    acc = jnp.zeros(h1.shape, F32)
    for f0 in range(0, D_FF, FFN_TF):
        a = jnp.maximum(_dot(hb, w1_ref[:, f0:f0 + FFN_TF]), 0.0)
        acc = acc + _dot((a * a).astype(BF16), w2_ref[f0:f0 + FFN_TF, :])
    o_ref[...] = _layer_norm(DEEPNORM_ALPHA * h1 + acc, g_ref[...], b_ref[...])


def _ffn_call(h1, w1, w2, g, b):
    n = h1.shape[0]
    tm = FFN_TM
    row = pl.BlockSpec((tm, D_MODEL), lambda i: (i, 0))
    consts = (w1, w2, g, b)
    return pl.pallas_call(
        _ffn_kernel,
        grid=(n // tm,),
        in_specs=[row] + [_const_spec(c.shape) for c in consts],
        out_specs=row,
        out_shape=jax.ShapeDtypeStruct((n, D_MODEL), F32),
        compiler_params=pltpu.CompilerParams(
            dimension_semantics=("parallel",), vmem_limit_bytes=VMEM_LIMIT),
        name="ffn",
    )(h1, *consts)


def kernel(x, positions, ln_in_g, ln_in_b, w_in, w_gla_a2, b_gla_a2, gla_norm_g, w_o_gla,
           q_a_norm_g, w_q_b, kv_a_norm_g, w_kv_b, w_o_mla, b_gate, w_out,
           ln1_g, ln1_b, w_ff1, w_ff2, ln2_g, ln2_b):
    assert DEPTH == 1 and w_in.shape[0] == 1
    bsz, seq, _ = x.shape
    n = bsz * seq
    row2 = lambda a: a.reshape(1, -1)

    inv_freq = 1.0 / (ROPE_THETA ** (jnp.arange(0, MLA_ROPE, 2, dtype=F32) / MLA_ROPE))
    ang = positions.astype(F32)[:, None, :] * inv_freq[None, :, None]
    cos_sin = jnp.concatenate(
        [jnp.cos(ang), jnp.sin(ang), jnp.zeros((bsz, LANE - MLA_ROPE, seq), F32)], axis=1)

    w_packed = _pack_call(w_in[0].T)
    slab_pad = LANE - MLA_ROPE - GLA_GATE_RANK
    w_a2 = jnp.pad(w_gla_a2[0], ((_SLAB_ALR, slab_pad), (0, 0))).astype(BF16)

    wq = w_q_b[0].reshape(MLA_Q_RANK, MLA_HEADS, MLA_QK)
    wq_nope = wq[:, :, :MLA_NOPE].reshape(MLA_Q_RANK, -1).astype(BF16)
    wq_rope = wq[:, :, MLA_NOPE:].reshape(MLA_Q_RANK, -1).astype(BF16)
    wkv = w_kv_b[0].reshape(MLA_KV_RANK, MLA_HEADS, MLA_NOPE + MLA_V)
    wkv_k = wkv[:, :, :MLA_NOPE].reshape(MLA_KV_RANK, -1).astype(BF16)
    wkv_v = wkv[:, :, MLA_NOPE:].reshape(MLA_KV_RANK, -1).astype(BF16)

    lng, lnb = row2(ln_in_g), row2(ln_in_b)
    (res, hb, vg, rg, gate), (wog, wom, wout, w1, w2) = _pre_a_call(
        x, lng, lnb, row2(b_gate[0]), w_packed, (w_o_gla, w_o_mla, w_out, w_ff1, w_ff2))
    qe, ke, kd, dec, q, k, v = _pre_b_call(
        hb, cos_sin, w_a2, row2(b_gla_a2[0]),
        row2(q_a_norm_g[0]), row2(kv_a_norm_g[0]), wq_nope, wq_rope, wkv_k, wkv_v, w_packed)

    act = _gla_call(qe, ke, kd, dec, vg, rg, row2(gla_norm_g[0]))
    omla = _mla_call(q, k, v)

    h1 = _post_call(
        res.reshape(n, D_MODEL), act.reshape(n, GLA_DV), omla.reshape(n, MLA_HEADS * MLA_V),
        gate.reshape(n, 2 * D_MODEL), wog, wom, wout,
        row2(ln1_g[0]), row2(ln1_b[0]))

    out = _ffn_call(h1, w1, w2, row2(ln2_g[0]), row2(ln2_b[0]))
    return out.reshape(bsz, seq, D_MODEL)
```

```python
import jax
import jax.numpy as jnp
from jax import lax
from jax.experimental import pallas as pl
from jax.experimental.pallas import tpu as pltpu

D_MODEL = 1024
DEPTH = 1
LN_EPS = 1e-5
RMS_EPS = 1e-6

GLA_HEADS = 4
GLA_DK = D_MODEL // 2
GLA_DV = D_MODEL
GLA_HK = GLA_DK // GLA_HEADS
GLA_HV = GLA_DV // GLA_HEADS
GLA_GATE_RANK = 16
GLA_TAU = 16.0

MLA_HEADS = 8
MLA_Q_RANK = 384
MLA_KV_RANK = 256
MLA_NOPE = 128
MLA_ROPE = 64
MLA_V = 128
MLA_QK = MLA_NOPE + MLA_ROPE
ROPE_THETA = 10000.0

LOG2_E = 1.4426950408889634

D_FF = 4 * D_MODEL
DEEPNORM_ALPHA = (2.0 * DEPTH) ** 0.25

_OFF_QG = 0
_OFF_KG = _OFF_QG + GLA_DK
_OFF_VG = _OFF_KG + GLA_DK
_OFF_RG = _OFF_VG + GLA_DV
_OFF_ALR = _OFF_RG + GLA_DV
_OFF_QLAT = _OFF_ALR + GLA_GATE_RANK
_OFF_KVLAT = _OFF_QLAT + MLA_Q_RANK
_OFF_KROPE = _OFF_KVLAT + MLA_KV_RANK
_OFF_GATE = _OFF_KROPE + MLA_ROPE
D_IN = _OFF_GATE + 2 * D_MODEL

LANE = 128
SUBLANE = 8
VMEM_LIMIT = 60 * 1024 * 1024

_SLAB0 = MLA_Q_RANK + MLA_KV_RANK
_SLAB_ALR = MLA_ROPE
assert _SLAB0 % LANE == 0 and _SLAB_ALR + GLA_GATE_RANK <= LANE
_LAT_W = _SLAB0 + LANE
_PACK_A = 2 * GLA_DV + 2 * D_MODEL
_PACK_B = _PACK_A // 2
assert 2 * GLA_DK + _LAT_W <= _PACK_B

PACK_COLS = 256
PRE_A_TM = 1024
PRE_B_TM = 1024
CUMSUM_ROWS = 256
GLA_CHUNK = 64
_CHUNK_SHIFT = GLA_CHUNK.bit_length() - 1
assert 1 << _CHUNK_SHIFT == GLA_CHUNK
GLA_TB = 1024
MLA_TQ = 256
MLA_HPS = 4
MLA_AHEAD = 1
POST_TM = 1024
POST_SPLIT = 4
FFN_TM = 1024
FFN_TF = 1024

_NT = (((1,), (1,)), ((), ()))
_TN = (((0,), (0,)), ((), ()))

BF16 = jnp.bfloat16
F32 = jnp.float32


def _dot(a, b):
    return jnp.dot(a, b, preferred_element_type=F32)


def _dot_t(a, b_t):
    return lax.dot_general(a, b_t, _NT, preferred_element_type=F32)


def _layer_norm(x, g, b):
    mu = jnp.mean(x, axis=-1, keepdims=True)
    xc = x - mu
    var = jnp.mean(xc * xc, axis=-1, keepdims=True)
    return xc * lax.rsqrt(var + LN_EPS) * g + b


def _rms_norm(x, g):
    return x * lax.rsqrt(jnp.mean(x * x, axis=-1, keepdims=True) + RMS_EPS) * g


def _sigmoid(x):
    return 0.5 * jnp.tanh(0.5 * x) + 0.5


def _rope_rot(x):
    width = x.shape[-1]
    half = MLA_ROPE // 2
    lane = lax.broadcasted_iota(jnp.int32, x.shape, 1)
    first_half = (lane & (MLA_ROPE - 1)) < half
    ahead = pltpu.roll(x, width - half, 1)
    behind = pltpu.roll(x, half, 1)
    return jnp.where(first_half, -ahead, behind)


def _const_spec(shape, index=None):
    index = (0,) * len(shape) if index is None else index
    return pl.BlockSpec(shape, lambda *_: index, pipeline_mode=pl.Buffered(1))


def _pack_kernel(wt_ref, o_ref):
    pieces = ((_OFF_VG, _OFF_ALR), (_OFF_GATE, D_IN), (_OFF_QG, _OFF_VG),
              (_OFF_QLAT, _OFF_GATE), (_OFF_ALR, _OFF_QLAT))
    off = 0
    for lo, hi in pieces:
        o_ref[off:off + hi - lo, :] = wt_ref[lo:hi, :].astype(BF16)
        off += hi - lo
    o_ref[off:, :] = jnp.zeros((o_ref.shape[0] - off, o_ref.shape[1]), BF16)


def _pack_call(w_t):
    cols = PACK_COLS
    return pl.pallas_call(
        _pack_kernel,
        grid=(D_MODEL // cols,),
        in_specs=[pl.BlockSpec((D_IN, cols), lambda i: (0, i))],
        out_specs=pl.BlockSpec((_PACK_A + _PACK_B, cols), lambda i: (0, i)),
        out_shape=jax.ShapeDtypeStruct((_PACK_A + _PACK_B, D_MODEL), BF16),
        compiler_params=pltpu.CompilerParams(
            dimension_semantics=("parallel",), vmem_limit_bytes=VMEM_LIMIT),
        name="pack",
    )(w_t)


def _pre_a_kernel(x_ref, lng_ref, lnb_ref, b_gate_ref, w_ref, *refs):
    n_side = (len(refs) - 5) // 2
    side_in, (res_ref, hb_ref, vg_ref, rg_ref, gate_ref), side_out = (
        refs[:n_side], refs[n_side:n_side + 5], refs[n_side + 5:])
    h = _layer_norm(x_ref[0], lng_ref[...], lnb_ref[...])
    res_ref[0] = DEEPNORM_ALPHA * h
    hb = h.astype(BF16)
    hb_ref[0] = hb
    vr = _dot_t(hb, w_ref[:2 * GLA_DV, :])
    vg_ref[0] = vr[:, :GLA_DV].astype(BF16)
    u = 0.5 * vr[:, GLA_DV:]
    rg_ref[0] = (u * (jnp.tanh(u) + 1.0)).astype(BF16)
    gate_ref[0] = _sigmoid(_dot_t(hb, w_ref[2 * GLA_DV:, :]) + b_gate_ref[...]).astype(BF16)
    for src, dst in zip(side_in, side_out):
        dst[...] = src[0].astype(BF16)


def _pre_a_call(x, lng, lnb, b_gate, w_packed, side_weights):
    bsz, seq, _ = x.shape
    tm = PRE_A_TM
    n_i = seq // tm
    steps = bsz * n_i
    row = lambda width: pl.BlockSpec((1, tm, width), lambda b, i: (b, i, 0))
    consts = (lng, lnb, b_gate)
    side_in_specs, side_out_specs, side_shapes = [], [], []
    for w in side_weights:
        _, rows, cols = w.shape
        slab = rows // steps
        assert slab * steps == rows and slab % (2 * SUBLANE) == 0
        side_in_specs.append(pl.BlockSpec((1, slab, cols), lambda b, i: (0, b * n_i + i, 0)))
        side_out_specs.append(pl.BlockSpec((slab, cols), lambda b, i: (b * n_i + i, 0)))
        side_shapes.append(jax.ShapeDtypeStruct((rows, cols), BF16))
    outs = pl.pallas_call(
        _pre_a_kernel,
        grid=(bsz, n_i),
        in_specs=[row(D_MODEL)] + [_const_spec(c.shape) for c in consts]
        + [_const_spec((_PACK_A, D_MODEL), (0, 0))] + side_in_specs,
        out_specs=[row(D_MODEL), row(D_MODEL), row(GLA_DV), row(GLA_DV), row(2 * D_MODEL)]
        + side_out_specs,
        out_shape=[jax.ShapeDtypeStruct((bsz, seq, D_MODEL), F32),
                   jax.ShapeDtypeStruct((bsz, seq, D_MODEL), BF16),
                   jax.ShapeDtypeStruct((bsz, seq, GLA_DV), BF16),
                   jax.ShapeDtypeStruct((bsz, seq, GLA_DV), BF16),
                   jax.ShapeDtypeStruct((bsz, seq, 2 * D_MODEL), BF16)]
        + side_shapes,
        compiler_params=pltpu.CompilerParams(
            dimension_semantics=("parallel", "parallel"), vmem_limit_bytes=VMEM_LIMIT),
        name="pre_a",
    )(x, *consts, w_packed, *side_weights)
    return outs[:5], outs[5:]


def _pre_b_kernel(hb_ref, cs_ref, w_a2_ref,
                  b_a2_ref, qn_g_ref, kvn_g_ref, wq_nope_ref, wq_rope_ref, wkv_k_ref, wkv_v_ref,
                  w_ref, qe_ref, ke_ref, kd_ref, dec_ref, q_ref, k_ref, v_ref):
    n = CUMSUM_ROWS
    row_i = lax.broadcasted_iota(jnp.int32, (n, n), 0)
    col_i = lax.broadcasted_iota(jnp.int32, (n, n), 1)
    same_chunk = (lax.shift_right_logical(row_i, _CHUNK_SHIFT)
                  == lax.shift_right_logical(col_i, _CHUNK_SHIFT))
    tri = (same_chunk & (row_i >= col_i)).astype(BF16)
    half = MLA_ROPE // 2
    scale = MLA_QK ** -0.5 * LOG2_E
    sub = hb_ref.shape[1]

    def project(r0):
        hb = hb_ref[0, r0:r0 + sub, :]
        qk = _dot_t(hb, w_ref[:2 * GLA_DK, :])
        lat = _dot_t(hb, w_ref[2 * GLA_DK:2 * GLA_DK + _LAT_W, :])
        slab = lat[:, _SLAB0:]

        qn = (_rms_norm(lat[:, :MLA_Q_RANK], qn_g_ref[...]) * scale).astype(BF16)
        ckv = _rms_norm(lat[:, MLA_Q_RANK:_SLAB0], kvn_g_ref[...]).astype(BF16)
        q_nope = _dot(qn, wq_nope_ref[...])
        q_rope = _dot(qn, wq_rope_ref[...])
        k_nope = _dot(ckv, wkv_k_ref[...])
        v = _dot(ckv, wkv_v_ref[...])
        return qk, slab, q_nope, q_rope, k_nope, v

    def finish(r0, qk, slab, q_nope, q_rope, k_nope, v):
        rows = slice(r0, r0 + sub)
        z = _dot(slab.astype(BF16), w_a2_ref[...]) + b_a2_ref[...]
        la = (jnp.minimum(z, 0.0) * (LOG2_E / GLA_TAU)
              - jnp.log2(1.0 + jnp.exp2(jnp.abs(z) * -LOG2_E)) * (1.0 / GLA_TAU))

        la_hi = la.astype(BF16)
        la_lo = (la - la_hi.astype(F32)).astype(BF16)
        b = jnp.concatenate(
            [_dot(tri, la_hi[c0:c0 + n]) + _dot(tri, la_lo[c0:c0 + n]) for c0 in range(0, sub, n)],
            axis=0)
        chunk_ends = range(GLA_CHUNK, sub + 1, GLA_CHUNK)
        per_chunk = lambda back: jnp.concatenate(
            [jnp.broadcast_to(b[e - back:e - back + 1], (GLA_CHUNK, GLA_DK)) for e in chunk_ends],
            axis=0)
        bl = per_chunk(1)
        bm = per_chunk(GLA_CHUNK // 2 + 1)
        qs = qk[:, :GLA_DK] * GLA_HK ** -0.5
        kk = qk[:, GLA_DK:]
        qe_ref[0, rows, :] = (qs * jnp.exp2(b - bm)).astype(BF16)
        ke_ref[0, rows, :] = (kk * jnp.exp2(bm - b)).astype(BF16)
        kd_ref[0, rows, :] = (kk * jnp.exp2(bl - b)).astype(BF16)
        to_end = lax.broadcasted_iota(jnp.int32, (SUBLANE, GLA_DK), 0) == 0
        for ci, e in enumerate(chunk_ends):
            dec_ref[0, r0 // GLA_CHUNK + ci] = jnp.exp2(jnp.where(
                to_end, b[e - 1:e], b[e - GLA_CHUNK // 2 - 1:e - GLA_CHUNK // 2]))

        cs = cs_ref[0, :, rows].T
        group = lax.shift_right_logical(lax.broadcasted_iota(jnp.int32, cs.shape, 1),
                                        half.bit_length() - 1)
        r1, r2, r3 = (pltpu.roll(cs, k * half, 1) for k in (1, 2, 3))
        cos_t = jnp.where(group == 0, cs, jnp.where(group == 1, r1, jnp.where(group == 2, r2, r3)))
        sin_t = jnp.where(group == 0, r3, jnp.where(group == 1, cs, jnp.where(group == 2, r1, r2)))
        k_rope = (slab * cos_t + _rope_rot(slab) * sin_t)[:, :MLA_ROPE].astype(BF16)

        reps = q_rope.shape[-1] // LANE
        q_rope = (q_rope * jnp.concatenate([cos_t] * reps, axis=-1)
                  + _rope_rot(q_rope) * jnp.concatenate([sin_t] * reps, axis=-1))
        for hd in range(MLA_HEADS):
            q_ref[0, hd, rows, :MLA_NOPE] = q_nope[:, hd * MLA_NOPE:(hd + 1) * MLA_NOPE].astype(BF16)
            q_ref[0, hd, rows, MLA_NOPE:] = q_rope[:, hd * MLA_ROPE:(hd + 1) * MLA_ROPE].astype(BF16)
            k_ref[0, hd, rows, :MLA_NOPE] = k_nope[:, hd * MLA_NOPE:(hd + 1) * MLA_NOPE].astype(BF16)
            k_ref[0, hd, rows, MLA_NOPE:] = k_rope
            v_ref[0, hd, rows, :] = v[:, hd * MLA_V:(hd + 1) * MLA_V].astype(BF16)

    finish(0, *project(0))


def _pre_b_call(hb, cos_sin, w_a2, b_a2, qn_g, kvn_g,
                wq_nope, wq_rope, wkv_k, wkv_v, w_packed):
    bsz, seq, _ = hb.shape
    tm = PRE_B_TM
    assert tm % CUMSUM_ROWS == 0 and CUMSUM_ROWS % GLA_CHUNK == 0
    row = lambda width: pl.BlockSpec((1, tm, width), lambda b, i: (b, i, 0))
    head = lambda width: pl.BlockSpec((1, MLA_HEADS, tm, width), lambda b, i: (b, 0, i, 0))
    consts = (w_a2, b_a2, qn_g, kvn_g, wq_nope, wq_rope, wkv_k, wkv_v)
    out_shape = (
        jax.ShapeDtypeStruct((bsz, seq, GLA_DK), BF16),
        jax.ShapeDtypeStruct((bsz, seq, GLA_DK), BF16),
        jax.ShapeDtypeStruct((bsz, seq, GLA_DK), BF16),
        jax.ShapeDtypeStruct((bsz, seq // GLA_CHUNK, SUBLANE, GLA_DK), F32),
        jax.ShapeDtypeStruct((bsz, MLA_HEADS, seq, MLA_QK), BF16),
        jax.ShapeDtypeStruct((bsz, MLA_HEADS, seq, MLA_QK), BF16),
        jax.ShapeDtypeStruct((bsz, MLA_HEADS, seq, MLA_V), BF16),
    )
    dec_spec = pl.BlockSpec((1, tm // GLA_CHUNK, SUBLANE, GLA_DK), lambda b, i: (b, i, 0, 0))
    out_specs = (row(GLA_DK), row(GLA_DK), row(GLA_DK), dec_spec,
                 head(MLA_QK), head(MLA_QK), head(MLA_V))
    return pl.pallas_call(
        _pre_b_kernel,
        grid=(bsz, seq // tm),
        in_specs=[row(D_MODEL), pl.BlockSpec((1, LANE, tm), lambda b, i: (b, 0, i))]
        + [_const_spec(c.shape) for c in consts]
        + [_const_spec((_PACK_B, D_MODEL), (_PACK_A // _PACK_B, 0))],
        out_specs=out_specs,
        out_shape=out_shape,
        compiler_params=pltpu.CompilerParams(
            dimension_semantics=("parallel", "parallel"), vmem_limit_bytes=VMEM_LIMIT),
        name="pre_b",
    )(hb, cos_sin, *consts, w_packed)


def _gla_kernel(qe_ref, ke_ref, kd_ref, dec_ref, v_ref, r_ref, g_ref, o_ref, st_ref):
    @pl.when(pl.program_id(1) == 0)
    def _():
        st_ref[...] = jnp.zeros_like(st_ref)

    c = GLA_CHUNK
    row = lax.broadcasted_iota(jnp.int32, (c, c), 0)
    col = lax.broadcasted_iota(jnp.int32, (c, c), 1)
    causal = row >= col

    n_chunks = qe_ref.shape[1] // c
    kcols = [slice(hd * GLA_HK, (hd + 1) * GLA_HK) for hd in range(GLA_HEADS)]
    vcols = [slice(hd * GLA_HV, (hd + 1) * GLA_HV) for hd in range(GLA_HEADS)]

    def intra_scores(ci):
        rows = slice(ci * c, (ci + 1) * c)
        ps = []
        for hd in range(GLA_HEADS):
            s = lax.dot_general(qe_ref[0, rows, kcols[hd]], ke_ref[0, rows, kcols[hd]], _NT,
                                preferred_element_type=F32)
            ps.append(jnp.where(causal, s, 0.0).astype(BF16))
        return ps

    def outputs_and_state(ci, ps):
        rows = slice(ci * c, (ci + 1) * c)
        os = []
        for hd in range(GLA_HEADS):
            qe = qe_ref[0, rows, kcols[hd]]
            v = v_ref[0, rows, vcols[hd]]
            st = st_ref[hd]
            st_mid = (st * dec_ref[0, ci, 1:2, kcols[hd]]).astype(BF16)
            os.append(_dot(ps[hd], v) + lax.dot_general(qe, st_mid, _NT,
                                                        preferred_element_type=F32))
            st_ref[hd] = st * dec_ref[0, ci, 0:1, kcols[hd]] + lax.dot_general(
                v, kd_ref[0, rows, kcols[hd]], _TN, preferred_element_type=F32)
        return os

    def norm_and_store(ci, os):
        rows = slice(ci * c, (ci + 1) * c)
        for hd in range(GLA_HEADS):
            on = _rms_norm(os[hd], g_ref[:, vcols[hd]])
            o_ref[0, rows, vcols[hd]] = (on * r_ref[0, rows, vcols[hd]].astype(F32)).astype(BF16)

    ps, os = {}, {}
    for t in range(n_chunks + 2):
        if t < n_chunks:
            ps[t] = intra_scores(t)
        if 1 <= t <= n_chunks:
            os[t - 1] = outputs_and_state(t - 1, ps.pop(t - 1))
        if t >= 2:
            norm_and_store(t - 2, os.pop(t - 2))


def _gla_call(qe, ke, kd, dec, vg, rg, g):
    bsz, seq, _ = qe.shape
    tb = GLA_TB
    kspec = pl.BlockSpec((1, tb, GLA_DK), lambda b, i: (b, i, 0))
    vspec = pl.BlockSpec((1, tb, GLA_DV), lambda b, i: (b, i, 0))
    dspec = pl.BlockSpec((1, tb // GLA_CHUNK, SUBLANE, GLA_DK), lambda b, i: (b, i, 0, 0))
    return pl.pallas_call(
        _gla_kernel,
        grid=(bsz, seq // tb),
        in_specs=[kspec, kspec, kspec, dspec, vspec, vspec, _const_spec(g.shape)],
        out_specs=vspec,
        out_shape=jax.ShapeDtypeStruct((bsz, seq, GLA_DV), BF16),
        scratch_shapes=[pltpu.VMEM((GLA_HEADS, GLA_HV, GLA_HK), F32)],
        compiler_params=pltpu.CompilerParams(
            dimension_semantics=("parallel", "arbitrary"), vmem_limit_bytes=VMEM_LIMIT),
        name="gla",
    )(qe, ke, kd, dec, vg, rg, g)


def _mla_kernel(q_ref, k_ref, v_ref, o_ref, vx_ref):
    tq = MLA_TQ
    seq = v_ref.shape[2]
    ones_col = lax.broadcasted_iota(jnp.int32, (seq, MLA_V), 1) == 0
    for hd in range(MLA_HPS):
        vx_ref[hd, :, :MLA_V] = v_ref[0, hd]
        vx_ref[hd, :, MLA_V:] = jnp.where(ones_col, 1.0, 0.0).astype(BF16)

    row = lax.broadcasted_iota(jnp.int32, (tq, tq), 0)
    col = lax.broadcasted_iota(jnp.int32, (tq, tq), 1)
    causal = row >= col

    def scores(qi, hd):
        q0 = qi * tq
        q = q_ref[0, hd, q0:q0 + tq, :]
        s_diag = lax.dot_general(q, k_ref[0, hd, q0:q0 + tq, :], _NT, preferred_element_type=F32)
        s_diag = jnp.where(causal, s_diag, -jnp.inf)
        if qi == 0:
            return s_diag, None
        return s_diag, lax.dot_general(q, k_ref[0, hd, :q0, :], _NT, preferred_element_type=F32)

    def finish(qi, hd, s_diag, s_off):
        q0 = qi * tq
        m = jnp.max(s_diag, axis=-1, keepdims=True)
        if s_off is not None:
            m = jnp.maximum(m, jnp.max(s_off, axis=-1, keepdims=True))
        acc = _dot(jnp.exp2(s_diag - m).astype(BF16), vx_ref[hd, q0:q0 + tq, :])
        if s_off is not None:
            acc = acc + _dot(jnp.exp2(s_off - m).astype(BF16), vx_ref[hd, :q0, :])
        o_ref[0, q0:q0 + tq, hd * MLA_V:(hd + 1) * MLA_V] = (
            acc[:, :MLA_V] / acc[:, MLA_V:MLA_V + 1]).astype(BF16)

    items = [(qi, hd) for qi in range(seq // tq) for hd in range(MLA_HPS)]
    pending = [scores(*item) for item in items[:MLA_AHEAD]]
    for t, item in enumerate(items):
        if t + MLA_AHEAD < len(items):
            pending.append(scores(*items[t + MLA_AHEAD]))
        finish(*item, *pending.pop(0))


def _mla_call(q, k, v):
    bsz, nh, seq, _ = q.shape
    hps = MLA_HPS
    head = lambda width: pl.BlockSpec((1, hps, seq, width), lambda b, h: (b, h, 0, 0))
    return pl.pallas_call(
        _mla_kernel,
        grid=(bsz, nh // hps),
        in_specs=[head(MLA_QK), head(MLA_QK), head(MLA_V)],
        out_specs=pl.BlockSpec((1, seq, hps * MLA_V), lambda b, h: (b, 0, h)),
        out_shape=jax.ShapeDtypeStruct((bsz, seq, nh * MLA_V), BF16),
        scratch_shapes=[pltpu.VMEM((hps, seq, 2 * MLA_V), BF16)],
        compiler_params=pltpu.CompilerParams(
            dimension_semantics=("parallel", "parallel"), vmem_limit_bytes=VMEM_LIMIT),
        name="mla",
    )(q, k, v)


def _post_kernel(res_ref, a_ref, m_ref, gate_ref, wog_ref, wom_ref, wout_ref,
                 ln1g_ref, ln1b_ref, h1_ref):
    sub = res_ref.shape[0] // POST_SPLIT
    starts = list(range(0, res_ref.shape[0], sub))

    def branch_outputs(r0):
        rows = slice(r0, r0 + sub)
        return _dot(a_ref[rows, :], wog_ref[...]), _dot(m_ref[rows, :], wom_ref[...])

    pending = branch_outputs(starts[0])
    for t, r0 in enumerate(starts):
        rows = slice(r0, r0 + sub)
        y_gla, y_mla = pending
        gate = gate_ref[rows, :].astype(F32)
        merged = gate[:, :D_MODEL] * y_gla + gate[:, D_MODEL:] * y_mla
        mix = _dot(merged.astype(BF16), wout_ref[...])
        if t + 1 < len(starts):
            pending = branch_outputs(starts[t + 1])
        h1_ref[rows, :] = _layer_norm(res_ref[rows, :] + mix, ln1g_ref[...], ln1b_ref[...])


def _post_call(res, act, omla, gate, wog, wom, wout, ln1g, ln1b):
    n = res.shape[0]
    tm = POST_TM
    row = lambda width: pl.BlockSpec((tm, width), lambda i: (i, 0))
    consts = (wog, wom, wout, ln1g, ln1b)
    return pl.pallas_call(
        _post_kernel,
        grid=(n // tm,),
        in_specs=[row(D_MODEL), row(D_MODEL), row(D_MODEL), row(2 * D_MODEL)]
        + [_const_spec(c.shape) for c in consts],
        out_specs=row(D_MODEL),
        out_shape=jax.ShapeDtypeStruct((n, D_MODEL), F32),
        compiler_params=pltpu.CompilerParams(
            dimension_semantics=("parallel",), vmem_limit_bytes=VMEM_LIMIT),
        name="post",
    )(res, act, omla, gate, *consts)


def _ffn_kernel(h1_ref, w1_ref, w2_ref, g_ref, b_ref, o_ref):
    h1 = h1_ref[...]
    hb = h1.astype(BF16)
    acc = jnp.zeros(h1.shape, F32)
    for f0 in range(0, D_FF, FFN_TF):
        a = jnp.maximum(_dot(hb, w1_ref[:, f0:f0 + FFN_TF]), 0.0)
        acc = acc + _dot((a * a).astype(BF16), w2_ref[f0:f0 + FFN_TF, :])
    o_ref[...] = _layer_norm(DEEPNORM_ALPHA * h1 + acc, g_ref[...], b_ref[...])


def _ffn_call(h1, w1, w2, g, b):
    n = h1.shape[0]
    tm = FFN_TM
    row = pl.BlockSpec((tm, D_MODEL), lambda i: (i, 0))
    consts = (w1, w2, g, b)
    return pl.pallas_call(
        _ffn_kernel,
        grid=(n // tm,),
        in_specs=[row] + [_const_spec(c.shape) for c in consts],
        out_specs=row,
        out_shape=jax.ShapeDtypeStruct((n, D_MODEL), F32),
        compiler_params=pltpu.CompilerParams(
            dimension_semantics=("parallel",), vmem_limit_bytes=VMEM_LIMIT),
        name="ffn",
    )(h1, *consts)


def kernel(x, positions, ln_in_g, ln_in_b, w_in, w_gla_a2, b_gla_a2, gla_norm_g, w_o_gla,
           q_a_norm_g, w_q_b, kv_a_norm_g, w_kv_b, w_o_mla, b_gate, w_out,
           ln1_g, ln1_b, w_ff1, w_ff2, ln2_g, ln2_b):
    assert DEPTH == 1 and w_in.shape[0] == 1
    bsz, seq, _ = x.shape
    n = bsz * seq
    row2 = lambda a: a.reshape(1, -1)

    inv_freq = 1.0 / (ROPE_THETA ** (jnp.arange(0, MLA_ROPE, 2, dtype=F32) / MLA_ROPE))
    ang = positions.astype(F32)[:, None, :] * inv_freq[None, :, None]
    cos_sin = jnp.concatenate(
        [jnp.cos(ang), jnp.sin(ang), jnp.zeros((bsz, LANE - MLA_ROPE, seq), F32)], axis=1)

    w_packed = _pack_call(w_in[0].T)
    slab_pad = LANE - MLA_ROPE - GLA_GATE_RANK
    w_a2 = jnp.pad(w_gla_a2[0], ((_SLAB_ALR, slab_pad), (0, 0))).astype(BF16)

    wq = w_q_b[0].reshape(MLA_Q_RANK, MLA_HEADS, MLA_QK)
    wq_nope = wq[:, :, :MLA_NOPE].reshape(MLA_Q_RANK, -1).astype(BF16)
    wq_rope = wq[:, :, MLA_NOPE:].reshape(MLA_Q_RANK, -1).astype(BF16)
    wkv = w_kv_b[0].reshape(MLA_KV_RANK, MLA_HEADS, MLA_NOPE + MLA_V)
    wkv_k = wkv[:, :, :MLA_NOPE].reshape(MLA_KV_RANK, -1).astype(BF16)
    wkv_v = wkv[:, :, MLA_NOPE:].reshape(MLA_KV_RANK, -1).astype(BF16)

    lng, lnb = row2(ln_in_g), row2(ln_in_b)
    (res, hb, vg, rg, gate), (wog, wom, wout, w1, w2) = _pre_a_call(
        x, lng, lnb, row2(b_gate[0]), w_packed, (w_o_gla, w_o_mla, w_out, w_ff1, w_ff2))
    qe, ke, kd, dec, q, k, v = _pre_b_call(
        hb, cos_sin, w_a2, row2(b_gla_a2[0]),
        row2(q_a_norm_g[0]), row2(kv_a_norm_g[0]), wq_nope, wq_rope, wkv_k, wkv_v, w_packed)

    act = _gla_call(qe, ke, kd, dec, vg, rg, row2(gla_norm_g[0]))
    omla = _mla_call(q, k, v)

    h1 = _post_call(
        res.reshape(n, D_MODEL), act.reshape(n, GLA_DV), omla.reshape(n, MLA_HEADS * MLA_V),
        gate.reshape(n, 2 * D_MODEL), wog, wom, wout,
        row2(ln1_g[0]), row2(ln1_b[0]))

    out = _ffn_call(h1, w1, w2, row2(ln2_g[0]), row2(ln2_b[0]))
    return out.reshape(bsz, seq, D_MODEL)
```

```python
import jax
import jax.numpy as jnp
from jax import lax
from jax.experimental import pallas as pl
from jax.experimental.pallas import tpu as pltpu

D_MODEL = 1024
DEPTH = 1
LN_EPS = 1e-5
RMS_EPS = 1e-6

GLA_HEADS = 4
GLA_DK = D_MODEL // 2
GLA_DV = D_MODEL
GLA_HK = GLA_DK // GLA_HEADS
GLA_HV = GLA_DV // GLA_HEADS
GLA_GATE_RANK = 16
GLA_TAU = 16.0

MLA_HEADS = 8
MLA_Q_RANK = 384
MLA_KV_RANK = 256
MLA_NOPE = 128
MLA_ROPE = 64
MLA_V = 128
MLA_QK = MLA_NOPE + MLA_ROPE
ROPE_THETA = 10000.0

LOG2_E = 1.4426950408889634

D_FF = 4 * D_MODEL
DEEPNORM_ALPHA = (2.0 * DEPTH) ** 0.25

_OFF_QG = 0
_OFF_KG = _OFF_QG + GLA_DK
_OFF_VG = _OFF_KG + GLA_DK
_OFF_RG = _OFF_VG + GLA_DV
_OFF_ALR = _OFF_RG + GLA_DV
_OFF_QLAT = _OFF_ALR + GLA_GATE_RANK
_OFF_KVLAT = _OFF_QLAT + MLA_Q_RANK
_OFF_KROPE = _OFF_KVLAT + MLA_KV_RANK
_OFF_GATE = _OFF_KROPE + MLA_ROPE
D_IN = _OFF_GATE + 2 * D_MODEL

LANE = 128
SUBLANE = 8
VMEM_LIMIT = 60 * 1024 * 1024

_SLAB0 = MLA_Q_RANK + MLA_KV_RANK
_SLAB_ALR = MLA_ROPE
assert _SLAB0 % LANE == 0 and _SLAB_ALR + GLA_GATE_RANK <= LANE
_LAT_W = _SLAB0 + LANE
_PACK_A = 2 * GLA_DV + 2 * D_MODEL
_PACK_B = _PACK_A // 2
assert 2 * GLA_DK + _LAT_W <= _PACK_B

PACK_COLS = 256
PRE_A_TM = 1024
PRE_B_TM = 1024
CUMSUM_ROWS = 256
GLA_CHUNK = 64
_CHUNK_SHIFT = GLA_CHUNK.bit_length() - 1
assert 1 << _CHUNK_SHIFT == GLA_CHUNK
GLA_TB = 1024
MLA_TQ = 256
MLA_HPS = 4
MLA_AHEAD = 1
POST_TM = 1024
POST_SPLIT = 4
FFN_TM = 1024
FFN_TF = 1024

_NT = (((1,), (1,)), ((), ()))
_TN = (((0,), (0,)), ((), ()))

BF16 = jnp.bfloat16
F32 = jnp.float32


def _dot(a, b):
    return jnp.dot(a, b, preferred_element_type=F32)


def _dot_t(a, b_t):
    return lax.dot_general(a, b_t, _NT, preferred_element_type=F32)


def _layer_norm(x, g, b):
    mu = jnp.mean(x, axis=-1, keepdims=True)
    xc = x - mu
    var = jnp.mean(xc * xc, axis=-1, keepdims=True)
    return xc * lax.rsqrt(var + LN_EPS) * g + b


def _rms_norm(x, g):
    return x * lax.rsqrt(jnp.mean(x * x, axis=-1, keepdims=True) + RMS_EPS) * g


def _sigmoid(x):
    return 0.5 * jnp.tanh(0.5 * x) + 0.5


def _rope_rot(x):
    width = x.shape[-1]
    half = MLA_ROPE // 2
    lane = lax.broadcasted_iota(jnp.int32, x.shape, 1)
    first_half = (lane & (MLA_ROPE - 1)) < half
    ahead = pltpu.roll(x, width - half, 1)
    behind = pltpu.roll(x, half, 1)
    return jnp.where(first_half, -ahead, behind)


def _const_spec(shape, index=None):
    index = (0,) * len(shape) if index is None else index
    return pl.BlockSpec(shape, lambda *_: index, pipeline_mode=pl.Buffered(1))


def _pack_kernel(wt_ref, o_ref):
    pieces = ((_OFF_VG, _OFF_ALR), (_OFF_GATE, D_IN), (_OFF_QG, _OFF_VG),
              (_OFF_QLAT, _OFF_GATE), (_OFF_ALR, _OFF_QLAT))
    off = 0
    for lo, hi in pieces:
        o_ref[off:off + hi - lo, :] = wt_ref[lo:hi, :].astype(BF16)
        off += hi - lo
    o_ref[off:, :] = jnp.zeros((o_ref.shape[0] - off, o_ref.shape[1]), BF16)


def _pack_call(w_t):
    cols = PACK_COLS
    return pl.pallas_call(
        _pack_kernel,
        grid=(D_MODEL // cols,),
        in_specs=[pl.BlockSpec((D_IN, cols), lambda i: (0, i))],
        out_specs=pl.BlockSpec((_PACK_A + _PACK_B, cols), lambda i: (0, i)),
        out_shape=jax.ShapeDtypeStruct((_PACK_A + _PACK_B, D_MODEL), BF16),
        compiler_params=pltpu.CompilerParams(
            dimension_semantics=("parallel",), vmem_limit_bytes=VMEM_LIMIT),
        name="pack",
    )(w_t)


def _pre_a_kernel(x_ref, lng_ref, lnb_ref, b_gate_ref, w_ref, *refs):
    n_side = (len(refs) - 5) // 2
    side_in, (res_ref, hb_ref, vg_ref, rg_ref, gate_ref), side_out = (
        refs[:n_side], refs[n_side:n_side + 5], refs[n_side + 5:])
    h = _layer_norm(x_ref[0], lng_ref[...], lnb_ref[...])
    res_ref[0] = DEEPNORM_ALPHA * h
    hb = h.astype(BF16)
    hb_ref[0] = hb
    vr = _dot_t(hb, w_ref[:2 * GLA_DV, :])
    vg_ref[0] = vr[:, :GLA_DV].astype(BF16)
    u = 0.5 * vr[:, GLA_DV:]
    rg_ref[0] = (u * (jnp.tanh(u) + 1.0)).astype(BF16)
    gate_ref[0] = _sigmoid(_dot_t(hb, w_ref[2 * GLA_DV:, :]) + b_gate_ref[...]).astype(BF16)
    for src, dst in zip(side_in, side_out):
        dst[...] = src[0].astype(BF16)


def _pre_a_call(x, lng, lnb, b_gate, w_packed, side_weights):
    bsz, seq, _ = x.shape
    tm = PRE_A_TM
    n_i = seq // tm
    steps = bsz * n_i
    row = lambda width: pl.BlockSpec((1, tm, width), lambda b, i: (b, i, 0))
    consts = (lng, lnb, b_gate)
    side_in_specs, side_out_specs, side_shapes = [], [], []
    for w in side_weights:
        _, rows, cols = w.shape
        slab = rows // steps
        assert slab * steps == rows and slab % (2 * SUBLANE) == 0
        side_in_specs.append(pl.BlockSpec((1, slab, cols), lambda b, i: (0, b * n_i + i, 0)))
        side_out_specs.append(pl.BlockSpec((slab, cols), lambda b, i: (b * n_i + i, 0)))
        side_shapes.append(jax.ShapeDtypeStruct((rows, cols), BF16))
    outs = pl.pallas_call(
        _pre_a_kernel,
        grid=(bsz, n_i),
        in_specs=[row(D_MODEL)] + [_const_spec(c.shape) for c in consts]
        + [_const_spec((_PACK_A, D_MODEL), (0, 0))] + side_in_specs,
        out_specs=[row(D_MODEL), row(D_MODEL), row(GLA_DV), row(GLA_DV), row(2 * D_MODEL)]
        + side_out_specs,
        out_shape=[jax.ShapeDtypeStruct((bsz, seq, D_MODEL), F32),
                   jax.ShapeDtypeStruct((bsz, seq, D_MODEL), BF16),
                   jax.ShapeDtypeStruct((bsz, seq, GLA_DV), BF16),
                   jax.ShapeDtypeStruct((bsz, seq, GLA_DV), BF16),
                   jax.ShapeDtypeStruct((bsz, seq, 2 * D_MODEL), BF16)]
        + side_shapes,
        compiler_params=pltpu.CompilerParams(
            dimension_semantics=("parallel", "parallel"), vmem_limit_bytes=VMEM_LIMIT),
        name="pre_a",
    )(x, *consts, w_packed, *side_weights)
    return outs[:5], outs[5:]


def _pre_b_kernel(hb_ref, cs_ref, w_a2_ref,
                  b_a2_ref, qn_g_ref, kvn_g_ref, wq_nope_ref, wq_rope_ref, wkv_k_ref, wkv_v_ref,
                  w_ref, qe_ref, ke_ref, qx_ref, kx_ref, qs_ref, kd_ref, dec_ref,
                  q_ref, k_ref, v_ref):
    n = CUMSUM_ROWS
    row_i = lax.broadcasted_iota(jnp.int32, (n, n), 0)
    col_i = lax.broadcasted_iota(jnp.int32, (n, n), 1)
    same_chunk = (lax.shift_right_logical(row_i, _CHUNK_SHIFT)
                  == lax.shift_right_logical(col_i, _CHUNK_SHIFT))
    tri = (same_chunk & (row_i >= col_i)).astype(BF16)
    half = MLA_ROPE // 2
    scale = MLA_QK ** -0.5 * LOG2_E
    sub = hb_ref.shape[1]

    def project(r0):
        hb = hb_ref[0, r0:r0 + sub, :]
        qk = _dot_t(hb, w_ref[:2 * GLA_DK, :])
        lat = _dot_t(hb, w_ref[2 * GLA_DK:2 * GLA_DK + _LAT_W, :])
        slab = lat[:, _SLAB0:]

        qn = (_rms_norm(lat[:, :MLA_Q_RANK], qn_g_ref[...]) * scale).astype(BF16)
        ckv = _rms_norm(lat[:, MLA_Q_RANK:_SLAB0], kvn_g_ref[...]).astype(BF16)
        q_nope = _dot(qn, wq_nope_ref[...])
        q_rope = _dot(qn, wq_rope_ref[...])
        k_nope = _dot(ckv, wkv_k_ref[...])
        v = _dot(ckv, wkv_v_ref[...])
        return qk, slab, q_nope, q_rope, k_nope, v

    def finish(r0, qk, slab, q_nope, q_rope, k_nope, v):
        rows = slice(r0, r0 + sub)
        z = _dot(slab.astype(BF16), w_a2_ref[...]) + b_a2_ref[...]
        la = (jnp.minimum(z, 0.0) * (LOG2_E / GLA_TAU)
              - jnp.log2(1.0 + jnp.exp2(jnp.abs(z) * -LOG2_E)) * (1.0 / GLA_TAU))

        la_hi = la.astype(BF16)
        la_lo = (la - la_hi.astype(F32)).astype(BF16)
        b = jnp.concatenate(
            [_dot(tri, la_hi[c0:c0 + n]) + _dot(tri, la_lo[c0:c0 + n]) for c0 in range(0, sub, n)],
            axis=0)
        c = GLA_CHUNK
        chunk_ends = range(c, sub + 1, c)
        at_row = lambda r: jnp.concatenate(
            [jnp.broadcast_to(b[e - c + r:e - c + r + 1], (c, GLA_DK)) for e in chunk_ends], axis=0)
        second_half = (lax.broadcasted_iota(jnp.int32, (sub, GLA_DK), 0) & (c - 1)) >= c // 2
        b_half = jnp.where(second_half, at_row(3 * c // 4 - 1), at_row(c // 4 - 1))
        b_mid = at_row(c // 2 - 1)
        qs = qk[:, :GLA_DK] * GLA_HK ** -0.5
        kk = qk[:, GLA_DK:]
        qe_ref[0, rows, :] = (qs * jnp.exp2(b - b_half)).astype(BF16)
        ke_ref[0, rows, :] = (kk * jnp.exp2(b_half - b)).astype(BF16)
        qx_ref[0, rows, :] = jnp.where(second_half, qs * jnp.exp2(b - b_mid), 0.0).astype(BF16)
        kx_ref[0, rows, :] = jnp.where(second_half, 0.0, kk * jnp.exp2(b_mid - b)).astype(BF16)
        qs_ref[0, rows, :] = (qs * jnp.exp2(b)).astype(BF16)
        kd_ref[0, rows, :] = (kk * jnp.exp2(at_row(c - 1) - b)).astype(BF16)
        for ci, e in enumerate(chunk_ends):
            dec_ref[0, r0 // c + ci] = jnp.exp2(jnp.broadcast_to(b[e - 1:e], (SUBLANE, GLA_DK)))

        cs = cs_ref[0, :, rows].T
        group = lax.shift_right_logical(lax.broadcasted_iota(jnp.int32, cs.shape, 1),
                                        half.bit_length() - 1)
        r1, r2, r3 = (pltpu.roll(cs, k * half, 1) for k in (1, 2, 3))
        cos_t = jnp.where(group == 0, cs, jnp.where(group == 1, r1, jnp.where(group == 2, r2, r3)))
        sin_t = jnp.where(group == 0, r3, jnp.where(group == 1, cs, jnp.where(group == 2, r1, r2)))
        k_rope = (slab * cos_t + _rope_rot(slab) * sin_t)[:, :MLA_ROPE].astype(BF16)

        reps = q_rope.shape[-1] // LANE
        q_rope = (q_rope * jnp.concatenate([cos_t] * reps, axis=-1)
                  + _rope_rot(q_rope) * jnp.concatenate([sin_t] * reps, axis=-1))
        for hd in range(MLA_HEADS):
            q_ref[0, hd, rows, :MLA_NOPE] = q_nope[:, hd * MLA_NOPE:(hd + 1) * MLA_NOPE].astype(BF16)
            q_ref[0, hd, rows, MLA_NOPE:] = q_rope[:, hd * MLA_ROPE:(hd + 1) * MLA_ROPE].astype(BF16)
            k_ref[0, hd, rows, :MLA_NOPE] = k_nope[:, hd * MLA_NOPE:(hd + 1) * MLA_NOPE].astype(BF16)
            k_ref[0, hd, rows, MLA_NOPE:] = k_rope
            v_ref[0, hd, rows, :] = v[:, hd * MLA_V:(hd + 1) * MLA_V].astype(BF16)

    finish(0, *project(0))


def _pre_b_call(hb, cos_sin, w_a2, b_a2, qn_g, kvn_g,
                wq_nope, wq_rope, wkv_k, wkv_v, w_packed):
    bsz, seq, _ = hb.shape
    tm = PRE_B_TM
    assert tm % CUMSUM_ROWS == 0 and CUMSUM_ROWS % GLA_CHUNK == 0
    row = lambda width: pl.BlockSpec((1, tm, width), lambda b, i: (b, i, 0))
    head = lambda width: pl.BlockSpec((1, MLA_HEADS, tm, width), lambda b, i: (b, 0, i, 0))
    consts = (w_a2, b_a2, qn_g, kvn_g, wq_nope, wq_rope, wkv_k, wkv_v)
    out_shape = (
        jax.ShapeDtypeStruct((bsz, seq, GLA_DK), BF16),
        jax.ShapeDtypeStruct((bsz, seq, GLA_DK), BF16),
        jax.ShapeDtypeStruct((bsz, seq, GLA_DK), BF16),
        jax.ShapeDtypeStruct((bsz, seq, GLA_DK), BF16),
        jax.ShapeDtypeStruct((bsz, seq, GLA_DK), BF16),
        jax.ShapeDtypeStruct((bsz, seq, GLA_DK), BF16),
        jax.ShapeDtypeStruct((bsz, seq // GLA_CHUNK, SUBLANE, GLA_DK), F32),
        jax.ShapeDtypeStruct((bsz, MLA_HEADS, seq, MLA_QK), BF16),
        jax.ShapeDtypeStruct((bsz, MLA_HEADS, seq, MLA_QK), BF16),
        jax.ShapeDtypeStruct((bsz, MLA_HEADS, seq, MLA_V), BF16),
    )
    dec_spec = pl.BlockSpec((1, tm // GLA_CHUNK, SUBLANE, GLA_DK), lambda b, i: (b, i, 0, 0))
    out_specs = (row(GLA_DK),) * 6 + (dec_spec, head(MLA_QK), head(MLA_QK), head(MLA_V))
    return pl.pallas_call(
        _pre_b_kernel,
        grid=(bsz, seq // tm),
        in_specs=[row(D_MODEL), pl.BlockSpec((1, LANE, tm), lambda b, i: (b, 0, i))]
        + [_const_spec(c.shape) for c in consts]
        + [_const_spec((_PACK_B, D_MODEL), (_PACK_A // _PACK_B, 0))],
        out_specs=out_specs,
        out_shape=out_shape,
        compiler_params=pltpu.CompilerParams(
            dimension_semantics=("parallel", "parallel"), vmem_limit_bytes=VMEM_LIMIT),
        name="pre_b",
    )(hb, cos_sin, *consts, w_packed)


def _gla_kernel(qe_ref, ke_ref, qx_ref, kx_ref, qs_ref, kd_ref, dec_ref, v_ref, r_ref, g_ref,
                o_ref, st_ref):
    @pl.when(pl.program_id(1) == 0)
    def _():
        st_ref[...] = jnp.zeros_like(st_ref)

    c = GLA_CHUNK
    row = lax.broadcasted_iota(jnp.int32, (c, c), 0)
    col = lax.broadcasted_iota(jnp.int32, (c, c), 1)
    same_half_causal = (row >= col) & ((row >= c // 2) == (col >= c // 2))

    n_chunks = qe_ref.shape[1] // c
    kcols = [slice(hd * GLA_HK, (hd + 1) * GLA_HK) for hd in range(GLA_HEADS)]
    vcols = [slice(hd * GLA_HV, (hd + 1) * GLA_HV) for hd in range(GLA_HEADS)]

    def intra_scores(ci):
        rows = slice(ci * c, (ci + 1) * c)
        ps = []
        for hd in range(GLA_HEADS):
            s = lax.dot_general(qe_ref[0, rows, kcols[hd]], ke_ref[0, rows, kcols[hd]], _NT,
                                preferred_element_type=F32)
            sx = lax.dot_general(qx_ref[0, rows, kcols[hd]], kx_ref[0, rows, kcols[hd]], _NT,
                                 preferred_element_type=F32)
            ps.append((jnp.where(same_half_causal, s, 0.0) + sx).astype(BF16))
        return ps

    def outputs_and_state(ci, ps):
        rows = slice(ci * c, (ci + 1) * c)
        os = []
        for hd in range(GLA_HEADS):
            v = v_ref[0, rows, vcols[hd]]
            st = st_ref[hd]
            os.append(_dot(ps[hd], v) + lax.dot_general(
                qs_ref[0, rows, kcols[hd]], st.astype(BF16), _NT, preferred_element_type=F32))
            st_ref[hd] = st * dec_ref[0, ci, 0:1, kcols[hd]] + lax.dot_general(
                v, kd_ref[0, rows, kcols[hd]], _TN, preferred_element_type=F32)
        return os

    def norm_and_store(ci, os):
        rows = slice(ci * c, (ci + 1) * c)
        for hd in range(GLA_HEADS):
            on = _rms_norm(os[hd], g_ref[:, vcols[hd]])
            o_ref[0, rows, vcols[hd]] = (on * r_ref[0, rows, vcols[hd]].astype(F32)).astype(BF16)

    ps, os = {}, {}
    for t in range(n_chunks + 2):
        if t < n_chunks:
            ps[t] = intra_scores(t)
        if 1 <= t <= n_chunks:
            os[t - 1] = outputs_and_state(t - 1, ps.pop(t - 1))
        if t >= 2:
            norm_and_store(t - 2, os.pop(t - 2))


def _gla_call(qe, ke, qx, kx, qs, kd, dec, vg, rg, g):
    bsz, seq, _ = qe.shape
    tb = GLA_TB
    kspec = pl.BlockSpec((1, tb, GLA_DK), lambda b, i: (b, i, 0))
    vspec = pl.BlockSpec((1, tb, GLA_DV), lambda b, i: (b, i, 0))
    dspec = pl.BlockSpec((1, tb // GLA_CHUNK, SUBLANE, GLA_DK), lambda b, i: (b, i, 0, 0))
    return pl.pallas_call(
        _gla_kernel,
        grid=(bsz, seq // tb),
        in_specs=[kspec] * 6 + [dspec, vspec, vspec, _const_spec(g.shape)],
        out_specs=vspec,
        out_shape=jax.ShapeDtypeStruct((bsz, seq, GLA_DV), BF16),
        scratch_shapes=[pltpu.VMEM((GLA_HEADS, GLA_HV, GLA_HK), F32)],
        compiler_params=pltpu.CompilerParams(
            dimension_semantics=("parallel", "arbitrary"), vmem_limit_bytes=VMEM_LIMIT),
        name="gla",
    )(qe, ke, qx, kx, qs, kd, dec, vg, rg, g)


def _mla_kernel(q_ref, k_ref, v_ref, o_ref, vx_ref):
    tq = MLA_TQ
    seq = v_ref.shape[2]
    ones_col = lax.broadcasted_iota(jnp.int32, (seq, MLA_V), 1) == 0
    for hd in range(MLA_HPS):
        vx_ref[hd, :, :MLA_V] = v_ref[0, hd]
        vx_ref[hd, :, MLA_V:] = jnp.where(ones_col, 1.0, 0.0).astype(BF16)

    row = lax.broadcasted_iota(jnp.int32, (tq, tq), 0)
    col = lax.broadcasted_iota(jnp.int32, (tq, tq), 1)
    causal = row >= col

    def scores(qi, hd):
        q0 = qi * tq
        q = q_ref[0, hd, q0:q0 + tq, :]
        s_diag = lax.dot_general(q, k_ref[0, hd, q0:q0 + tq, :], _NT, preferred_element_type=F32)
        s_diag = jnp.where(causal, s_diag, -jnp.inf)
        if qi == 0:
            return s_diag, None
        return s_diag, lax.dot_general(q, k_ref[0, hd, :q0, :], _NT, preferred_element_type=F32)

    def finish(qi, hd, s_diag, s_off):
        q0 = qi * tq
        m = jnp.max(s_diag, axis=-1, keepdims=True)
        if s_off is not None:
            m = jnp.maximum(m, jnp.max(s_off, axis=-1, keepdims=True))
        acc = _dot(jnp.exp2(s_diag - m).astype(BF16), vx_ref[hd, q0:q0 + tq, :])
        if s_off is not None:
            acc = acc + _dot(jnp.exp2(s_off - m).astype(BF16), vx_ref[hd, :q0, :])
        o_ref[0, q0:q0 + tq, hd * MLA_V:(hd + 1) * MLA_V] = (
            acc[:, :MLA_V] / acc[:, MLA_V:MLA_V + 1]).astype(BF16)

    items = [(qi, hd) for qi in range(seq // tq) for hd in range(MLA_HPS)]
    pending = [scores(*item) for item in items[:MLA_AHEAD]]
    for t, item in enumerate(items):
        if t + MLA_AHEAD < len(items):
            pending.append(scores(*items[t + MLA_AHEAD]))
        finish(*item, *pending.pop(0))


def _mla_call(q, k, v):
    bsz, nh, seq, _ = q.shape
    hps = MLA_HPS
    head = lambda width: pl.BlockSpec((1, hps, seq, width), lambda b, h: (b, h, 0, 0))
    return pl.pallas_call(
        _mla_kernel,
        grid=(bsz, nh // hps),
        in_specs=[head(MLA_QK), head(MLA_QK), head(MLA_V)],
        out_specs=pl.BlockSpec((1, seq, hps * MLA_V), lambda b, h: (b, 0, h)),
        out_shape=jax.ShapeDtypeStruct((bsz, seq, nh * MLA_V), BF16),
        scratch_shapes=[pltpu.VMEM((hps, seq, 2 * MLA_V), BF16)],
        compiler_params=pltpu.CompilerParams(
            dimension_semantics=("parallel", "parallel"), vmem_limit_bytes=VMEM_LIMIT),
        name="mla",
    )(q, k, v)


def _post_kernel(res_ref, a_ref, m_ref, gate_ref, wog_ref, wom_ref, wout_ref,
                 ln1g_ref, ln1b_ref, h1_ref):
    sub = res_ref.shape[0] // POST_SPLIT
    starts = list(range(0, res_ref.shape[0], sub))

    def branch_outputs(r0):
        rows = slice(r0, r0 + sub)
        return _dot(a_ref[rows, :], wog_ref[...]), _dot(m_ref[rows, :], wom_ref[...])

    pending = branch_outputs(starts[0])
    for t, r0 in enumerate(starts):
        rows = slice(r0, r0 + sub)
        y_gla, y_mla = pending
        gate = gate_ref[rows, :].astype(F32)
        merged = gate[:, :D_MODEL] * y_gla + gate[:, D_MODEL:] * y_mla
        mix = _dot(merged.astype(BF16), wout_ref[...])
        if t + 1 < len(starts):
            pending = branch_outputs(starts[t + 1])
        h1_ref[rows, :] = _layer_norm(res_ref[rows, :] + mix, ln1g_ref[...], ln1b_ref[...])


def _post_call(res, act, omla, gate, wog, wom, wout, ln1g, ln1b):
    n = res.shape[0]
    tm = POST_TM
    row = lambda width: pl.BlockSpec((tm, width), lambda i: (i, 0))
    consts = (wog, wom, wout, ln1g, ln1b)
    return pl.pallas_call(
        _post_kernel,
        grid=(n // tm,),
        in_specs=[row(D_MODEL), row(D_MODEL), row(D_MODEL), row(2 * D_MODEL)]
        + [_const_spec(c.shape) for c in consts],
        out_specs=row(D_MODEL),
        out_shape=jax.ShapeDtypeStruct((n, D_MODEL), F32),
        compiler_params=pltpu.CompilerParams(
            dimension_semantics=("parallel",), vmem_limit_bytes=VMEM_LIMIT),
        name="post",
    )(res, act, omla, gate, *consts)


def _ffn_kernel(h1_ref, w1_ref, w2_ref, g_ref, b_ref, o_ref):
    h1 = h1_ref[...]
    hb = h1.astype(BF16)
    acc = jnp.zeros(h1.shape, F32)
    for f0 in range(0, D_FF, FFN_TF):
        a = jnp.maximum(_dot(hb, w1_ref[:, f0:f0 + FFN_TF]), 0.0)
        acc = acc + _dot((a * a).astype(BF16), w2_ref[f0:f0 + FFN_TF, :])
    o_ref[...] = _layer_norm(DEEPNORM_ALPHA * h1 + acc, g_ref[...], b_ref[...])


def _ffn_call(h1, w1, w2, g, b):
    n = h1.shape[0]
    tm = FFN_TM
    row = pl.BlockSpec((tm, D_MODEL), lambda i: (i, 0))
    consts = (w1, w2, g, b)
    return pl.pallas_call(
        _ffn_kernel,
        grid=(n // tm,),
        in_specs=[row] + [_const_spec(c.shape) for c in consts],
        out_specs=row,
        out_shape=jax.ShapeDtypeStruct((n, D_MODEL), F32),
        compiler_params=pltpu.CompilerParams(
            dimension_semantics=("parallel",), vmem_limit_bytes=VMEM_LIMIT),
        name="ffn",
    )(h1, *consts)


def kernel(x, positions, ln_in_g, ln_in_b, w_in, w_gla_a2, b_gla_a2, gla_norm_g, w_o_gla,
           q_a_norm_g, w_q_b, kv_a_norm_g, w_kv_b, w_o_mla, b_gate, w_out,
           ln1_g, ln1_b, w_ff1, w_ff2, ln2_g, ln2_b):
    assert DEPTH == 1 and w_in.shape[0] == 1
    bsz, seq, _ = x.shape
    n = bsz * seq
    row2 = lambda a: a.reshape(1, -1)

    inv_freq = 1.0 / (ROPE_THETA ** (jnp.arange(0, MLA_ROPE, 2, dtype=F32) / MLA_ROPE))
    ang = positions.astype(F32)[:, None, :] * inv_freq[None, :, None]
    cos_sin = jnp.concatenate(
        [jnp.cos(ang), jnp.sin(ang), jnp.zeros((bsz, LANE - MLA_ROPE, seq), F32)], axis=1)

    w_packed = _pack_call(w_in[0].T)
    slab_pad = LANE - MLA_ROPE - GLA_GATE_RANK
    w_a2 = jnp.pad(w_gla_a2[0], ((_SLAB_ALR, slab_pad), (0, 0))).astype(BF16)

    wq = w_q_b[0].reshape(MLA_Q_RANK, MLA_HEADS, MLA_QK)
    wq_nope = wq[:, :, :MLA_NOPE].reshape(MLA_Q_RANK, -1).astype(BF16)
    wq_rope = wq[:, :, MLA_NOPE:].reshape(MLA_Q_RANK, -1).astype(BF16)
    wkv = w_kv_b[0].reshape(MLA_KV_RANK, MLA_HEADS, MLA_NOPE + MLA_V)
    wkv_k = wkv[:, :, :MLA_NOPE].reshape(MLA_KV_RANK, -1).astype(BF16)
    wkv_v = wkv[:, :, MLA_NOPE:].reshape(MLA_KV_RANK, -1).astype(BF16)

    lng, lnb = row2(ln_in_g), row2(ln_in_b)
    (res, hb, vg, rg, gate), (wog, wom, wout, w1, w2) = _pre_a_call(
        x, lng, lnb, row2(b_gate[0]), w_packed, (w_o_gla, w_o_mla, w_out, w_ff1, w_ff2))
    qe, ke, qx, kx, qs, kd, dec, q, k, v = _pre_b_call(
        hb, cos_sin, w_a2, row2(b_gla_a2[0]),
        row2(q_a_norm_g[0]), row2(kv_a_norm_g[0]), wq_nope, wq_rope, wkv_k, wkv_v, w_packed)

    act = _gla_call(qe, ke, qx, kx, qs, kd, dec, vg, rg, row2(gla_norm_g[0]))
    omla = _mla_call(q, k, v)

    h1 = _post_call(
        res.reshape(n, D_MODEL), act.reshape(n, GLA_DV), omla.reshape(n, MLA_HEADS * MLA_V),
        gate.reshape(n, 2 * D_MODEL), wog, wom, wout,
        row2(ln1_g[0]), row2(ln1_b[0]))

    out = _ffn_call(h1, w1, w2, row2(ln2_g[0]), row2(ln2_b[0]))
    return out.reshape(bsz, seq, D_MODEL)
```

```python
import jax
import jax.numpy as jnp
from jax import lax
from jax.experimental import pallas as pl
from jax.experimental.pallas import tpu as pltpu

D_MODEL = 1024
DEPTH = 1
LN_EPS = 1e-5
RMS_EPS = 1e-6

GLA_HEADS = 4
GLA_DK = D_MODEL // 2
GLA_DV = D_MODEL
GLA_HK = GLA_DK // GLA_HEADS
GLA_HV = GLA_DV // GLA_HEADS
GLA_GATE_RANK = 16
GLA_TAU = 16.0

MLA_HEADS = 8
MLA_Q_RANK = 384
MLA_KV_RANK = 256
MLA_NOPE = 128
MLA_ROPE = 64
MLA_V = 128
MLA_QK = MLA_NOPE + MLA_ROPE
ROPE_THETA = 10000.0

LOG2_E = 1.4426950408889634

D_FF = 4 * D_MODEL
DEEPNORM_ALPHA = (2.0 * DEPTH) ** 0.25

_OFF_QG = 0
_OFF_KG = _OFF_QG + GLA_DK
_OFF_VG = _OFF_KG + GLA_DK
_OFF_RG = _OFF_VG + GLA_DV
_OFF_ALR = _OFF_RG + GLA_DV
_OFF_QLAT = _OFF_ALR + GLA_GATE_RANK
_OFF_KVLAT = _OFF_QLAT + MLA_Q_RANK
_OFF_KROPE = _OFF_KVLAT + MLA_KV_RANK
_OFF_GATE = _OFF_KROPE + MLA_ROPE
D_IN = _OFF_GATE + 2 * D_MODEL

LANE = 128
SUBLANE = 8
VMEM_LIMIT = 60 * 1024 * 1024

_SLAB0 = MLA_Q_RANK + MLA_KV_RANK
_SLAB_ALR = MLA_ROPE
assert _SLAB0 % LANE == 0 and _SLAB_ALR + GLA_GATE_RANK <= LANE
_LAT_W = _SLAB0 + LANE
_PACK_A = 2 * GLA_DV + 2 * D_MODEL
_PACK_B = _PACK_A // 2
assert 2 * GLA_DK + _LAT_W <= _PACK_B

PACK_COLS = 256
PRE_A_TM = 1024
PRE_B_TM = 1024
CUMSUM_ROWS = 256
GLA_CHUNK = 64
_CHUNK_SHIFT = GLA_CHUNK.bit_length() - 1
assert 1 << _CHUNK_SHIFT == GLA_CHUNK
GLA_TB = 1024
MLA_TQ = 256
MLA_HPS = 4
MLA_AHEAD = 1
POST_TM = 1024
POST_SPLIT = 4
FFN_TM = 1024
FFN_TF = 1024

_NT = (((1,), (1,)), ((), ()))
_TN = (((0,), (0,)), ((), ()))

BF16 = jnp.bfloat16
F32 = jnp.float32


def _dot(a, b):
    return jnp.dot(a, b, preferred_element_type=F32)


def _dot_t(a, b_t):
    return lax.dot_general(a, b_t, _NT, preferred_element_type=F32)


def _layer_norm(x, g, b):
    mu = jnp.mean(x, axis=-1, keepdims=True)
    xc = x - mu
    var = jnp.mean(xc * xc, axis=-1, keepdims=True)
    return xc * lax.rsqrt(var + LN_EPS) * g + b


def _rms_norm(x, g):
    return x * lax.rsqrt(jnp.mean(x * x, axis=-1, keepdims=True) + RMS_EPS) * g


def _sigmoid(x):
    return 0.5 * jnp.tanh(0.5 * x) + 0.5


def _rope_rot(x):
    width = x.shape[-1]
    half = MLA_ROPE // 2
    lane = lax.broadcasted_iota(jnp.int32, x.shape, 1)
    first_half = (lane & (MLA_ROPE - 1)) < half
    ahead = pltpu.roll(x, width - half, 1)
    behind = pltpu.roll(x, half, 1)
    return jnp.where(first_half, -ahead, behind)


def _const_spec(shape, index=None):
    index = (0,) * len(shape) if index is None else index
    return pl.BlockSpec(shape, lambda *_: index, pipeline_mode=pl.Buffered(1))


def _pack_kernel(wt_ref, o_ref):
    pieces = ((_OFF_VG, _OFF_ALR), (_OFF_GATE, D_IN), (_OFF_QG, _OFF_VG),
              (_OFF_QLAT, _OFF_GATE), (_OFF_ALR, _OFF_QLAT))
    off = 0
    for lo, hi in pieces:
        o_ref[off:off + hi - lo, :] = wt_ref[lo:hi, :].astype(BF16)
        off += hi - lo
    o_ref[off:, :] = jnp.zeros((o_ref.shape[0] - off, o_ref.shape[1]), BF16)


def _pack_call(w_t):
    cols = PACK_COLS
    return pl.pallas_call(
        _pack_kernel,
        grid=(D_MODEL // cols,),
        in_specs=[pl.BlockSpec((D_IN, cols), lambda i: (0, i))],
        out_specs=pl.BlockSpec((_PACK_A + _PACK_B, cols), lambda i: (0, i)),
        out_shape=jax.ShapeDtypeStruct((_PACK_A + _PACK_B, D_MODEL), BF16),
        compiler_params=pltpu.CompilerParams(
            dimension_semantics=("parallel",), vmem_limit_bytes=VMEM_LIMIT),
        name="pack",
    )(w_t)


def _pre_a_kernel(x_ref, lng_ref, lnb_ref, b_gate_ref, w_ref, *refs):
    n_side = (len(refs) - 5) // 2
    side_in, (res_ref, hb_ref, vg_ref, rg_ref, gate_ref), side_out = (
        refs[:n_side], refs[n_side:n_side + 5], refs[n_side + 5:])
    h = _layer_norm(x_ref[0], lng_ref[...], lnb_ref[...])
    res_ref[0] = DEEPNORM_ALPHA * h
    hb = h.astype(BF16)
    hb_ref[0] = hb
    vr = _dot_t(hb, w_ref[:2 * GLA_DV, :])
    vg_ref[0] = vr[:, :GLA_DV].astype(BF16)
    u = 0.5 * vr[:, GLA_DV:]
    rg_ref[0] = (u * (jnp.tanh(u) + 1.0)).astype(BF16)
    gate_ref[0] = _sigmoid(_dot_t(hb, w_ref[2 * GLA_DV:, :]) + b_gate_ref[...]).astype(BF16)
    for src, dst in zip(side_in, side_out):
        dst[...] = src[0].astype(BF16)


def _pre_a_call(x, lng, lnb, b_gate, w_packed, side_weights):
    bsz, seq, _ = x.shape
    tm = PRE_A_TM
    n_i = seq // tm
    steps = bsz * n_i
    row = lambda width: pl.BlockSpec((1, tm, width), lambda b, i: (b, i, 0))
    consts = (lng, lnb, b_gate)
    side_in_specs, side_out_specs, side_shapes = [], [], []
    for w in side_weights:
        _, rows, cols = w.shape
        slab = rows // steps
        assert slab * steps == rows and slab % (2 * SUBLANE) == 0
        side_in_specs.append(pl.BlockSpec((1, slab, cols), lambda b, i: (0, b * n_i + i, 0)))
        side_out_specs.append(pl.BlockSpec((slab, cols), lambda b, i: (b * n_i + i, 0)))
        side_shapes.append(jax.ShapeDtypeStruct((rows, cols), BF16))
    outs = pl.pallas_call(
        _pre_a_kernel,
        grid=(bsz, n_i),
        in_specs=[row(D_MODEL)] + [_const_spec(c.shape) for c in consts]
        + [_const_spec((_PACK_A, D_MODEL), (0, 0))] + side_in_specs,
        out_specs=[row(D_MODEL), row(D_MODEL), row(GLA_DV), row(GLA_DV), row(2 * D_MODEL)]
        + side_out_specs,
        out_shape=[jax.ShapeDtypeStruct((bsz, seq, D_MODEL), F32),
                   jax.ShapeDtypeStruct((bsz, seq, D_MODEL), BF16),
                   jax.ShapeDtypeStruct((bsz, seq, GLA_DV), BF16),
                   jax.ShapeDtypeStruct((bsz, seq, GLA_DV), BF16),
                   jax.ShapeDtypeStruct((bsz, seq, 2 * D_MODEL), BF16)]
        + side_shapes,
        compiler_params=pltpu.CompilerParams(
            dimension_semantics=("parallel", "parallel"), vmem_limit_bytes=VMEM_LIMIT),
        name="pre_a",
    )(x, *consts, w_packed, *side_weights)
    return outs[:5], outs[5:]


def _pre_b_kernel(hb_ref, cs_ref, w_a2_ref,
                  b_a2_ref, qn_g_ref, kvn_g_ref, wq_nope_ref, wq_rope_ref, wkv_k_ref, wkv_v_ref,
                  w_ref, qe_ref, ke_ref, x_ref, qs_ref, kd_ref, dec_ref, q_ref, k_ref, v_ref):
    n = CUMSUM_ROWS
    row_i = lax.broadcasted_iota(jnp.int32, (n, n), 0)
    col_i = lax.broadcasted_iota(jnp.int32, (n, n), 1)
    same_chunk = (lax.shift_right_logical(row_i, _CHUNK_SHIFT)
                  == lax.shift_right_logical(col_i, _CHUNK_SHIFT))
    tri = (same_chunk & (row_i >= col_i)).astype(BF16)
    half = MLA_ROPE // 2
    scale = MLA_QK ** -0.5 * LOG2_E
    sub = hb_ref.shape[1]

    def project(r0):
        hb = hb_ref[0, r0:r0 + sub, :]
        qk = _dot_t(hb, w_ref[:2 * GLA_DK, :])
        lat = _dot_t(hb, w_ref[2 * GLA_DK:2 * GLA_DK + _LAT_W, :])
        slab = lat[:, _SLAB0:]

        qn = (_rms_norm(lat[:, :MLA_Q_RANK], qn_g_ref[...]) * scale).astype(BF16)
        ckv = _rms_norm(lat[:, MLA_Q_RANK:_SLAB0], kvn_g_ref[...]).astype(BF16)
        q_nope = _dot(qn, wq_nope_ref[...])
        q_rope = _dot(qn, wq_rope_ref[...])
        k_nope = _dot(ckv, wkv_k_ref[...])
        v = _dot(ckv, wkv_v_ref[...])
        return qk, slab, q_nope, q_rope, k_nope, v

    def finish(r0, qk, slab, q_nope, q_rope, k_nope, v):
        rows = slice(r0, r0 + sub)
        z = _dot(slab.astype(BF16), w_a2_ref[...]) + b_a2_ref[...]
        la = (jnp.minimum(z, 0.0) * (LOG2_E / GLA_TAU)
              - jnp.log2(1.0 + jnp.exp2(jnp.abs(z) * -LOG2_E)) * (1.0 / GLA_TAU))

        la_hi = la.astype(BF16)
        la_lo = (la - la_hi.astype(F32)).astype(BF16)
        b = jnp.concatenate(
            [_dot(tri, la_hi[c0:c0 + n]) + _dot(tri, la_lo[c0:c0 + n]) for c0 in range(0, sub, n)],
            axis=0)
        c = GLA_CHUNK
        chunk_ends = range(c, sub + 1, c)
        at_row = lambda r: jnp.concatenate(
            [jnp.broadcast_to(b[e - c + r:e - c + r + 1], (c, GLA_DK)) for e in chunk_ends], axis=0)
        second_half = (lax.broadcasted_iota(jnp.int32, (sub, GLA_DK), 0) & (c - 1)) >= c // 2
        b_half = jnp.where(second_half, at_row(3 * c // 4 - 1), at_row(c // 4 - 1))
        b_mid = at_row(c // 2 - 1)
        qs = qk[:, :GLA_DK] * GLA_HK ** -0.5
        kk = qk[:, GLA_DK:]
        qe_ref[0, rows, :] = (qs * jnp.exp2(b - b_half)).astype(BF16)
        ke_ref[0, rows, :] = (kk * jnp.exp2(b_half - b)).astype(BF16)
        x_exp = jnp.where(second_half, b - b_mid, b_mid - b)
        x_ref[0, rows, :] = (jnp.where(second_half, qs, kk) * jnp.exp2(x_exp)).astype(BF16)
        qs_ref[0, rows, :] = (qs * jnp.exp2(b)).astype(BF16)
        kd_ref[0, rows, :] = (kk * jnp.exp2(at_row(c - 1) - b)).astype(BF16)
        for ci, e in enumerate(chunk_ends):
            dec_ref[0, r0 // c + ci] = jnp.exp2(jnp.broadcast_to(b[e - 1:e], (SUBLANE, GLA_DK)))

        cs = cs_ref[0, :, rows].T
        group = lax.shift_right_logical(lax.broadcasted_iota(jnp.int32, cs.shape, 1),
                                        half.bit_length() - 1)
        r1, r2, r3 = (pltpu.roll(cs, k * half, 1) for k in (1, 2, 3))
        cos_t = jnp.where(group == 0, cs, jnp.where(group == 1, r1, jnp.where(group == 2, r2, r3)))
        sin_t = jnp.where(group == 0, r3, jnp.where(group == 1, cs, jnp.where(group == 2, r1, r2)))
        k_rope = (slab * cos_t + _rope_rot(slab) * sin_t)[:, :MLA_ROPE].astype(BF16)

        reps = q_rope.shape[-1] // LANE
        q_rope = (q_rope * jnp.concatenate([cos_t] * reps, axis=-1)
                  + _rope_rot(q_rope) * jnp.concatenate([sin_t] * reps, axis=-1))
        for hd in range(MLA_HEADS):
            q_ref[0, hd, rows, :MLA_NOPE] = q_nope[:, hd * MLA_NOPE:(hd + 1) * MLA_NOPE].astype(BF16)
            q_ref[0, hd, rows, MLA_NOPE:] = q_rope[:, hd * MLA_ROPE:(hd + 1) * MLA_ROPE].astype(BF16)
            k_ref[0, hd, rows, :MLA_NOPE] = k_nope[:, hd * MLA_NOPE:(hd + 1) * MLA_NOPE].astype(BF16)
            k_ref[0, hd, rows, MLA_NOPE:] = k_rope
            v_ref[0, hd, rows, :] = v[:, hd * MLA_V:(hd + 1) * MLA_V].astype(BF16)

    finish(0, *project(0))


def _pre_b_call(hb, cos_sin, w_a2, b_a2, qn_g, kvn_g,
                wq_nope, wq_rope, wkv_k, wkv_v, w_packed):
    bsz, seq, _ = hb.shape
    tm = PRE_B_TM
    assert tm % CUMSUM_ROWS == 0 and CUMSUM_ROWS % GLA_CHUNK == 0
    row = lambda width: pl.BlockSpec((1, tm, width), lambda b, i: (b, i, 0))
    head = lambda width: pl.BlockSpec((1, MLA_HEADS, tm, width), lambda b, i: (b, 0, i, 0))
    consts = (w_a2, b_a2, qn_g, kvn_g, wq_nope, wq_rope, wkv_k, wkv_v)
    out_shape = (
        jax.ShapeDtypeStruct((bsz, seq, GLA_DK), BF16),
        jax.ShapeDtypeStruct((bsz, seq, GLA_DK), BF16),
        jax.ShapeDtypeStruct((bsz, seq, GLA_DK), BF16),
        jax.ShapeDtypeStruct((bsz, seq, GLA_DK), BF16),
        jax.ShapeDtypeStruct((bsz, seq, GLA_DK), BF16),
        jax.ShapeDtypeStruct((bsz, seq // GLA_CHUNK, SUBLANE, GLA_DK), F32),
        jax.ShapeDtypeStruct((bsz, MLA_HEADS, seq, MLA_QK), BF16),
        jax.ShapeDtypeStruct((bsz, MLA_HEADS, seq, MLA_QK), BF16),
        jax.ShapeDtypeStruct((bsz, MLA_HEADS, seq, MLA_V), BF16),
    )
    dec_spec = pl.BlockSpec((1, tm // GLA_CHUNK, SUBLANE, GLA_DK), lambda b, i: (b, i, 0, 0))
    out_specs = (row(GLA_DK),) * 5 + (dec_spec, head(MLA_QK), head(MLA_QK), head(MLA_V))
    return pl.pallas_call(
        _pre_b_kernel,
        grid=(bsz, seq // tm),
        in_specs=[row(D_MODEL), pl.BlockSpec((1, LANE, tm), lambda b, i: (b, 0, i))]
        + [_const_spec(c.shape) for c in consts]
        + [_const_spec((_PACK_B, D_MODEL), (_PACK_A // _PACK_B, 0))],
        out_specs=out_specs,
        out_shape=out_shape,
        compiler_params=pltpu.CompilerParams(
            dimension_semantics=("parallel", "parallel"), vmem_limit_bytes=VMEM_LIMIT),
        name="pre_b",
    )(hb, cos_sin, *consts, w_packed)


def _gla_kernel(qe_ref, ke_ref, x_ref, qs_ref, kd_ref, dec_ref, v_ref, r_ref, g_ref,
                o_ref, st_ref):
    @pl.when(pl.program_id(1) == 0)
    def _():
        st_ref[...] = jnp.zeros_like(st_ref)

    c = GLA_CHUNK
    row = lax.broadcasted_iota(jnp.int32, (c, c), 0)
    col = lax.broadcasted_iota(jnp.int32, (c, c), 1)
    same_half_causal = (row >= col) & ((row >= c // 2) == (col >= c // 2))
    x_is_query = lax.broadcasted_iota(jnp.int32, (c, GLA_HK), 0) >= c // 2

    n_chunks = qe_ref.shape[1] // c
    kcols = [slice(hd * GLA_HK, (hd + 1) * GLA_HK) for hd in range(GLA_HEADS)]
    vcols = [slice(hd * GLA_HV, (hd + 1) * GLA_HV) for hd in range(GLA_HEADS)]

    def intra_scores(ci):
        rows = slice(ci * c, (ci + 1) * c)
        ps = []
        for hd in range(GLA_HEADS):
            s = lax.dot_general(qe_ref[0, rows, kcols[hd]], ke_ref[0, rows, kcols[hd]], _NT,
                                preferred_element_type=F32)
            x = x_ref[0, rows, kcols[hd]]
            zero = jnp.zeros_like(x)
            sx = lax.dot_general(jnp.where(x_is_query, x, zero), jnp.where(x_is_query, zero, x),
                                 _NT, preferred_element_type=F32)
            ps.append((jnp.where(same_half_causal, s, 0.0) + sx).astype(BF16))
        return ps

    def outputs_and_state(ci, ps):
        rows = slice(ci * c, (ci + 1) * c)
        os = []
        for hd in range(GLA_HEADS):
            v = v_ref[0, rows, vcols[hd]]
            st = st_ref[hd]
            os.append(_dot(ps[hd], v) + lax.dot_general(
                qs_ref[0, rows, kcols[hd]], st.astype(BF16), _NT, preferred_element_type=F32))
            st_ref[hd] = st * dec_ref[0, ci, 0:1, kcols[hd]] + lax.dot_general(
                v, kd_ref[0, rows, kcols[hd]], _TN, preferred_element_type=F32)
        return os

    def norm_and_store(ci, os):
        rows = slice(ci * c, (ci + 1) * c)
        for hd in range(GLA_HEADS):
            on = _rms_norm(os[hd], g_ref[:, vcols[hd]])
            o_ref[0, rows, vcols[hd]] = (on * r_ref[0, rows, vcols[hd]].astype(F32)).astype(BF16)

    ps, os = {}, {}
    for t in range(n_chunks + 2):
        if t < n_chunks:
            ps[t] = intra_scores(t)
        if 1 <= t <= n_chunks:
            os[t - 1] = outputs_and_state(t - 1, ps.pop(t - 1))
        if t >= 2:
            norm_and_store(t - 2, os.pop(t - 2))


def _gla_call(qe, ke, xf, qs, kd, dec, vg, rg, g):
    bsz, seq, _ = qe.shape
    tb = GLA_TB
    kspec = pl.BlockSpec((1, tb, GLA_DK), lambda b, i: (b, i, 0))
    vspec = pl.BlockSpec((1, tb, GLA_DV), lambda b, i: (b, i, 0))
    dspec = pl.BlockSpec((1, tb // GLA_CHUNK, SUBLANE, GLA_DK), lambda b, i: (b, i, 0, 0))
    return pl.pallas_call(
        _gla_kernel,
        grid=(bsz, seq // tb),
        in_specs=[kspec] * 5 + [dspec, vspec, vspec, _const_spec(g.shape)],
        out_specs=vspec,
        out_shape=jax.ShapeDtypeStruct((bsz, seq, GLA_DV), BF16),
        scratch_shapes=[pltpu.VMEM((GLA_HEADS, GLA_HV, GLA_HK), F32)],
        compiler_params=pltpu.CompilerParams(
            dimension_semantics=("parallel", "arbitrary"), vmem_limit_bytes=VMEM_LIMIT),
        name="gla",
    )(qe, ke, xf, qs, kd, dec, vg, rg, g)


def _mla_kernel(q_ref, k_ref, v_ref, o_ref, vx_ref):
    tq = MLA_TQ
    seq = v_ref.shape[2]
    ones_col = lax.broadcasted_iota(jnp.int32, (seq, MLA_V), 1) == 0
    for hd in range(MLA_HPS):
        vx_ref[hd, :, :MLA_V] = v_ref[0, hd]
        vx_ref[hd, :, MLA_V:] = jnp.where(ones_col, 1.0, 0.0).astype(BF16)

    row = lax.broadcasted_iota(jnp.int32, (tq, tq), 0)
    col = lax.broadcasted_iota(jnp.int32, (tq, tq), 1)
    causal = row >= col

    def scores(qi, hd):
        q0 = qi * tq
        q = q_ref[0, hd, q0:q0 + tq, :]
        s_diag = lax.dot_general(q, k_ref[0, hd, q0:q0 + tq, :], _NT, preferred_element_type=F32)
        s_diag = jnp.where(causal, s_diag, -jnp.inf)
        if qi == 0:
            return s_diag, None
        return s_diag, lax.dot_general(q, k_ref[0, hd, :q0, :], _NT, preferred_element_type=F32)

    def finish(qi, hd, s_diag, s_off):
        q0 = qi * tq
        m = jnp.max(s_diag, axis=-1, keepdims=True)
        if s_off is not None:
            m = jnp.maximum(m, jnp.max(s_off, axis=-1, keepdims=True))
        acc = _dot(jnp.exp2(s_diag - m).astype(BF16), vx_ref[hd, q0:q0 + tq, :])
        if s_off is not None:
            acc = acc + _dot(jnp.exp2(s_off - m).astype(BF16), vx_ref[hd, :q0, :])
        o_ref[0, q0:q0 + tq, hd * MLA_V:(hd + 1) * MLA_V] = (
            acc[:, :MLA_V] / acc[:, MLA_V:MLA_V + 1]).astype(BF16)

    items = [(qi, hd) for qi in range(seq // tq) for hd in range(MLA_HPS)]
    pending = [scores(*item) for item in items[:MLA_AHEAD]]
    for t, item in enumerate(items):
        if t + MLA_AHEAD < len(items):
            pending.append(scores(*items[t + MLA_AHEAD]))
        finish(*item, *pending.pop(0))


def _mla_call(q, k, v):
    bsz, nh, seq, _ = q.shape
    hps = MLA_HPS
    head = lambda width: pl.BlockSpec((1, hps, seq, width), lambda b, h: (b, h, 0, 0))
    return pl.pallas_call(
        _mla_kernel,
        grid=(bsz, nh // hps),
        in_specs=[head(MLA_QK), head(MLA_QK), head(MLA_V)],
        out_specs=pl.BlockSpec((1, seq, hps * MLA_V), lambda b, h: (b, 0, h)),
        out_shape=jax.ShapeDtypeStruct((bsz, seq, nh * MLA_V), BF16),
        scratch_shapes=[pltpu.VMEM((hps, seq, 2 * MLA_V), BF16)],
        compiler_params=pltpu.CompilerParams(
            dimension_semantics=("parallel", "parallel"), vmem_limit_bytes=VMEM_LIMIT),
        name="mla",
    )(q, k, v)


def _post_kernel(res_ref, a_ref, m_ref, gate_ref, wog_ref, wom_ref, wout_ref,
                 ln1g_ref, ln1b_ref, h1_ref):
    sub = res_ref.shape[0] // POST_SPLIT
    starts = list(range(0, res_ref.shape[0], sub))

    def branch_outputs(r0):
        rows = slice(r0, r0 + sub)
        return _dot(a_ref[rows, :], wog_ref[...]), _dot(m_ref[rows, :], wom_ref[...])

    pending = branch_outputs(starts[0])
    for t, r0 in enumerate(starts):
        rows = slice(r0, r0 + sub)
        y_gla, y_mla = pending
        gate = gate_ref[rows, :].astype(F32)
        merged = gate[:, :D_MODEL] * y_gla + gate[:, D_MODEL:] * y_mla
        mix = _dot(merged.astype(BF16), wout_ref[...])
        if t + 1 < len(starts):
            pending = branch_outputs(starts[t + 1])
        h1_ref[rows, :] = _layer_norm(res_ref[rows, :] + mix, ln1g_ref[...], ln1b_ref[...])


def _post_call(res, act, omla, gate, wog, wom, wout, ln1g, ln1b):
    n = res.shape[0]
    tm = POST_TM
    row = lambda width: pl.BlockSpec((tm, width), lambda i: (i, 0))
    consts = (wog, wom, wout, ln1g, ln1b)
    return pl.pallas_call(
        _post_kernel,
        grid=(n // tm,),
        in_specs=[row(D_MODEL), row(D_MODEL), row(D_MODEL), row(2 * D_MODEL)]
        + [_const_spec(c.shape) for c in consts],
        out_specs=row(D_MODEL),
        out_shape=jax.ShapeDtypeStruct((n, D_MODEL), F32),
        compiler_params=pltpu.CompilerParams(
            dimension_semantics=("parallel",), vmem_limit_bytes=VMEM_LIMIT),
        name="post",
    )(res, act, omla, gate, *consts)


def _ffn_kernel(h1_ref, w1_ref, w2_ref, g_ref, b_ref, o_ref):
    h1 = h1_ref[...]
    hb = h1.astype(BF16)
    acc = jnp.zeros(h1.shape, F32)
    for f0 in range(0, D_FF, FFN_TF):
        a = jnp.maximum(_dot(hb, w1_ref[:, f0:f0 + FFN_TF]), 0.0)
        acc = acc + _dot((a * a).astype(BF16), w2_ref[f0:f0 + FFN_TF, :])
    o_ref[...] = _layer_norm(DEEPNORM_ALPHA * h1 + acc, g_ref[...], b_ref[...])


def _ffn_call(h1, w1, w2, g, b):
    n = h1.shape[0]
    tm = FFN_TM
    row = pl.BlockSpec((tm, D_MODEL), lambda i: (i, 0))
    consts = (w1, w2, g, b)
    return pl.pallas_call(
        _ffn_kernel,
        grid=(n // tm,),
        in_specs=[row] + [_const_spec(c.shape) for c in consts],
        out_specs=row,
        out_shape=jax.ShapeDtypeStruct((n, D_MODEL), F32),
        compiler_params=pltpu.CompilerParams(
            dimension_semantics=("parallel",), vmem_limit_bytes=VMEM_LIMIT),
        name="ffn",
    )(h1, *consts)


def kernel(x, positions, ln_in_g, ln_in_b, w_in, w_gla_a2, b_gla_a2, gla_norm_g, w_o_gla,
           q_a_norm_g, w_q_b, kv_a_norm_g, w_kv_b, w_o_mla, b_gate, w_out,
           ln1_g, ln1_b, w_ff1, w_ff2, ln2_g, ln2_b):
    assert DEPTH == 1 and w_in.shape[0] == 1
    bsz, seq, _ = x.shape
    n = bsz * seq
    row2 = lambda a: a.reshape(1, -1)

    inv_freq = 1.0 / (ROPE_THETA ** (jnp.arange(0, MLA_ROPE, 2, dtype=F32) / MLA_ROPE))
    ang = positions.astype(F32)[:, None, :] * inv_freq[None, :, None]
    cos_sin = jnp.concatenate(
        [jnp.cos(ang), jnp.sin(ang), jnp.zeros((bsz, LANE - MLA_ROPE, seq), F32)], axis=1)

    w_packed = _pack_call(w_in[0].T)
    slab_pad = LANE - MLA_ROPE - GLA_GATE_RANK
    w_a2 = jnp.pad(w_gla_a2[0], ((_SLAB_ALR, slab_pad), (0, 0))).astype(BF16)

    wq = w_q_b[0].reshape(MLA_Q_RANK, MLA_HEADS, MLA_QK)
    wq_nope = wq[:, :, :MLA_NOPE].reshape(MLA_Q_RANK, -1).astype(BF16)
    wq_rope = wq[:, :, MLA_NOPE:].reshape(MLA_Q_RANK, -1).astype(BF16)
    wkv = w_kv_b[0].reshape(MLA_KV_RANK, MLA_HEADS, MLA_NOPE + MLA_V)
    wkv_k = wkv[:, :, :MLA_NOPE].reshape(MLA_KV_RANK, -1).astype(BF16)
    wkv_v = wkv[:, :, MLA_NOPE:].reshape(MLA_KV_RANK, -1).astype(BF16)

    lng, lnb = row2(ln_in_g), row2(ln_in_b)
    (res, hb, vg, rg, gate), (wog, wom, wout, w1, w2) = _pre_a_call(
        x, lng, lnb, row2(b_gate[0]), w_packed, (w_o_gla, w_o_mla, w_out, w_ff1, w_ff2))
    qe, ke, xf, qs, kd, dec, q, k, v = _pre_b_call(
        hb, cos_sin, w_a2, row2(b_gla_a2[0]),
        row2(q_a_norm_g[0]), row2(kv_a_norm_g[0]), wq_nope, wq_rope, wkv_k, wkv_v, w_packed)

    act = _gla_call(qe, ke, xf, qs, kd, dec, vg, rg, row2(gla_norm_g[0]))
    omla = _mla_call(q, k, v)

    h1 = _post_call(
        res.reshape(n, D_MODEL), act.reshape(n, GLA_DV), omla.reshape(n, MLA_HEADS * MLA_V),
        gate.reshape(n, 2 * D_MODEL), wog, wom, wout,
        row2(ln1_g[0]), row2(ln1_b[0]))

    out = _ffn_call(h1, w1, w2, row2(ln2_g[0]), row2(ln2_b[0]))
    return out.reshape(bsz, seq, D_MODEL)
```

```python
import jax
import jax.numpy as jnp
from jax import lax
from jax.experimental import pallas as pl
from jax.experimental.pallas import tpu as pltpu

D_MODEL = 1024
DEPTH = 1
LN_EPS = 1e-5
RMS_EPS = 1e-6

GLA_HEADS = 4
GLA_DK = D_MODEL // 2
GLA_DV = D_MODEL
GLA_HK = GLA_DK // GLA_HEADS
GLA_HV = GLA_DV // GLA_HEADS
GLA_GATE_RANK = 16
GLA_TAU = 16.0

MLA_HEADS = 8
MLA_Q_RANK = 384
MLA_KV_RANK = 256
MLA_NOPE = 128
MLA_ROPE = 64
MLA_V = 128
MLA_QK = MLA_NOPE + MLA_ROPE
ROPE_THETA = 10000.0

LOG2_E = 1.4426950408889634

D_FF = 4 * D_MODEL
DEEPNORM_ALPHA = (2.0 * DEPTH) ** 0.25

_OFF_QG = 0
_OFF_KG = _OFF_QG + GLA_DK
_OFF_VG = _OFF_KG + GLA_DK
_OFF_RG = _OFF_VG + GLA_DV
_OFF_ALR = _OFF_RG + GLA_DV
_OFF_QLAT = _OFF_ALR + GLA_GATE_RANK
_OFF_KVLAT = _OFF_QLAT + MLA_Q_RANK
_OFF_KROPE = _OFF_KVLAT + MLA_KV_RANK
_OFF_GATE = _OFF_KROPE + MLA_ROPE
D_IN = _OFF_GATE + 2 * D_MODEL

LANE = 128
SUBLANE = 8
VMEM_LIMIT = 60 * 1024 * 1024

_SLAB0 = MLA_Q_RANK + MLA_KV_RANK
_SLAB_ALR = MLA_ROPE
assert _SLAB0 % LANE == 0 and _SLAB_ALR + GLA_GATE_RANK <= LANE
_LAT_W = _SLAB0 + LANE
_PACK_A = 2 * GLA_DV + 2 * D_MODEL
_PACK_B = _PACK_A // 2
assert 2 * GLA_DK + _LAT_W <= _PACK_B

PACK_COLS = 256
PRE_A_TM = 1024
PRE_B_TM = 1024
CUMSUM_ROWS = 256
GLA_CHUNK = 64
_CHUNK_SHIFT = GLA_CHUNK.bit_length() - 1
assert 1 << _CHUNK_SHIFT == GLA_CHUNK
GLA_TB = 1024
MLA_TQ = 256
MLA_HPS = 4
MLA_AHEAD = 1
POST_TM = 1024
POST_SPLIT = 4
FFN_TM = 1024
FFN_TF = 1024

_NT = (((1,), (1,)), ((), ()))
_TN = (((0,), (0,)), ((), ()))

BF16 = jnp.bfloat16
F32 = jnp.float32


def _dot(a, b):
    return jnp.dot(a, b, preferred_element_type=F32)


def _dot_t(a, b_t):
    return lax.dot_general(a, b_t, _NT, preferred_element_type=F32)


def _layer_norm(x, g, b):
    mu = jnp.mean(x, axis=-1, keepdims=True)
    xc = x - mu
    var = jnp.mean(xc * xc, axis=-1, keepdims=True)
    return xc * lax.rsqrt(var + LN_EPS) * g + b


def _rms_norm(x, g):
    return x * lax.rsqrt(jnp.mean(x * x, axis=-1, keepdims=True) + RMS_EPS) * g


def _sigmoid(x):
    return 0.5 * jnp.tanh(0.5 * x) + 0.5


def _rope_rot(x):
    width = x.shape[-1]
    half = MLA_ROPE // 2
    lane = lax.broadcasted_iota(jnp.int32, x.shape, 1)
    first_half = (lane & (MLA_ROPE - 1)) < half
    ahead = pltpu.roll(x, width - half, 1)
    behind = pltpu.roll(x, half, 1)
    return jnp.where(first_half, -ahead, behind)


def _const_spec(shape, index=None):
    index = (0,) * len(shape) if index is None else index
    return pl.BlockSpec(shape, lambda *_: index, pipeline_mode=pl.Buffered(1))


def _pack_kernel(wt_ref, wq_ref, wkv_ref, o_ref, wq_nope_ref, wq_rope_ref, wkv_k_ref, wkv_v_ref):
    pieces = ((_OFF_VG, _OFF_ALR), (_OFF_GATE, D_IN), (_OFF_QG, _OFF_VG),
              (_OFF_QLAT, _OFF_GATE), (_OFF_ALR, _OFF_QLAT))
    off = 0
    for lo, hi in pieces:
        o_ref[off:off + hi - lo, :] = wt_ref[lo:hi, :].astype(BF16)
        off += hi - lo
    o_ref[off:, :] = jnp.zeros((o_ref.shape[0] - off, o_ref.shape[1]), BF16)

    @pl.when(pl.program_id(0) == 0)
    def _():
        kv = MLA_NOPE + MLA_V
        for hd in range(MLA_HEADS):
            wq_nope_ref[:, hd * MLA_NOPE:(hd + 1) * MLA_NOPE] = (
                wq_ref[0, :, hd * MLA_QK:hd * MLA_QK + MLA_NOPE].astype(BF16))
            wq_rope_ref[:, hd * MLA_ROPE:(hd + 1) * MLA_ROPE] = (
                wq_ref[0, :, hd * MLA_QK + MLA_NOPE:(hd + 1) * MLA_QK].astype(BF16))
            wkv_k_ref[:, hd * MLA_NOPE:(hd + 1) * MLA_NOPE] = (
                wkv_ref[0, :, hd * kv:hd * kv + MLA_NOPE].astype(BF16))
            wkv_v_ref[:, hd * MLA_V:(hd + 1) * MLA_V] = (
                wkv_ref[0, :, hd * kv + MLA_NOPE:(hd + 1) * kv].astype(BF16))


def _pack_call(w_t, w_q_b, w_kv_b):
    cols = PACK_COLS
    whole = lambda shape: pl.BlockSpec(shape, lambda i: (0,) * len(shape))
    return pl.pallas_call(
        _pack_kernel,
        grid=(D_MODEL // cols,),
        in_specs=[pl.BlockSpec((D_IN, cols), lambda i: (0, i)),
                  _const_spec(w_q_b.shape), _const_spec(w_kv_b.shape)],
        out_specs=(pl.BlockSpec((_PACK_A + _PACK_B, cols), lambda i: (0, i)),
                   whole((MLA_Q_RANK, MLA_HEADS * MLA_NOPE)), whole((MLA_Q_RANK, MLA_HEADS * MLA_ROPE)),
                   whole((MLA_KV_RANK, MLA_HEADS * MLA_NOPE)), whole((MLA_KV_RANK, MLA_HEADS * MLA_V))),
        out_shape=(jax.ShapeDtypeStruct((_PACK_A + _PACK_B, D_MODEL), BF16),
                   jax.ShapeDtypeStruct((MLA_Q_RANK, MLA_HEADS * MLA_NOPE), BF16),
                   jax.ShapeDtypeStruct((MLA_Q_RANK, MLA_HEADS * MLA_ROPE), BF16),
                   jax.ShapeDtypeStruct((MLA_KV_RANK, MLA_HEADS * MLA_NOPE), BF16),
                   jax.ShapeDtypeStruct((MLA_KV_RANK, MLA_HEADS * MLA_V), BF16)),
        compiler_params=pltpu.CompilerParams(
            dimension_semantics=("arbitrary",), vmem_limit_bytes=VMEM_LIMIT),
        name="pack",
    )(w_t, w_q_b, w_kv_b)


def _pre_a_kernel(x_ref, lng_ref, lnb_ref, b_gate_ref, w_ref, *refs):
    n_side = (len(refs) - 5) // 2
    side_in, (res_ref, hb_ref, vg_ref, rg_ref, gate_ref), side_out = (
        refs[:n_side], refs[n_side:n_side + 5], refs[n_side + 5:])
    h = _layer_norm(x_ref[0], lng_ref[...], lnb_ref[...])
    res_ref[0] = DEEPNORM_ALPHA * h
    hb = h.astype(BF16)
    hb_ref[0] = hb
    vr = _dot_t(hb, w_ref[:2 * GLA_DV, :])
    vg_ref[0] = vr[:, :GLA_DV].astype(BF16)
    u = 0.5 * vr[:, GLA_DV:]
    rg_ref[0] = (u * (jnp.tanh(u) + 1.0)).astype(BF16)
    gate_ref[0] = _sigmoid(_dot_t(hb, w_ref[2 * GLA_DV:, :]) + b_gate_ref[...]).astype(BF16)
    for src, dst in zip(side_in, side_out):
        dst[...] = src[0].astype(BF16)


def _pre_a_call(x, lng, lnb, b_gate, w_packed, side_weights):
    bsz, seq, _ = x.shape
    tm = PRE_A_TM
    n_i = seq // tm
    steps = bsz * n_i
    row = lambda width: pl.BlockSpec((1, tm, width), lambda b, i: (b, i, 0))
    consts = (lng, lnb, b_gate)
    side_in_specs, side_out_specs, side_shapes = [], [], []
    for w in side_weights:
        _, rows, cols = w.shape
        slab = rows // steps
        assert slab * steps == rows and slab % (2 * SUBLANE) == 0
        side_in_specs.append(pl.BlockSpec((1, slab, cols), lambda b, i: (0, b * n_i + i, 0)))
        side_out_specs.append(pl.BlockSpec((slab, cols), lambda b, i: (b * n_i + i, 0)))
        side_shapes.append(jax.ShapeDtypeStruct((rows, cols), BF16))
    outs = pl.pallas_call(
        _pre_a_kernel,
        grid=(bsz, n_i),
        in_specs=[row(D_MODEL)] + [_const_spec(c.shape) for c in consts]
        + [_const_spec((_PACK_A, D_MODEL), (0, 0))] + side_in_specs,
        out_specs=[row(D_MODEL), row(D_MODEL), row(GLA_DV), row(GLA_DV), row(2 * D_MODEL)]
        + side_out_specs,
        out_shape=[jax.ShapeDtypeStruct((bsz, seq, D_MODEL), F32),
                   jax.ShapeDtypeStruct((bsz, seq, D_MODEL), BF16),
                   jax.ShapeDtypeStruct((bsz, seq, GLA_DV), BF16),
                   jax.ShapeDtypeStruct((bsz, seq, GLA_DV), BF16),
                   jax.ShapeDtypeStruct((bsz, seq, 2 * D_MODEL), BF16)]
        + side_shapes,
        compiler_params=pltpu.CompilerParams(
            dimension_semantics=("parallel", "parallel"), vmem_limit_bytes=VMEM_LIMIT),
        name="pre_a",
    )(x, *consts, w_packed, *side_weights)
    return outs[:5], outs[5:]


def _pre_b_kernel(hb_ref, cs_ref, w_a2_ref,
                  b_a2_ref, qn_g_ref, kvn_g_ref, wq_nope_ref, wq_rope_ref, wkv_k_ref, wkv_v_ref,
                  w_ref, qe_ref, ke_ref, x_ref, qs_ref, kd_ref, dec_ref, q_ref, k_ref, v_ref):
    n = CUMSUM_ROWS
    row_i = lax.broadcasted_iota(jnp.int32, (n, n), 0)
    col_i = lax.broadcasted_iota(jnp.int32, (n, n), 1)
    same_chunk = (lax.shift_right_logical(row_i, _CHUNK_SHIFT)
                  == lax.shift_right_logical(col_i, _CHUNK_SHIFT))
    tri = (same_chunk & (row_i >= col_i)).astype(BF16)
    half = MLA_ROPE // 2
    scale = MLA_QK ** -0.5 * LOG2_E
    sub = hb_ref.shape[1]

    def project(r0):
        hb = hb_ref[0, r0:r0 + sub, :]
        qk = _dot_t(hb, w_ref[:2 * GLA_DK, :])
        lat = _dot_t(hb, w_ref[2 * GLA_DK:2 * GLA_DK + _LAT_W, :])
        slab = lat[:, _SLAB0:]

        qn = (_rms_norm(lat[:, :MLA_Q_RANK], qn_g_ref[...]) * scale).astype(BF16)
        ckv = _rms_norm(lat[:, MLA_Q_RANK:_SLAB0], kvn_g_ref[...]).astype(BF16)
        q_nope = _dot(qn, wq_nope_ref[...])
        q_rope = _dot(qn, wq_rope_ref[...])
        k_nope = _dot(ckv, wkv_k_ref[...])
        v = _dot(ckv, wkv_v_ref[...])
        return qk, slab, q_nope, q_rope, k_nope, v

    def finish(r0, qk, slab, q_nope, q_rope, k_nope, v):
        rows = slice(r0, r0 + sub)
        z = _dot(slab.astype(BF16), w_a2_ref[...]) + b_a2_ref[...]
        la = (jnp.minimum(z, 0.0) * (LOG2_E / GLA_TAU)
              - jnp.log2(1.0 + jnp.exp2(jnp.abs(z) * -LOG2_E)) * (1.0 / GLA_TAU))

        la_hi = la.astype(BF16)
        la_lo = (la - la_hi.astype(F32)).astype(BF16)
        b = jnp.concatenate(
            [_dot(tri, la_hi[c0:c0 + n]) + _dot(tri, la_lo[c0:c0 + n]) for c0 in range(0, sub, n)],
            axis=0)
        c = GLA_CHUNK
        chunk_ends = range(c, sub + 1, c)
        at_row = lambda r: jnp.concatenate(
            [jnp.broadcast_to(b[e - c + r:e - c + r + 1], (c, GLA_DK)) for e in chunk_ends], axis=0)
        second_half = (lax.broadcasted_iota(jnp.int32, (sub, GLA_DK), 0) & (c - 1)) >= c // 2
        b_half = jnp.where(second_half, at_row(3 * c // 4 - 1), at_row(c // 4 - 1))
        b_mid = at_row(c // 2 - 1)
        qs = qk[:, :GLA_DK] * GLA_HK ** -0.5
        kk = qk[:, GLA_DK:]
        qe_ref[0, rows, :] = (qs * jnp.exp2(b - b_half)).astype(BF16)
        ke_ref[0, rows, :] = (kk * jnp.exp2(b_half - b)).astype(BF16)
        x_exp = jnp.where(second_half, b - b_mid, b_mid - b)
        x_ref[0, rows, :] = (jnp.where(second_half, qs, kk) * jnp.exp2(x_exp)).astype(BF16)
        qs_ref[0, rows, :] = (qs * jnp.exp2(b)).astype(BF16)
        kd_ref[0, rows, :] = (kk * jnp.exp2(at_row(c - 1) - b)).astype(BF16)
        for ci, e in enumerate(chunk_ends):
            dec_ref[0, r0 // c + ci] = jnp.exp2(jnp.broadcast_to(b[e - 1:e], (SUBLANE, GLA_DK)))

        cs = cs_ref[0, :, rows].T
        group = lax.shift_right_logical(lax.broadcasted_iota(jnp.int32, cs.shape, 1),
                                        half.bit_length() - 1)
        r1, r2, r3 = (pltpu.roll(cs, k * half, 1) for k in (1, 2, 3))
        cos_t = jnp.where(group == 0, cs, jnp.where(group == 1, r1, jnp.where(group == 2, r2, r3)))
        sin_t = jnp.where(group == 0, r3, jnp.where(group == 1, cs, jnp.where(group == 2, r1, r2)))
        k_rope = (slab * cos_t + _rope_rot(slab) * sin_t)[:, :MLA_ROPE].astype(BF16)

        reps = q_rope.shape[-1] // LANE
        q_rope = (q_rope * jnp.concatenate([cos_t] * reps, axis=-1)
                  + _rope_rot(q_rope) * jnp.concatenate([sin_t] * reps, axis=-1))
        for hd in range(MLA_HEADS):
            q_ref[0, hd, rows, :MLA_NOPE] = q_nope[:, hd * MLA_NOPE:(hd + 1) * MLA_NOPE].astype(BF16)
            q_ref[0, hd, rows, MLA_NOPE:] = q_rope[:, hd * MLA_ROPE:(hd + 1) * MLA_ROPE].astype(BF16)
            k_ref[0, hd, rows, :MLA_NOPE] = k_nope[:, hd * MLA_NOPE:(hd + 1) * MLA_NOPE].astype(BF16)
            k_ref[0, hd, rows, MLA_NOPE:] = k_rope
            v_ref[0, hd, rows, :] = v[:, hd * MLA_V:(hd + 1) * MLA_V].astype(BF16)

    finish(0, *project(0))


def _pre_b_call(hb, cos_sin, w_a2, b_a2, qn_g, kvn_g,
                wq_nope, wq_rope, wkv_k, wkv_v, w_packed):
    bsz, seq, _ = hb.shape
    tm = PRE_B_TM
    assert tm % CUMSUM_ROWS == 0 and CUMSUM_ROWS % GLA_CHUNK == 0
    row = lambda width: pl.BlockSpec((1, tm, width), lambda b, i: (b, i, 0))
    head = lambda width: pl.BlockSpec((1, MLA_HEADS, tm, width), lambda b, i: (b, 0, i, 0))
    consts = (w_a2, b_a2, qn_g, kvn_g, wq_nope, wq_rope, wkv_k, wkv_v)
    out_shape = (
        jax.ShapeDtypeStruct((bsz, seq, GLA_DK), BF16),
        jax.ShapeDtypeStruct((bsz, seq, GLA_DK), BF16),
        jax.ShapeDtypeStruct((bsz, seq, GLA_DK), BF16),
        jax.ShapeDtypeStruct((bsz, seq, GLA_DK), BF16),
        jax.ShapeDtypeStruct((bsz, seq, GLA_DK), BF16),
        jax.ShapeDtypeStruct((bsz, seq // GLA_CHUNK, SUBLANE, GLA_DK), F32),
        jax.ShapeDtypeStruct((bsz, MLA_HEADS, seq, MLA_QK), BF16),
        jax.ShapeDtypeStruct((bsz, MLA_HEADS, seq, MLA_QK), BF16),
        jax.ShapeDtypeStruct((bsz, MLA_HEADS, seq, MLA_V), BF16),
    )
    dec_spec = pl.BlockSpec((1, tm // GLA_CHUNK, SUBLANE, GLA_DK), lambda b, i: (b, i, 0, 0))
    out_specs = (row(GLA_DK),) * 5 + (dec_spec, head(MLA_QK), head(MLA_QK), head(MLA_V))
    return pl.pallas_call(
        _pre_b_kernel,
        grid=(bsz, seq // tm),
        in_specs=[row(D_MODEL), pl.BlockSpec((1, LANE, tm), lambda b, i: (b, 0, i))]
        + [_const_spec(c.shape) for c in consts]
        + [_const_spec((_PACK_B, D_MODEL), (_PACK_A // _PACK_B, 0))],
        out_specs=out_specs,
        out_shape=out_shape,
        compiler_params=pltpu.CompilerParams(
            dimension_semantics=("parallel", "parallel"), vmem_limit_bytes=VMEM_LIMIT),
        name="pre_b",
    )(hb, cos_sin, *consts, w_packed)


def _gla_kernel(qe_ref, ke_ref, x_ref, qs_ref, kd_ref, dec_ref, v_ref, r_ref, g_ref,
                o_ref, st_ref):
    @pl.when(pl.program_id(1) == 0)
    def _():
        st_ref[...] = jnp.zeros_like(st_ref)

    c = GLA_CHUNK
    row = lax.broadcasted_iota(jnp.int32, (c, c), 0)
    col = lax.broadcasted_iota(jnp.int32, (c, c), 1)
    same_half_causal = (row >= col) & ((row >= c // 2) == (col >= c // 2))
    x_is_query = lax.broadcasted_iota(jnp.int32, (c, GLA_HK), 0) >= c // 2

    n_chunks = qe_ref.shape[1] // c
    kcols = [slice(hd * GLA_HK, (hd + 1) * GLA_HK) for hd in range(GLA_HEADS)]
    vcols = [slice(hd * GLA_HV, (hd + 1) * GLA_HV) for hd in range(GLA_HEADS)]

    def intra_scores(ci):
        rows = slice(ci * c, (ci + 1) * c)
        ps = []
        for hd in range(GLA_HEADS):
            s = lax.dot_general(qe_ref[0, rows, kcols[hd]], ke_ref[0, rows, kcols[hd]], _NT,
                                preferred_element_type=F32)
            x = x_ref[0, rows, kcols[hd]]
            zero = jnp.zeros_like(x)
            sx = lax.dot_general(jnp.where(x_is_query, x, zero), jnp.where(x_is_query, zero, x),
                                 _NT, preferred_element_type=F32)
            ps.append((jnp.where(same_half_causal, s, 0.0) + sx).astype(BF16))
        return ps

    def outputs_and_state(ci, ps):
        rows = slice(ci * c, (ci + 1) * c)
        os = []
        for hd in range(GLA_HEADS):
            v = v_ref[0, rows, vcols[hd]]
            st = st_ref[hd]
            os.append(_dot(ps[hd], v) + lax.dot_general(
                qs_ref[0, rows, kcols[hd]], st.astype(BF16), _NT, preferred_element_type=F32))
            st_ref[hd] = st * dec_ref[0, ci, 0:1, kcols[hd]] + lax.dot_general(
                v, kd_ref[0, rows, kcols[hd]], _TN, preferred_element_type=F32)
        return os

    def norm_and_store(ci, os):
        rows = slice(ci * c, (ci + 1) * c)
        for hd in range(GLA_HEADS):
            on = _rms_norm(os[hd], g_ref[:, vcols[hd]])
            o_ref[0, rows, vcols[hd]] = (on * r_ref[0, rows, vcols[hd]].astype(F32)).astype(BF16)

    ps, os = {}, {}
    for t in range(n_chunks + 2):
        if t < n_chunks:
            ps[t] = intra_scores(t)
        if 1 <= t <= n_chunks:
            os[t - 1] = outputs_and_state(t - 1, ps.pop(t - 1))
        if t >= 2:
            norm_and_store(t - 2, os.pop(t - 2))


def _gla_call(qe, ke, xf, qs, kd, dec, vg, rg, g):
    bsz, seq, _ = qe.shape
    tb = GLA_TB
    kspec = pl.BlockSpec((1, tb, GLA_DK), lambda b, i: (b, i, 0))
    vspec = pl.BlockSpec((1, tb, GLA_DV), lambda b, i: (b, i, 0))
    dspec = pl.BlockSpec((1, tb // GLA_CHUNK, SUBLANE, GLA_DK), lambda b, i: (b, i, 0, 0))
    return pl.pallas_call(
        _gla_kernel,
        grid=(bsz, seq // tb),
        in_specs=[kspec] * 5 + [dspec, vspec, vspec, _const_spec(g.shape)],
        out_specs=vspec,
        out_shape=jax.ShapeDtypeStruct((bsz, seq, GLA_DV), BF16),
        scratch_shapes=[pltpu.VMEM((GLA_HEADS, GLA_HV, GLA_HK), F32)],
        compiler_params=pltpu.CompilerParams(
            dimension_semantics=("parallel", "arbitrary"), vmem_limit_bytes=VMEM_LIMIT),
        name="gla",
    )(qe, ke, xf, qs, kd, dec, vg, rg, g)


def _mla_kernel(q_ref, k_ref, v_ref, o_ref, vx_ref):
    tq = MLA_TQ
    seq = v_ref.shape[2]
    ones_col = lax.broadcasted_iota(jnp.int32, (seq, MLA_V), 1) == 0
    for hd in range(MLA_HPS):
        vx_ref[hd, :, :MLA_V] = v_ref[0, hd]
        vx_ref[hd, :, MLA_V:] = jnp.where(ones_col, 1.0, 0.0).astype(BF16)

    row = lax.broadcasted_iota(jnp.int32, (tq, tq), 0)
    col = lax.broadcasted_iota(jnp.int32, (tq, tq), 1)
    causal = row >= col

    def scores(qi, hd):
        q0 = qi * tq
        q = q_ref[0, hd, q0:q0 + tq, :]
        s_diag = lax.dot_general(q, k_ref[0, hd, q0:q0 + tq, :], _NT, preferred_element_type=F32)
        s_diag = jnp.where(causal, s_diag, -jnp.inf)
        if qi == 0:
            return s_diag, None
        return s_diag, lax.dot_general(q, k_ref[0, hd, :q0, :], _NT, preferred_element_type=F32)

    def finish(qi, hd, s_diag, s_off):
        q0 = qi * tq
        m = jnp.max(s_diag, axis=-1, keepdims=True)
        if s_off is not None:
            m = jnp.maximum(m, jnp.max(s_off, axis=-1, keepdims=True))
        acc = _dot(jnp.exp2(s_diag - m).astype(BF16), vx_ref[hd, q0:q0 + tq, :])
        if s_off is not None:
            acc = acc + _dot(jnp.exp2(s_off - m).astype(BF16), vx_ref[hd, :q0, :])
        o_ref[0, q0:q0 + tq, hd * MLA_V:(hd + 1) * MLA_V] = (
            acc[:, :MLA_V] / acc[:, MLA_V:MLA_V + 1]).astype(BF16)

    items = [(qi, hd) for qi in range(seq // tq) for hd in range(MLA_HPS)]
    pending = [scores(*item) for item in items[:MLA_AHEAD]]
    for t, item in enumerate(items):
        if t + MLA_AHEAD < len(items):
            pending.append(scores(*items[t + MLA_AHEAD]))
        finish(*item, *pending.pop(0))


def _mla_call(q, k, v):
    bsz, nh, seq, _ = q.shape
    hps = MLA_HPS
    head = lambda width: pl.BlockSpec((1, hps, seq, width), lambda b, h: (b, h, 0, 0))
    return pl.pallas_call(
        _mla_kernel,
        grid=(bsz, nh // hps),
        in_specs=[head(MLA_QK), head(MLA_QK), head(MLA_V)],
        out_specs=pl.BlockSpec((1, seq, hps * MLA_V), lambda b, h: (b, 0, h)),
        out_shape=jax.ShapeDtypeStruct((bsz, seq, nh * MLA_V), BF16),
        scratch_shapes=[pltpu.VMEM((hps, seq, 2 * MLA_V), BF16)],
        compiler_params=pltpu.CompilerParams(
            dimension_semantics=("parallel", "parallel"), vmem_limit_bytes=VMEM_LIMIT),
        name="mla",
    )(q, k, v)


def _post_kernel(res_ref, a_ref, m_ref, gate_ref, wog_ref, wom_ref, wout_ref,
                 ln1g_ref, ln1b_ref, h1_ref):
    sub = res_ref.shape[0] // POST_SPLIT
    starts = list(range(0, res_ref.shape[0], sub))

    def branch_outputs(r0):
        rows = slice(r0, r0 + sub)
        return _dot(a_ref[rows, :], wog_ref[...]), _dot(m_ref[rows, :], wom_ref[...])

    pending = branch_outputs(starts[0])
    for t, r0 in enumerate(starts):
        rows = slice(r0, r0 + sub)
        y_gla, y_mla = pending
        gate = gate_ref[rows, :].astype(F32)
        merged = gate[:, :D_MODEL] * y_gla + gate[:, D_MODEL:] * y_mla
        mix = _dot(merged.astype(BF16), wout_ref[...])
        if t + 1 < len(starts):
            pending = branch_outputs(starts[t + 1])
        h1_ref[rows, :] = _layer_norm(res_ref[rows, :] + mix, ln1g_ref[...], ln1b_ref[...])


def _post_call(res, act, omla, gate, wog, wom, wout, ln1g, ln1b):
    n = res.shape[0]
    tm = POST_TM
    row = lambda width: pl.BlockSpec((tm, width), lambda i: (i, 0))
    consts = (wog, wom, wout, ln1g, ln1b)
    return pl.pallas_call(
        _post_kernel,
        grid=(n // tm,),
        in_specs=[row(D_MODEL), row(D_MODEL), row(D_MODEL), row(2 * D_MODEL)]
        + [_const_spec(c.shape) for c in consts],
        out_specs=row(D_MODEL),
        out_shape=jax.ShapeDtypeStruct((n, D_MODEL), F32),
        compiler_params=pltpu.CompilerParams(
            dimension_semantics=("parallel",), vmem_limit_bytes=VMEM_LIMIT),
        name="post",
    )(res, act, omla, gate, *consts)


def _ffn_kernel(h1_ref, w1_ref, w2_ref, g_ref, b_ref, o_ref):
    h1 = h1_ref[...]
    hb = h1.astype(BF16)
    acc = jnp.zeros(h1.shape, F32)
    for f0 in range(0, D_FF, FFN_TF):
        a = jnp.maximum(_dot(hb, w1_ref[:, f0:f0 + FFN_TF]), 0.0)
        acc = acc + _dot((a * a).astype(BF16), w2_ref[f0:f0 + FFN_TF, :])
    o_ref[...] = _layer_norm(DEEPNORM_ALPHA * h1 + acc, g_ref[...], b_ref[...])


def _ffn_call(h1, w1, w2, g, b):
    n = h1.shape[0]
    tm = FFN_TM
    row = pl.BlockSpec((tm, D_MODEL), lambda i: (i, 0))
    consts = (w1, w2, g, b)
    return pl.pallas_call(
        _ffn_kernel,
        grid=(n // tm,),
        in_specs=[row] + [_const_spec(c.shape) for c in consts],
        out_specs=row,
        out_shape=jax.ShapeDtypeStruct((n, D_MODEL), F32),
        compiler_params=pltpu.CompilerParams(
            dimension_semantics=("parallel",), vmem_limit_bytes=VMEM_LIMIT),
        name="ffn",
    )(h1, *consts)


def kernel(x, positions, ln_in_g, ln_in_b, w_in, w_gla_a2, b_gla_a2, gla_norm_g, w_o_gla,
           q_a_norm_g, w_q_b, kv_a_norm_g, w_kv_b, w_o_mla, b_gate, w_out,
           ln1_g, ln1_b, w_ff1, w_ff2, ln2_g, ln2_b):
    assert DEPTH == 1 and w_in.shape[0] == 1
    bsz, seq, _ = x.shape
    n = bsz * seq
    row2 = lambda a: a.reshape(1, -1)

    inv_freq = 1.0 / (ROPE_THETA ** (jnp.arange(0, MLA_ROPE, 2, dtype=F32) / MLA_ROPE))
    ang = positions.astype(F32)[:, None, :] * inv_freq[None, :, None]
    cos_sin = jnp.concatenate(
        [jnp.cos(ang), jnp.sin(ang), jnp.zeros((bsz, LANE - MLA_ROPE, seq), F32)], axis=1)

    w_packed, wq_nope, wq_rope, wkv_k, wkv_v = _pack_call(w_in[0].T, w_q_b, w_kv_b)
    slab_pad = LANE - MLA_ROPE - GLA_GATE_RANK
    w_a2 = jnp.pad(w_gla_a2[0], ((_SLAB_ALR, slab_pad), (0, 0))).astype(BF16)

    lng, lnb = row2(ln_in_g), row2(ln_in_b)
    (res, hb, vg, rg, gate), (wog, wom, wout, w1, w2) = _pre_a_call(
        x, lng, lnb, row2(b_gate[0]), w_packed, (w_o_gla, w_o_mla, w_out, w_ff1, w_ff2))
    qe, ke, xf, qs, kd, dec, q, k, v = _pre_b_call(
        hb, cos_sin, w_a2, row2(b_gla_a2[0]),
        row2(q_a_norm_g[0]), row2(kv_a_norm_g[0]), wq_nope, wq_rope, wkv_k, wkv_v, w_packed)

    act = _gla_call(qe, ke, xf, qs, kd, dec, vg, rg, row2(gla_norm_g[0]))
    omla = _mla_call(q, k, v)

    h1 = _post_call(
        res.reshape(n, D_MODEL), act.reshape(n, GLA_DV), omla.reshape(n, MLA_HEADS * MLA_V),
        gate.reshape(n, 2 * D_MODEL), wog, wom, wout,
        row2(ln1_g[0]), row2(ln1_b[0]))

    out = _ffn_call(h1, w1, w2, row2(ln2_g[0]), row2(ln2_b[0]))
    return out.reshape(bsz, seq, D_MODEL)
```

```python
import jax
import jax.numpy as jnp
from jax import lax
from jax.experimental import pallas as pl
from jax.experimental.pallas import tpu as pltpu

D_MODEL = 1024
DEPTH = 1
LN_EPS = 1e-5
RMS_EPS = 1e-6

GLA_HEADS = 4
GLA_DK = D_MODEL // 2
GLA_DV = D_MODEL
GLA_HK = GLA_DK // GLA_HEADS
GLA_HV = GLA_DV // GLA_HEADS
GLA_GATE_RANK = 16
GLA_TAU = 16.0

MLA_HEADS = 8
MLA_Q_RANK = 384
MLA_KV_RANK = 256
MLA_NOPE = 128
MLA_ROPE = 64
MLA_V = 128
MLA_QK = MLA_NOPE + MLA_ROPE
ROPE_THETA = 10000.0

LOG2_E = 1.4426950408889634

D_FF = 4 * D_MODEL
DEEPNORM_ALPHA = (2.0 * DEPTH) ** 0.25

_OFF_QG = 0
_OFF_KG = _OFF_QG + GLA_DK
_OFF_VG = _OFF_KG + GLA_DK
_OFF_RG = _OFF_VG + GLA_DV
_OFF_ALR = _OFF_RG + GLA_DV
_OFF_QLAT = _OFF_ALR + GLA_GATE_RANK
_OFF_KVLAT = _OFF_QLAT + MLA_Q_RANK
_OFF_KROPE = _OFF_KVLAT + MLA_KV_RANK
_OFF_GATE = _OFF_KROPE + MLA_ROPE
D_IN = _OFF_GATE + 2 * D_MODEL

LANE = 128
SUBLANE = 8
VMEM_LIMIT = 60 * 1024 * 1024

_SLAB0 = MLA_Q_RANK + MLA_KV_RANK
_SLAB_ALR = MLA_ROPE
assert _SLAB0 % LANE == 0 and _SLAB_ALR + GLA_GATE_RANK <= LANE
_LAT_W = _SLAB0 + LANE
_PACK_A = 2 * GLA_DV + 2 * D_MODEL
_PACK_B = _PACK_A // 2
assert 2 * GLA_DK + _LAT_W <= _PACK_B

PACK_COLS = 256
PRE_A_TM = 1024
PRE_B_TM = 1024
CUMSUM_ROWS = 256
GLA_CHUNK = 64
_CHUNK_SHIFT = GLA_CHUNK.bit_length() - 1
assert 1 << _CHUNK_SHIFT == GLA_CHUNK
GLA_TB = 1024
MLA_TQ = 256
MLA_HPS = 4
MLA_AHEAD = 1
POST_TM = 1024
POST_SPLIT = 4
FFN_TM = 1024
FFN_TF = 1024

_NT = (((1,), (1,)), ((), ()))
_TN = (((0,), (0,)), ((), ()))

BF16 = jnp.bfloat16
F32 = jnp.float32


def _dot(a, b):
    return jnp.dot(a, b, preferred_element_type=F32)


def _dot_t(a, b_t):
    return lax.dot_general(a, b_t, _NT, preferred_element_type=F32)


def _layer_norm(x, g, b):
    mu = jnp.mean(x, axis=-1, keepdims=True)
    xc = x - mu
    var = jnp.mean(xc * xc, axis=-1, keepdims=True)
    return xc * lax.rsqrt(var + LN_EPS) * g + b


def _rms_norm(x, g):
    return x * lax.rsqrt(jnp.mean(x * x, axis=-1, keepdims=True) + RMS_EPS) * g


def _sigmoid(x):
    return 0.5 * jnp.tanh(0.5 * x) + 0.5


def _rope_rot(x):
    width = x.shape[-1]
    half = MLA_ROPE // 2
    lane = lax.broadcasted_iota(jnp.int32, x.shape, 1)
    first_half = (lane & (MLA_ROPE - 1)) < half
    ahead = pltpu.roll(x, width - half, 1)
    behind = pltpu.roll(x, half, 1)
    return jnp.where(first_half, -ahead, behind)


def _const_spec(shape, index=None):
    index = (0,) * len(shape) if index is None else index
    return pl.BlockSpec(shape, lambda *_: index, pipeline_mode=pl.Buffered(1))


def _pack_kernel(wt_ref, wq_ref, wkv_ref, o_ref, wq_nope_ref, wq_rope_ref, wkv_k_ref, wkv_v_ref):
    pieces = ((_OFF_VG, _OFF_ALR), (_OFF_GATE, D_IN), (_OFF_QG, _OFF_VG),
              (_OFF_QLAT, _OFF_GATE), (_OFF_ALR, _OFF_QLAT))
    off = 0
    for lo, hi in pieces:
        o_ref[off:off + hi - lo, :] = wt_ref[lo:hi, :].astype(BF16)
        off += hi - lo
    o_ref[off:, :] = jnp.zeros((o_ref.shape[0] - off, o_ref.shape[1]), BF16)

    @pl.when(pl.program_id(0) == 0)
    def _():
        kv = MLA_NOPE + MLA_V
        for hd in range(MLA_HEADS):
            wq_nope_ref[:, hd * MLA_NOPE:(hd + 1) * MLA_NOPE] = (
                wq_ref[0, :, hd * MLA_QK:hd * MLA_QK + MLA_NOPE].astype(BF16))
            wq_rope_ref[:, hd * MLA_ROPE:(hd + 1) * MLA_ROPE] = (
                wq_ref[0, :, hd * MLA_QK + MLA_NOPE:(hd + 1) * MLA_QK].astype(BF16))
            wkv_k_ref[:, hd * MLA_NOPE:(hd + 1) * MLA_NOPE] = (
                wkv_ref[0, :, hd * kv:hd * kv + MLA_NOPE].astype(BF16))
            wkv_v_ref[:, hd * MLA_V:(hd + 1) * MLA_V] = (
                wkv_ref[0, :, hd * kv + MLA_NOPE:(hd + 1) * kv].astype(BF16))


def _pack_call(w_t, w_q_b, w_kv_b):
    cols = PACK_COLS
    whole = lambda shape: pl.BlockSpec(shape, lambda i: (0,) * len(shape))
    return pl.pallas_call(
        _pack_kernel,
        grid=(D_MODEL // cols,),
        in_specs=[pl.BlockSpec((D_IN, cols), lambda i: (0, i)),
                  _const_spec(w_q_b.shape), _const_spec(w_kv_b.shape)],
        out_specs=(pl.BlockSpec((_PACK_A + _PACK_B, cols), lambda i: (0, i)),
                   whole((MLA_Q_RANK, MLA_HEADS * MLA_NOPE)), whole((MLA_Q_RANK, MLA_HEADS * MLA_ROPE)),
                   whole((MLA_KV_RANK, MLA_HEADS * MLA_NOPE)), whole((MLA_KV_RANK, MLA_HEADS * MLA_V))),
        out_shape=(jax.ShapeDtypeStruct((_PACK_A + _PACK_B, D_MODEL), BF16),
                   jax.ShapeDtypeStruct((MLA_Q_RANK, MLA_HEADS * MLA_NOPE), BF16),
                   jax.ShapeDtypeStruct((MLA_Q_RANK, MLA_HEADS * MLA_ROPE), BF16),
                   jax.ShapeDtypeStruct((MLA_KV_RANK, MLA_HEADS * MLA_NOPE), BF16),
                   jax.ShapeDtypeStruct((MLA_KV_RANK, MLA_HEADS * MLA_V), BF16)),
        compiler_params=pltpu.CompilerParams(
            dimension_semantics=("arbitrary",), vmem_limit_bytes=VMEM_LIMIT),
        name="pack",
    )(w_t, w_q_b, w_kv_b)


def _pre_a_kernel(x_ref, lng_ref, lnb_ref, b_gate_ref, w_ref, *refs):
    n_side = (len(refs) - 5) // 2
    side_in, (res_ref, hb_ref, vg_ref, rg_ref, gate_ref), side_out = (
        refs[:n_side], refs[n_side:n_side + 5], refs[n_side + 5:])
    h = _layer_norm(x_ref[0], lng_ref[...], lnb_ref[...])
    res_ref[0] = DEEPNORM_ALPHA * h
    hb = h.astype(BF16)
    hb_ref[0] = hb
    vr = _dot_t(hb, w_ref[:2 * GLA_DV, :])
    vg_ref[0] = vr[:, :GLA_DV].astype(BF16)
    u = 0.5 * vr[:, GLA_DV:]
    rg_ref[0] = (u * (jnp.tanh(u) + 1.0)).astype(BF16)
    gate_ref[0] = _sigmoid(_dot_t(hb, w_ref[2 * GLA_DV:, :]) + b_gate_ref[...]).astype(BF16)
    for src, dst in zip(side_in, side_out):
        dst[...] = src[0].astype(BF16)


def _pre_a_call(x, lng, lnb, b_gate, w_packed, side_weights):
    bsz, seq, _ = x.shape
    tm = PRE_A_TM
    n_i = seq // tm
    steps = bsz * n_i
    row = lambda width: pl.BlockSpec((1, tm, width), lambda b, i: (b, i, 0))
    consts = (lng, lnb, b_gate)
    side_in_specs, side_out_specs, side_shapes = [], [], []
    for w in side_weights:
        _, rows, cols = w.shape
        slab = rows // steps
        assert slab * steps == rows and slab % (2 * SUBLANE) == 0
        side_in_specs.append(pl.BlockSpec((1, slab, cols), lambda b, i: (0, b * n_i + i, 0)))
        side_out_specs.append(pl.BlockSpec((slab, cols), lambda b, i: (b * n_i + i, 0)))
        side_shapes.append(jax.ShapeDtypeStruct((rows, cols), BF16))
    outs = pl.pallas_call(
        _pre_a_kernel,
        grid=(bsz, n_i),
        in_specs=[row(D_MODEL)] + [_const_spec(c.shape) for c in consts]
        + [_const_spec((_PACK_A, D_MODEL), (0, 0))] + side_in_specs,
        out_specs=[row(D_MODEL), row(D_MODEL), row(GLA_DV), row(GLA_DV), row(2 * D_MODEL)]
        + side_out_specs,
        out_shape=[jax.ShapeDtypeStruct((bsz, seq, D_MODEL), F32),
                   jax.ShapeDtypeStruct((bsz, seq, D_MODEL), BF16),
                   jax.ShapeDtypeStruct((bsz, seq, GLA_DV), BF16),
                   jax.ShapeDtypeStruct((bsz, seq, GLA_DV), BF16),
                   jax.ShapeDtypeStruct((bsz, seq, 2 * D_MODEL), BF16)]
        + side_shapes,
        compiler_params=pltpu.CompilerParams(
            dimension_semantics=("parallel", "parallel"), vmem_limit_bytes=VMEM_LIMIT),
        name="pre_a",
    )(x, *consts, w_packed, *side_weights)
    return outs[:5], outs[5:]


def _pre_b_kernel(hb_ref, cos_ref, sin_ref, w_a2_ref,
                  b_a2_ref, qn_g_ref, kvn_g_ref, wq_nope_ref, wq_rope_ref, wkv_k_ref, wkv_v_ref,
                  w_ref, qe_ref, ke_ref, x_ref, qs_ref, kd_ref, dec_ref, q_ref, k_ref, v_ref):
    n = CUMSUM_ROWS
    row_i = lax.broadcasted_iota(jnp.int32, (n, n), 0)
    col_i = lax.broadcasted_iota(jnp.int32, (n, n), 1)
    same_chunk = (lax.shift_right_logical(row_i, _CHUNK_SHIFT)
                  == lax.shift_right_logical(col_i, _CHUNK_SHIFT))
    tri = (same_chunk & (row_i >= col_i)).astype(BF16)
    half = MLA_ROPE // 2
    scale = MLA_QK ** -0.5 * LOG2_E
    sub = hb_ref.shape[1]

    def project(r0):
        hb = hb_ref[0, r0:r0 + sub, :]
        qk = _dot_t(hb, w_ref[:2 * GLA_DK, :])
        lat = _dot_t(hb, w_ref[2 * GLA_DK:2 * GLA_DK + _LAT_W, :])
        slab = lat[:, _SLAB0:]

        qn = (_rms_norm(lat[:, :MLA_Q_RANK], qn_g_ref[...]) * scale).astype(BF16)
        ckv = _rms_norm(lat[:, MLA_Q_RANK:_SLAB0], kvn_g_ref[...]).astype(BF16)
        q_nope = _dot(qn, wq_nope_ref[...])
        q_rope = _dot(qn, wq_rope_ref[...])
        k_nope = _dot(ckv, wkv_k_ref[...])
        v = _dot(ckv, wkv_v_ref[...])
        return qk, slab, q_nope, q_rope, k_nope, v

    def finish(r0, qk, slab, q_nope, q_rope, k_nope, v):
        rows = slice(r0, r0 + sub)
        z = _dot(slab.astype(BF16), w_a2_ref[...]) + b_a2_ref[...]
        la = (jnp.minimum(z, 0.0) * (LOG2_E / GLA_TAU)
              - jnp.log2(1.0 + jnp.exp2(jnp.abs(z) * -LOG2_E)) * (1.0 / GLA_TAU))

        la_hi = la.astype(BF16)
        la_lo = (la - la_hi.astype(F32)).astype(BF16)
        b = jnp.concatenate(
            [_dot(tri, la_hi[c0:c0 + n]) + _dot(tri, la_lo[c0:c0 + n]) for c0 in range(0, sub, n)],
            axis=0)
        c = GLA_CHUNK
        chunk_ends = range(c, sub + 1, c)
        at_row = lambda r: jnp.concatenate(
            [jnp.broadcast_to(b[e - c + r:e - c + r + 1], (c, GLA_DK)) for e in chunk_ends], axis=0)
        second_half = (lax.broadcasted_iota(jnp.int32, (sub, GLA_DK), 0) & (c - 1)) >= c // 2
        b_half = jnp.where(second_half, at_row(3 * c // 4 - 1), at_row(c // 4 - 1))
        b_mid = at_row(c // 2 - 1)
        qs = qk[:, :GLA_DK] * GLA_HK ** -0.5
        kk = qk[:, GLA_DK:]
        qe_ref[0, rows, :] = (qs * jnp.exp2(b - b_half)).astype(BF16)
        ke_ref[0, rows, :] = (kk * jnp.exp2(b_half - b)).astype(BF16)
        x_exp = jnp.where(second_half, b - b_mid, b_mid - b)
        x_ref[0, rows, :] = (jnp.where(second_half, qs, kk) * jnp.exp2(x_exp)).astype(BF16)
        qs_ref[0, rows, :] = (qs * jnp.exp2(b)).astype(BF16)
        kd_ref[0, rows, :] = (kk * jnp.exp2(at_row(c - 1) - b)).astype(BF16)
        for ci, e in enumerate(chunk_ends):
            dec_ref[0, r0 // c + ci] = jnp.exp2(jnp.broadcast_to(b[e - 1:e], (SUBLANE, GLA_DK)))

        cos_r, sin_r = cos_ref[0, :, rows], sin_ref[0, :, rows]
        cs = jnp.concatenate([cos_r, sin_r, cos_r, sin_r], axis=0).T
        odd_group = (lax.broadcasted_iota(jnp.int32, cs.shape, 1) & half) != 0
        cos_t = jnp.where(odd_group, pltpu.roll(cs, half, 1), cs)
        sin_t = jnp.where(odd_group, cs, pltpu.roll(cs, LANE - half, 1))
        k_rope = (slab * cos_t + _rope_rot(slab) * sin_t)[:, :MLA_ROPE].astype(BF16)

        reps = q_rope.shape[-1] // LANE
        q_rope = (q_rope * jnp.concatenate([cos_t] * reps, axis=-1)
                  + _rope_rot(q_rope) * jnp.concatenate([sin_t] * reps, axis=-1))
        for hd in range(MLA_HEADS):
            q_ref[0, hd, rows, :MLA_NOPE] = q_nope[:, hd * MLA_NOPE:(hd + 1) * MLA_NOPE].astype(BF16)
            q_ref[0, hd, rows, MLA_NOPE:] = q_rope[:, hd * MLA_ROPE:(hd + 1) * MLA_ROPE].astype(BF16)
            k_ref[0, hd, rows, :MLA_NOPE] = k_nope[:, hd * MLA_NOPE:(hd + 1) * MLA_NOPE].astype(BF16)
            k_ref[0, hd, rows, MLA_NOPE:] = k_rope
            v_ref[0, hd, rows, :] = v[:, hd * MLA_V:(hd + 1) * MLA_V].astype(BF16)

    finish(0, *project(0))


def _pre_b_call(hb, cos_t, sin_t, w_a2, b_a2, qn_g, kvn_g,
                wq_nope, wq_rope, wkv_k, wkv_v, w_packed):
    bsz, seq, _ = hb.shape
    tm = PRE_B_TM
    assert tm % CUMSUM_ROWS == 0 and CUMSUM_ROWS % GLA_CHUNK == 0
    row = lambda width: pl.BlockSpec((1, tm, width), lambda b, i: (b, i, 0))
    head = lambda width: pl.BlockSpec((1, MLA_HEADS, tm, width), lambda b, i: (b, 0, i, 0))
    consts = (w_a2, b_a2, qn_g, kvn_g, wq_nope, wq_rope, wkv_k, wkv_v)
    out_shape = (
        jax.ShapeDtypeStruct((bsz, seq, GLA_DK), BF16),
        jax.ShapeDtypeStruct((bsz, seq, GLA_DK), BF16),
        jax.ShapeDtypeStruct((bsz, seq, GLA_DK), BF16),
        jax.ShapeDtypeStruct((bsz, seq, GLA_DK), BF16),
        jax.ShapeDtypeStruct((bsz, seq, GLA_DK), BF16),
        jax.ShapeDtypeStruct((bsz, seq // GLA_CHUNK, SUBLANE, GLA_DK), F32),
        jax.ShapeDtypeStruct((bsz, MLA_HEADS, seq, MLA_QK), BF16),
        jax.ShapeDtypeStruct((bsz, MLA_HEADS, seq, MLA_QK), BF16),
        jax.ShapeDtypeStruct((bsz, MLA_HEADS, seq, MLA_V), BF16),
    )
    dec_spec = pl.BlockSpec((1, tm // GLA_CHUNK, SUBLANE, GLA_DK), lambda b, i: (b, i, 0, 0))
    out_specs = (row(GLA_DK),) * 5 + (dec_spec, head(MLA_QK), head(MLA_QK), head(MLA_V))
    return pl.pallas_call(
        _pre_b_kernel,
        grid=(bsz, seq // tm),
        in_specs=[row(D_MODEL)] + [pl.BlockSpec((1, MLA_ROPE // 2, tm), lambda b, i: (b, 0, i))] * 2
        + [_const_spec(c.shape) for c in consts]
        + [_const_spec((_PACK_B, D_MODEL), (_PACK_A // _PACK_B, 0))],
        out_specs=out_specs,
        out_shape=out_shape,
        compiler_params=pltpu.CompilerParams(
            dimension_semantics=("parallel", "parallel"), vmem_limit_bytes=VMEM_LIMIT),
        name="pre_b",
    )(hb, cos_t, sin_t, *consts, w_packed)


def _gla_kernel(qe_ref, ke_ref, x_ref, qs_ref, kd_ref, dec_ref, v_ref, r_ref, g_ref,
                o_ref, st_ref):
    @pl.when(pl.program_id(1) == 0)
    def _():
        st_ref[...] = jnp.zeros_like(st_ref)

    c = GLA_CHUNK
    row = lax.broadcasted_iota(jnp.int32, (c, c), 0)
    col = lax.broadcasted_iota(jnp.int32, (c, c), 1)
    same_half_causal = (row >= col) & ((row >= c // 2) == (col >= c // 2))
    x_is_query = lax.broadcasted_iota(jnp.int32, (c, GLA_HK), 0) >= c // 2

    n_chunks = qe_ref.shape[1] // c
    kcols = [slice(hd * GLA_HK, (hd + 1) * GLA_HK) for hd in range(GLA_HEADS)]
    vcols = [slice(hd * GLA_HV, (hd + 1) * GLA_HV) for hd in range(GLA_HEADS)]

    def intra_scores(ci):
        rows = slice(ci * c, (ci + 1) * c)
        ps = []
        for hd in range(GLA_HEADS):
            s = lax.dot_general(qe_ref[0, rows, kcols[hd]], ke_ref[0, rows, kcols[hd]], _NT,
                                preferred_element_type=F32)
            x = x_ref[0, rows, kcols[hd]]
            zero = jnp.zeros_like(x)
            sx = lax.dot_general(jnp.where(x_is_query, x, zero), jnp.where(x_is_query, zero, x),
                                 _NT, preferred_element_type=F32)
            ps.append((jnp.where(same_half_causal, s, 0.0) + sx).astype(BF16))
        return ps

    def outputs_and_state(ci, ps):
        rows = slice(ci * c, (ci + 1) * c)
        os = []
        for hd in range(GLA_HEADS):
            v = v_ref[0, rows, vcols[hd]]
            st = st_ref[hd]
            os.append(_dot(ps[hd], v) + lax.dot_general(
                qs_ref[0, rows, kcols[hd]], st.astype(BF16), _NT, preferred_element_type=F32))
            st_ref[hd] = st * dec_ref[0, ci, 0:1, kcols[hd]] + lax.dot_general(
                v, kd_ref[0, rows, kcols[hd]], _TN, preferred_element_type=F32)
        return os

    def norm_and_store(ci, os):
        rows = slice(ci * c, (ci + 1) * c)
        for hd in range(GLA_HEADS):
            on = _rms_norm(os[hd], g_ref[:, vcols[hd]])
            o_ref[0, rows, vcols[hd]] = (on * r_ref[0, rows, vcols[hd]].astype(F32)).astype(BF16)

    ps, os = {}, {}
    for t in range(n_chunks + 2):
        if t < n_chunks:
            ps[t] = intra_scores(t)
        if 1 <= t <= n_chunks:
            os[t - 1] = outputs_and_state(t - 1, ps.pop(t - 1))
        if t >= 2:
            norm_and_store(t - 2, os.pop(t - 2))


def _gla_call(qe, ke, xf, qs, kd, dec, vg, rg, g):
    bsz, seq, _ = qe.shape
    tb = GLA_TB
    kspec = pl.BlockSpec((1, tb, GLA_DK), lambda b, i: (b, i, 0))
    vspec = pl.BlockSpec((1, tb, GLA_DV), lambda b, i: (b, i, 0))
    dspec = pl.BlockSpec((1, tb // GLA_CHUNK, SUBLANE, GLA_DK), lambda b, i: (b, i, 0, 0))
    return pl.pallas_call(
        _gla_kernel,
        grid=(bsz, seq // tb),
        in_specs=[kspec] * 5 + [dspec, vspec, vspec, _const_spec(g.shape)],
        out_specs=vspec,
        out_shape=jax.ShapeDtypeStruct((bsz, seq, GLA_DV), BF16),
        scratch_shapes=[pltpu.VMEM((GLA_HEADS, GLA_HV, GLA_HK), F32)],
        compiler_params=pltpu.CompilerParams(
            dimension_semantics=("parallel", "arbitrary"), vmem_limit_bytes=VMEM_LIMIT),
        name="gla",
    )(qe, ke, xf, qs, kd, dec, vg, rg, g)


def _mla_kernel(q_ref, k_ref, v_ref, o_ref, vx_ref):
    tq = MLA_TQ
    seq = v_ref.shape[2]
    ones_col = lax.broadcasted_iota(jnp.int32, (seq, MLA_V), 1) == 0
    for hd in range(MLA_HPS):
        vx_ref[hd, :, :MLA_V] = v_ref[0, hd]
        vx_ref[hd, :, MLA_V:] = jnp.where(ones_col, 1.0, 0.0).astype(BF16)

    row = lax.broadcasted_iota(jnp.int32, (tq, tq), 0)
    col = lax.broadcasted_iota(jnp.int32, (tq, tq), 1)
    causal = row >= col

    def scores(qi, hd):
        q0 = qi * tq
        q = q_ref[0, hd, q0:q0 + tq, :]
        s_diag = lax.dot_general(q, k_ref[0, hd, q0:q0 + tq, :], _NT, preferred_element_type=F32)
        s_diag = jnp.where(causal, s_diag, -jnp.inf)
        if qi == 0:
            return s_diag, None
        return s_diag, lax.dot_general(q, k_ref[0, hd, :q0, :], _NT, preferred_element_type=F32)

    def finish(qi, hd, s_diag, s_off):
        q0 = qi * tq
        m = jnp.max(s_diag, axis=-1, keepdims=True)
        if s_off is not None:
            m = jnp.maximum(m, jnp.max(s_off, axis=-1, keepdims=True))
        acc = _dot(jnp.exp2(s_diag - m).astype(BF16), vx_ref[hd, q0:q0 + tq, :])
        if s_off is not None:
            acc = acc + _dot(jnp.exp2(s_off - m).astype(BF16), vx_ref[hd, :q0, :])
        o_ref[0, q0:q0 + tq, hd * MLA_V:(hd + 1) * MLA_V] = (
            acc[:, :MLA_V] / acc[:, MLA_V:MLA_V + 1]).astype(BF16)

    items = [(qi, hd) for qi in range(seq // tq) for hd in range(MLA_HPS)]
    pending = [scores(*item) for item in items[:MLA_AHEAD]]
    for t, item in enumerate(items):
        if t + MLA_AHEAD < len(items):
            pending.append(scores(*items[t + MLA_AHEAD]))
        finish(*item, *pending.pop(0))


def _mla_call(q, k, v):
    bsz, nh, seq, _ = q.shape
    hps = MLA_HPS
    head = lambda width: pl.BlockSpec((1, hps, seq, width), lambda b, h: (b, h, 0, 0))
    return pl.pallas_call(
        _mla_kernel,
        grid=(bsz, nh // hps),
        in_specs=[head(MLA_QK), head(MLA_QK), head(MLA_V)],
        out_specs=pl.BlockSpec((1, seq, hps * MLA_V), lambda b, h: (b, 0, h)),
        out_shape=jax.ShapeDtypeStruct((bsz, seq, nh * MLA_V), BF16),
        scratch_shapes=[pltpu.VMEM((hps, seq, 2 * MLA_V), BF16)],
        compiler_params=pltpu.CompilerParams(
            dimension_semantics=("parallel", "parallel"), vmem_limit_bytes=VMEM_LIMIT),
        name="mla",
    )(q, k, v)


def _post_kernel(res_ref, a_ref, m_ref, gate_ref, wog_ref, wom_ref, wout_ref,
                 ln1g_ref, ln1b_ref, h1_ref):
    sub = res_ref.shape[0] // POST_SPLIT
    starts = list(range(0, res_ref.shape[0], sub))

    def branch_outputs(r0):
        rows = slice(r0, r0 + sub)
        return _dot(a_ref[rows, :], wog_ref[...]), _dot(m_ref[rows, :], wom_ref[...])

    pending = branch_outputs(starts[0])
    for t, r0 in enumerate(starts):
        rows = slice(r0, r0 + sub)
        y_gla, y_mla = pending
        gate = gate_ref[rows, :].astype(F32)
        merged = gate[:, :D_MODEL] * y_gla + gate[:, D_MODEL:] * y_mla
        mix = _dot(merged.astype(BF16), wout_ref[...])
        if t + 1 < len(starts):
            pending = branch_outputs(starts[t + 1])
        h1_ref[rows, :] = _layer_norm(res_ref[rows, :] + mix, ln1g_ref[...], ln1b_ref[...])


def _post_call(res, act, omla, gate, wog, wom, wout, ln1g, ln1b):
    n = res.shape[0]
    tm = POST_TM
    row = lambda width: pl.BlockSpec((tm, width), lambda i: (i, 0))
    consts = (wog, wom, wout, ln1g, ln1b)
    return pl.pallas_call(
        _post_kernel,
        grid=(n // tm,),
        in_specs=[row(D_MODEL), row(D_MODEL), row(D_MODEL), row(2 * D_MODEL)]
        + [_const_spec(c.shape) for c in consts],
        out_specs=row(D_MODEL),
        out_shape=jax.ShapeDtypeStruct((n, D_MODEL), F32),
        compiler_params=pltpu.CompilerParams(
            dimension_semantics=("parallel",), vmem_limit_bytes=VMEM_LIMIT),
        name="post",
    )(res, act, omla, gate, *consts)


def _ffn_kernel(h1_ref, w1_ref, w2_ref, g_ref, b_ref, o_ref):
    h1 = h1_ref[...]
    hb = h1.astype(BF16)
    acc = jnp.zeros(h1.shape, F32)
    for f0 in range(0, D_FF, FFN_TF):
        a = jnp.maximum(_dot(hb, w1_ref[:, f0:f0 + FFN_TF]), 0.0)
        acc = acc + _dot((a * a).astype(BF16), w2_ref[f0:f0 + FFN_TF, :])
    o_ref[...] = _layer_norm(DEEPNORM_ALPHA * h1 + acc, g_ref[...], b_ref[...])


def _ffn_call(h1, w1, w2, g, b):
    n = h1.shape[0]
    tm = FFN_TM
    row = pl.BlockSpec((tm, D_MODEL), lambda i: (i, 0))
    consts = (w1, w2, g, b)
    return pl.pallas_call(
        _ffn_kernel,
        grid=(n // tm,),
        in_specs=[row] + [_const_spec(c.shape) for c in consts],
        out_specs=row,
        out_shape=jax.ShapeDtypeStruct((n, D_MODEL), F32),
        compiler_params=pltpu.CompilerParams(
            dimension_semantics=("parallel",), vmem_limit_bytes=VMEM_LIMIT),
        name="ffn",
    )(h1, *consts)


def kernel(x, positions, ln_in_g, ln_in_b, w_in, w_gla_a2, b_gla_a2, gla_norm_g, w_o_gla,
           q_a_norm_g, w_q_b, kv_a_norm_g, w_kv_b, w_o_mla, b_gate, w_out,
           ln1_g, ln1_b, w_ff1, w_ff2, ln2_g, ln2_b):
    assert DEPTH == 1 and w_in.shape[0] == 1
    bsz, seq, _ = x.shape
    n = bsz * seq
    row2 = lambda a: a.reshape(1, -1)

    inv_freq = 1.0 / (ROPE_THETA ** (jnp.arange(0, MLA_ROPE, 2, dtype=F32) / MLA_ROPE))
    ang = positions.astype(F32)[:, None, :] * inv_freq[None, :, None]
    cos_t, sin_t = jnp.cos(ang), jnp.sin(ang)

    w_packed, wq_nope, wq_rope, wkv_k, wkv_v = _pack_call(w_in[0].T, w_q_b, w_kv_b)
    slab_pad = LANE - MLA_ROPE - GLA_GATE_RANK
    w_a2 = jnp.pad(w_gla_a2[0], ((_SLAB_ALR, slab_pad), (0, 0))).astype(BF16)

    lng, lnb = row2(ln_in_g), row2(ln_in_b)
    (res, hb, vg, rg, gate), (wog, wom, wout, w1, w2) = _pre_a_call(
        x, lng, lnb, row2(b_gate[0]), w_packed, (w_o_gla, w_o_mla, w_out, w_ff1, w_ff2))
    qe, ke, xf, qs, kd, dec, q, k, v = _pre_b_call(
        hb, cos_t, sin_t, w_a2, row2(b_gla_a2[0]),
        row2(q_a_norm_g[0]), row2(kv_a_norm_g[0]), wq_nope, wq_rope, wkv_k, wkv_v, w_packed)

    act = _gla_call(qe, ke, xf, qs, kd, dec, vg, rg, row2(gla_norm_g[0]))
    omla = _mla_call(q, k, v)

    h1 = _post_call(
        res.reshape(n, D_MODEL), act.reshape(n, GLA_DV), omla.reshape(n, MLA_HEADS * MLA_V),
        gate.reshape(n, 2 * D_MODEL), wog, wom, wout,
        row2(ln1_g[0]), row2(ln1_b[0]))

    out = _ffn_call(h1, w1, w2, row2(ln2_g[0]), row2(ln2_b[0]))
    return out.reshape(bsz, seq, D_MODEL)
```

```python
import jax
import jax.numpy as jnp
from jax import lax
from jax.experimental import pallas as pl
from jax.experimental.pallas import tpu as pltpu

D_MODEL = 1024
DEPTH = 1
LN_EPS = 1e-5
RMS_EPS = 1e-6

GLA_HEADS = 4
GLA_DK = D_MODEL // 2
GLA_DV = D_MODEL
GLA_HK = GLA_DK // GLA_HEADS
GLA_HV = GLA_DV // GLA_HEADS
GLA_GATE_RANK = 16
GLA_TAU = 16.0

MLA_HEADS = 8
MLA_Q_RANK = 384
MLA_KV_RANK = 256
MLA_NOPE = 128
MLA_ROPE = 64
MLA_V = 128
MLA_QK = MLA_NOPE + MLA_ROPE
ROPE_THETA = 10000.0

LOG2_E = 1.4426950408889634

D_FF = 4 * D_MODEL
DEEPNORM_ALPHA = (2.0 * DEPTH) ** 0.25

_OFF_QG = 0
_OFF_KG = _OFF_QG + GLA_DK
_OFF_VG = _OFF_KG + GLA_DK
_OFF_RG = _OFF_VG + GLA_DV
_OFF_ALR = _OFF_RG + GLA_DV
_OFF_QLAT = _OFF_ALR + GLA_GATE_RANK
_OFF_KVLAT = _OFF_QLAT + MLA_Q_RANK
_OFF_KROPE = _OFF_KVLAT + MLA_KV_RANK
_OFF_GATE = _OFF_KROPE + MLA_ROPE
D_IN = _OFF_GATE + 2 * D_MODEL

LANE = 128
SUBLANE = 8
VMEM_LIMIT = 60 * 1024 * 1024

_SLAB0 = MLA_Q_RANK + MLA_KV_RANK
_SLAB_ALR = MLA_ROPE
assert _SLAB0 % LANE == 0 and _SLAB_ALR + GLA_GATE_RANK <= LANE
_LAT_W = _SLAB0 + LANE
_PACK_A = 2 * GLA_DV + 2 * D_MODEL
_PACK_B = _PACK_A // 2
assert 2 * GLA_DK + _LAT_W <= _PACK_B

PACK_COLS = 256
PRE_A_TM = 1024
PRE_B_TM = 1024
CUMSUM_ROWS = 256
GLA_CHUNK = 64
_CHUNK_SHIFT = GLA_CHUNK.bit_length() - 1
assert 1 << _CHUNK_SHIFT == GLA_CHUNK
GLA_TB = 1024
MLA_TQ = 256
MLA_HPS = 4
MLA_AHEAD = 1
POST_TM = 1024
POST_SPLIT = 4
FFN_TM = 1024
FFN_TF = 1024

_NT = (((1,), (1,)), ((), ()))
_TN = (((0,), (0,)), ((), ()))

BF16 = jnp.bfloat16
F32 = jnp.float32


def _dot(a, b):
    return jnp.dot(a, b, preferred_element_type=F32)


def _dot_t(a, b_t):
    return lax.dot_general(a, b_t, _NT, preferred_element_type=F32)


def _layer_norm(x, g, b):
    mu = jnp.mean(x, axis=-1, keepdims=True)
    xc = x - mu
    var = jnp.mean(xc * xc, axis=-1, keepdims=True)
    return xc * lax.rsqrt(var + LN_EPS) * g + b


def _rms_norm(x, g):
    return x * lax.rsqrt(jnp.mean(x * x, axis=-1, keepdims=True) + RMS_EPS) * g


def _sigmoid(x):
    return 0.5 * jnp.tanh(0.5 * x) + 0.5


def _rope_rot(x):
    width = x.shape[-1]
    half = MLA_ROPE // 2
    lane = lax.broadcasted_iota(jnp.int32, x.shape, 1)
    first_half = (lane & (MLA_ROPE - 1)) < half
    ahead = pltpu.roll(x, width - half, 1)
    behind = pltpu.roll(x, half, 1)
    return jnp.where(first_half, -ahead, behind)


def _const_spec(shape, index=None):
    index = (0,) * len(shape) if index is None else index
    return pl.BlockSpec(shape, lambda *_: index, pipeline_mode=pl.Buffered(1))


def _pack_kernel(wt_ref, wq_ref, wkv_ref, o_ref, wq_nope_ref, wq_rope_ref, wkv_k_ref, wkv_v_ref):
    pieces = ((_OFF_VG, _OFF_ALR), (_OFF_GATE, D_IN), (_OFF_QG, _OFF_VG),
              (_OFF_QLAT, _OFF_GATE), (_OFF_ALR, _OFF_QLAT))
    off = 0
    for lo, hi in pieces:
        o_ref[off:off + hi - lo, :] = wt_ref[lo:hi, :].astype(BF16)
        off += hi - lo
    o_ref[off:, :] = jnp.zeros((o_ref.shape[0] - off, o_ref.shape[1]), BF16)

    @pl.when(pl.program_id(0) == 0)
    def _():
        kv = MLA_NOPE + MLA_V
        for hd in range(MLA_HEADS):
            wq_nope_ref[:, hd * MLA_NOPE:(hd + 1) * MLA_NOPE] = (
                wq_ref[0, :, hd * MLA_QK:hd * MLA_QK + MLA_NOPE].astype(BF16))
            wq_rope_ref[:, hd * MLA_ROPE:(hd + 1) * MLA_ROPE] = (
                wq_ref[0, :, hd * MLA_QK + MLA_NOPE:(hd + 1) * MLA_QK].astype(BF16))
            wkv_k_ref[:, hd * MLA_NOPE:(hd + 1) * MLA_NOPE] = (
                wkv_ref[0, :, hd * kv:hd * kv + MLA_NOPE].astype(BF16))
            wkv_v_ref[:, hd * MLA_V:(hd + 1) * MLA_V] = (
                wkv_ref[0, :, hd * kv + MLA_NOPE:(hd + 1) * kv].astype(BF16))


def _pack_call(w_t, w_q_b, w_kv_b):
    cols = PACK_COLS
    whole = lambda shape: pl.BlockSpec(shape, lambda i: (0,) * len(shape))
    return pl.pallas_call(
        _pack_kernel,
        grid=(D_MODEL // cols,),
        in_specs=[pl.BlockSpec((D_IN, cols), lambda i: (0, i)),
                  _const_spec(w_q_b.shape), _const_spec(w_kv_b.shape)],
        out_specs=(pl.BlockSpec((_PACK_A + _PACK_B, cols), lambda i: (0, i)),
                   whole((MLA_Q_RANK, MLA_HEADS * MLA_NOPE)), whole((MLA_Q_RANK, MLA_HEADS * MLA_ROPE)),
                   whole((MLA_KV_RANK, MLA_HEADS * MLA_NOPE)), whole((MLA_KV_RANK, MLA_HEADS * MLA_V))),
        out_shape=(jax.ShapeDtypeStruct((_PACK_A + _PACK_B, D_MODEL), BF16),
                   jax.ShapeDtypeStruct((MLA_Q_RANK, MLA_HEADS * MLA_NOPE), BF16),
                   jax.ShapeDtypeStruct((MLA_Q_RANK, MLA_HEADS * MLA_ROPE), BF16),
                   jax.ShapeDtypeStruct((MLA_KV_RANK, MLA_HEADS * MLA_NOPE), BF16),
                   jax.ShapeDtypeStruct((MLA_KV_RANK, MLA_HEADS * MLA_V), BF16)),
        compiler_params=pltpu.CompilerParams(
            dimension_semantics=("arbitrary",), vmem_limit_bytes=VMEM_LIMIT),
        name="pack",
    )(w_t, w_q_b, w_kv_b)


def _pre_a_kernel(x_ref, lng_ref, lnb_ref, b_gate_ref, w_ref, *refs):
    n_side = (len(refs) - 5) // 2
    side_in, (res_ref, hb_ref, vg_ref, rg_ref, gate_ref), side_out = (
        refs[:n_side], refs[n_side:n_side + 5], refs[n_side + 5:])
    h = _layer_norm(x_ref[0], lng_ref[...], lnb_ref[...])
    res_ref[0] = DEEPNORM_ALPHA * h
    hb = h.astype(BF16)
    hb_ref[0] = hb
    vr = _dot_t(hb, w_ref[:2 * GLA_DV, :])
    vg_ref[0] = vr[:, :GLA_DV].astype(BF16)
    u = 0.5 * vr[:, GLA_DV:]
    rg_ref[0] = (u * (jnp.tanh(u) + 1.0)).astype(BF16)
    gate_ref[0] = _sigmoid(_dot_t(hb, w_ref[2 * GLA_DV:, :]) + b_gate_ref[...]).astype(BF16)
    for src, dst in zip(side_in, side_out):
        dst[...] = src[0].astype(BF16)


def _pre_a_call(x, lng, lnb, b_gate, w_packed, side_weights):
    bsz, seq, _ = x.shape
    tm = PRE_A_TM
    n_i = seq // tm
    steps = bsz * n_i
    row = lambda width: pl.BlockSpec((1, tm, width), lambda b, i: (b, i, 0))
    consts = (lng, lnb, b_gate)
    side_in_specs, side_out_specs, side_shapes = [], [], []
    for w in side_weights:
        _, rows, cols = w.shape
        slab = rows // steps
        assert slab * steps == rows and slab % (2 * SUBLANE) == 0
        side_in_specs.append(pl.BlockSpec((1, slab, cols), lambda b, i: (0, b * n_i + i, 0)))
        side_out_specs.append(pl.BlockSpec((slab, cols), lambda b, i: (b * n_i + i, 0)))
        side_shapes.append(jax.ShapeDtypeStruct((rows, cols), BF16))
    outs = pl.pallas_call(
        _pre_a_kernel,
        grid=(bsz, n_i),
        in_specs=[row(D_MODEL)] + [_const_spec(c.shape) for c in consts]
        + [_const_spec((_PACK_A, D_MODEL), (0, 0))] + side_in_specs,
        out_specs=[row(D_MODEL), row(D_MODEL), row(GLA_DV), row(GLA_DV), row(2 * D_MODEL)]
        + side_out_specs,
        out_shape=[jax.ShapeDtypeStruct((bsz, seq, D_MODEL), F32),
                   jax.ShapeDtypeStruct((bsz, seq, D_MODEL), BF16),
                   jax.ShapeDtypeStruct((bsz, seq, GLA_DV), BF16),
                   jax.ShapeDtypeStruct((bsz, seq, GLA_DV), BF16),
                   jax.ShapeDtypeStruct((bsz, seq, 2 * D_MODEL), BF16)]
        + side_shapes,
        compiler_params=pltpu.CompilerParams(
            dimension_semantics=("parallel", "parallel"), vmem_limit_bytes=VMEM_LIMIT),
        name="pre_a",
    )(x, *consts, w_packed, *side_weights)
    return outs[:5], outs[5:]


def _pre_b_kernel(hb_ref, cos_ref, sin_ref, w_a2_ref,
                  b_a2_ref, qn_g_ref, kvn_g_ref, wq_nope_ref, wq_rope_ref, wkv_k_ref, wkv_v_ref,
                  w_ref, qe_ref, ke_ref, x_ref, qs_ref, kd_ref, dec_ref, q_ref, k_ref, v_ref):
    n = CUMSUM_ROWS
    row_i = lax.broadcasted_iota(jnp.int32, (n, n), 0)
    col_i = lax.broadcasted_iota(jnp.int32, (n, n), 1)
    same_chunk = (lax.shift_right_logical(row_i, _CHUNK_SHIFT)
                  == lax.shift_right_logical(col_i, _CHUNK_SHIFT))
    tri = (same_chunk & (row_i >= col_i)).astype(BF16)
    half = MLA_ROPE // 2
    scale = MLA_QK ** -0.5 * LOG2_E
    sub = hb_ref.shape[1]

    def project(r0):
        hb = hb_ref[0, r0:r0 + sub, :]
        qk = _dot_t(hb, w_ref[:2 * GLA_DK, :])
        lat = _dot_t(hb, w_ref[2 * GLA_DK:2 * GLA_DK + _LAT_W, :])
        slab = lat[:, _SLAB0:]

        qn = (_rms_norm(lat[:, :MLA_Q_RANK], qn_g_ref[...]) * scale).astype(BF16)
        ckv = _rms_norm(lat[:, MLA_Q_RANK:_SLAB0], kvn_g_ref[...]).astype(BF16)
        q_nope = _dot(qn, wq_nope_ref[...])
        q_rope = _dot(qn, wq_rope_ref[...])
        k_nope = _dot(ckv, wkv_k_ref[...])
        v = _dot(ckv, wkv_v_ref[...])
        return qk, slab, q_nope, q_rope, k_nope, v

    def finish(r0, qk, slab, q_nope, q_rope, k_nope, v):
        rows = slice(r0, r0 + sub)
        z = _dot(slab.astype(BF16), w_a2_ref[...]) + b_a2_ref[...]
        la = (jnp.minimum(z, 0.0) * (LOG2_E / GLA_TAU)
              - jnp.log2(1.0 + jnp.exp2(jnp.abs(z) * -LOG2_E)) * (1.0 / GLA_TAU))

        la_hi = la.astype(BF16)
        la_lo = (la - la_hi.astype(F32)).astype(BF16)
        b = jnp.concatenate(
            [_dot(tri, la_hi[c0:c0 + n]) + _dot(tri, la_lo[c0:c0 + n]) for c0 in range(0, sub, n)],
            axis=0)
        c = GLA_CHUNK
        chunk_ends = range(c, sub + 1, c)
        at_row = lambda r: jnp.concatenate(
            [jnp.broadcast_to(b[e - c + r:e - c + r + 1], (c, GLA_DK)) for e in chunk_ends], axis=0)
        second_half = (lax.broadcasted_iota(jnp.int32, (sub, GLA_DK), 0) & (c - 1)) >= c // 2
        b_half = jnp.where(second_half, at_row(3 * c // 4 - 1), at_row(c // 4 - 1))
        b_mid = at_row(c // 2 - 1)
        qs = qk[:, :GLA_DK] * GLA_HK ** -0.5
        kk = qk[:, GLA_DK:]
        qe_ref[0, rows, :] = (qs * jnp.exp2(b - b_half)).astype(BF16)
        ke_ref[0, rows, :] = (kk * jnp.exp2(b_half - b)).astype(BF16)
        x_exp = jnp.where(second_half, b - b_mid, b_mid - b)
        x_ref[0, rows, :] = (jnp.where(second_half, qs, kk) * jnp.exp2(x_exp)).astype(BF16)
        qs_ref[0, rows, :] = (qs * jnp.exp2(b)).astype(BF16)
        kd_ref[0, rows, :] = (kk * jnp.exp2(at_row(c - 1) - b)).astype(BF16)
        for ci, e in enumerate(chunk_ends):
            dec_ref[0, r0 // c + ci] = jnp.exp2(jnp.broadcast_to(b[e - 1:e], (SUBLANE, GLA_DK)))

        cos_r, sin_r = cos_ref[0, :, rows], sin_ref[0, :, rows]
        cs = jnp.concatenate([cos_r, sin_r, cos_r, sin_r], axis=0).T
        odd_group = (lax.broadcasted_iota(jnp.int32, cs.shape, 1) & half) != 0
        cos_t = jnp.where(odd_group, pltpu.roll(cs, half, 1), cs)
        sin_t = jnp.where(odd_group, cs, pltpu.roll(cs, LANE - half, 1))
        k_rope = (slab * cos_t + _rope_rot(slab) * sin_t)[:, :MLA_ROPE].astype(BF16)

        reps = q_rope.shape[-1] // LANE
        q_rope = (q_rope * jnp.concatenate([cos_t] * reps, axis=-1)
                  + _rope_rot(q_rope) * jnp.concatenate([sin_t] * reps, axis=-1))
        for hd in range(MLA_HEADS):
            q_ref[0, hd, rows, :MLA_NOPE] = q_nope[:, hd * MLA_NOPE:(hd + 1) * MLA_NOPE].astype(BF16)
            q_ref[0, hd, rows, MLA_NOPE:] = q_rope[:, hd * MLA_ROPE:(hd + 1) * MLA_ROPE].astype(BF16)
            k_ref[0, hd, rows, :MLA_NOPE] = k_nope[:, hd * MLA_NOPE:(hd + 1) * MLA_NOPE].astype(BF16)
            k_ref[0, hd, rows, MLA_NOPE:] = k_rope
            v_ref[0, hd, rows, :] = v[:, hd * MLA_V:(hd + 1) * MLA_V].astype(BF16)

    finish(0, *project(0))


def _pre_b_call(hb, cos_t, sin_t, w_a2, b_a2, qn_g, kvn_g,
                wq_nope, wq_rope, wkv_k, wkv_v, w_packed):
    bsz, seq, _ = hb.shape
    tm = PRE_B_TM
    assert tm % CUMSUM_ROWS == 0 and CUMSUM_ROWS % GLA_CHUNK == 0
    row = lambda width: pl.BlockSpec((1, tm, width), lambda b, i: (b, i, 0))
    head = lambda width: pl.BlockSpec((1, MLA_HEADS, tm, width), lambda b, i: (b, 0, i, 0))
    consts = (w_a2, b_a2, qn_g, kvn_g, wq_nope, wq_rope, wkv_k, wkv_v)
    out_shape = (
        jax.ShapeDtypeStruct((bsz, seq, GLA_DK), BF16),
        jax.ShapeDtypeStruct((bsz, seq, GLA_DK), BF16),
        jax.ShapeDtypeStruct((bsz, seq, GLA_DK), BF16),
        jax.ShapeDtypeStruct((bsz, seq, GLA_DK), BF16),
        jax.ShapeDtypeStruct((bsz, seq, GLA_DK), BF16),
        jax.ShapeDtypeStruct((bsz, seq // GLA_CHUNK, SUBLANE, GLA_DK), F32),
        jax.ShapeDtypeStruct((bsz, MLA_HEADS, seq, MLA_QK), BF16),
        jax.ShapeDtypeStruct((bsz, MLA_HEADS, seq, MLA_QK), BF16),
        jax.ShapeDtypeStruct((bsz, MLA_HEADS, seq, MLA_V), BF16),
    )
    dec_spec = pl.BlockSpec((1, tm // GLA_CHUNK, SUBLANE, GLA_DK), lambda b, i: (b, i, 0, 0))
    out_specs = (row(GLA_DK),) * 5 + (dec_spec, head(MLA_QK), head(MLA_QK), head(MLA_V))
    return pl.pallas_call(
        _pre_b_kernel,
        grid=(bsz, seq // tm),
        in_specs=[row(D_MODEL)] + [pl.BlockSpec((1, MLA_ROPE // 2, tm), lambda b, i: (b, 0, i))] * 2
        + [_const_spec(c.shape) for c in consts]
        + [_const_spec((_PACK_B, D_MODEL), (_PACK_A // _PACK_B, 0))],
        out_specs=out_specs,
        out_shape=out_shape,
        compiler_params=pltpu.CompilerParams(
            dimension_semantics=("parallel", "parallel"), vmem_limit_bytes=VMEM_LIMIT),
        name="pre_b",
    )(hb, cos_t, sin_t, *consts, w_packed)


def _gla_kernel(qe_ref, ke_ref, x_ref, qs_ref, kd_ref, dec_ref, v_ref, r_ref, g_ref,
                o_ref, st_ref):
    @pl.when(pl.program_id(1) == 0)
    def _():
        st_ref[...] = jnp.zeros_like(st_ref)

    c = GLA_CHUNK
    row = lax.broadcasted_iota(jnp.int32, (c, c), 0)
    col = lax.broadcasted_iota(jnp.int32, (c, c), 1)
    same_half_causal = (row >= col) & ((row >= c // 2) == (col >= c // 2))
    x_is_query = lax.broadcasted_iota(jnp.int32, (c, GLA_HK), 0) >= c // 2

    n_chunks = qe_ref.shape[1] // c
    kcols = [slice(hd * GLA_HK, (hd + 1) * GLA_HK) for hd in range(GLA_HEADS)]
    vcols = [slice(hd * GLA_HV, (hd + 1) * GLA_HV) for hd in range(GLA_HEADS)]

    def intra_scores(ci):
        rows = slice(ci * c, (ci + 1) * c)
        ps = []
        for hd in range(GLA_HEADS):
            s = lax.dot_general(qe_ref[0, rows, kcols[hd]], ke_ref[0, rows, kcols[hd]], _NT,
                                preferred_element_type=F32)
            x = x_ref[0, rows, kcols[hd]]
            zero = jnp.zeros_like(x)
            sx = lax.dot_general(jnp.where(x_is_query, x, zero), jnp.where(x_is_query, zero, x),
                                 _NT, preferred_element_type=F32)
            ps.append((jnp.where(same_half_causal, s, 0.0) + sx).astype(BF16))
        return ps

    def outputs_and_state(ci, ps):
        rows = slice(ci * c, (ci + 1) * c)
        os = []
        for hd in range(GLA_HEADS):
            v = v_ref[0, rows, vcols[hd]]
            st = st_ref[hd]
            os.append(_dot(ps[hd], v) + lax.dot_general(
                qs_ref[0, rows, kcols[hd]], st.astype(BF16), _NT, preferred_element_type=F32))
            st_ref[hd] = st * dec_ref[0, ci, 0:1, kcols[hd]] + lax.dot_general(
                v, kd_ref[0, rows, kcols[hd]], _TN, preferred_element_type=F32)
        return os

    def norm_and_store(ci, os):
        rows = slice(ci * c, (ci + 1) * c)
        for hd in range(GLA_HEADS):
            on = _rms_norm(os[hd], g_ref[:, vcols[hd]])
            o_ref[0, rows, vcols[hd]] = (on * r_ref[0, rows, vcols[hd]].astype(F32)).astype(BF16)

    ps, os = {}, {}
    for t in range(n_chunks + 2):
        if t < n_chunks:
            ps[t] = intra_scores(t)
        if 1 <= t <= n_chunks:
            os[t - 1] = outputs_and_state(t - 1, ps.pop(t - 1))
        if t >= 2:
            norm_and_store(t - 2, os.pop(t - 2))


def _gla_call(qe, ke, xf, qs, kd, dec, vg, rg, g):
    bsz, seq, _ = qe.shape
    tb = GLA_TB
    kspec = pl.BlockSpec((1, tb, GLA_DK), lambda b, i: (b, i, 0))
    vspec = pl.BlockSpec((1, tb, GLA_DV), lambda b, i: (b, i, 0))
    dspec = pl.BlockSpec((1, tb // GLA_CHUNK, SUBLANE, GLA_DK), lambda b, i: (b, i, 0, 0))
    return pl.pallas_call(
        _gla_kernel,
        grid=(bsz, seq // tb),
        in_specs=[kspec] * 5 + [dspec, vspec, vspec, _const_spec(g.shape)],
        out_specs=vspec,
        out_shape=jax.ShapeDtypeStruct((bsz, seq, GLA_DV), BF16),
        scratch_shapes=[pltpu.VMEM((GLA_HEADS, GLA_HV, GLA_HK), F32)],
        compiler_params=pltpu.CompilerParams(
            dimension_semantics=("parallel", "arbitrary"), vmem_limit_bytes=VMEM_LIMIT),
        name="gla",
    )(qe, ke, xf, qs, kd, dec, vg, rg, g)


def _mla_kernel(q_ref, k_ref, v_ref, o_ref, vx_ref):
    tq = MLA_TQ
    seq = v_ref.shape[2]
    ones_col = lax.broadcasted_iota(jnp.int32, (seq, MLA_V), 1) == 0
    for hd in range(MLA_HPS):
        vx_ref[hd, :, :MLA_V] = v_ref[0, hd]
        vx_ref[hd, :, MLA_V:] = jnp.where(ones_col, 1.0, 0.0).astype(BF16)

    row = lax.broadcasted_iota(jnp.int32, (tq, tq), 0)
    col = lax.broadcasted_iota(jnp.int32, (tq, tq), 1)
    causal = row >= col

    def scores(qi, hd):
        q0 = qi * tq
        q = q_ref[0, hd, q0:q0 + tq, :]
        s_diag = lax.dot_general(q, k_ref[0, hd, q0:q0 + tq, :], _NT, preferred_element_type=F32)
        s_diag = jnp.where(causal, s_diag, -jnp.inf)
        if qi == 0:
            return s_diag, None
        return s_diag, lax.dot_general(q, k_ref[0, hd, :q0, :], _NT, preferred_element_type=F32)

    def finish(qi, hd, s_diag, s_off):
        q0 = qi * tq
        m = jnp.max(s_diag, axis=-1, keepdims=True)
        if s_off is not None:
            m = jnp.maximum(m, jnp.max(s_off, axis=-1, keepdims=True))
        acc = _dot(jnp.exp2(s_diag - m).astype(BF16), vx_ref[hd, q0:q0 + tq, :])
        if s_off is not None:
            acc = acc + _dot(jnp.exp2(s_off - m).astype(BF16), vx_ref[hd, :q0, :])
        o_ref[0, q0:q0 + tq, hd * MLA_V:(hd + 1) * MLA_V] = (
            acc[:, :MLA_V] / acc[:, MLA_V:MLA_V + 1]).astype(BF16)

    nq = seq // tq
    long_first = [(qi, hd) for qi in range(nq - 1, nq // 2 - 1, -1) for hd in range(MLA_HPS)]
    short_first = [(qi, hd) for qi in range(nq // 2) for hd in range(MLA_HPS)]
    items = [item for pair in zip(long_first, short_first) for item in pair]
    pending = [scores(*item) for item in items[:MLA_AHEAD]]
    for t, item in enumerate(items):
        if t + MLA_AHEAD < len(items):
            pending.append(scores(*items[t + MLA_AHEAD]))
        finish(*item, *pending.pop(0))


def _mla_call(q, k, v):
    bsz, nh, seq, _ = q.shape
    hps = MLA_HPS
    head = lambda width: pl.BlockSpec((1, hps, seq, width), lambda b, h: (b, h, 0, 0))
    return pl.pallas_call(
        _mla_kernel,
        grid=(bsz, nh // hps),
        in_specs=[head(MLA_QK), head(MLA_QK), head(MLA_V)],
        out_specs=pl.BlockSpec((1, seq, hps * MLA_V), lambda b, h: (b, 0, h)),
        out_shape=jax.ShapeDtypeStruct((bsz, seq, nh * MLA_V), BF16),
        scratch_shapes=[pltpu.VMEM((hps, seq, 2 * MLA_V), BF16)],
        compiler_params=pltpu.CompilerParams(
            dimension_semantics=("parallel", "parallel"), vmem_limit_bytes=VMEM_LIMIT),
        name="mla",
    )(q, k, v)


def _post_kernel(res_ref, a_ref, m_ref, gate_ref, wog_ref, wom_ref, wout_ref,
                 ln1g_ref, ln1b_ref, h1_ref):
    sub = res_ref.shape[0] // POST_SPLIT
    starts = list(range(0, res_ref.shape[0], sub))

    def branch_outputs(r0):
        rows = slice(r0, r0 + sub)
        return _dot(a_ref[rows, :], wog_ref[...]), _dot(m_ref[rows, :], wom_ref[...])

    pending = branch_outputs(starts[0])
    for t, r0 in enumerate(starts):
        rows = slice(r0, r0 + sub)
        y_gla, y_mla = pending
        gate = gate_ref[rows, :].astype(F32)
        merged = gate[:, :D_MODEL] * y_gla + gate[:, D_MODEL:] * y_mla
        mix = _dot(merged.astype(BF16), wout_ref[...])
        if t + 1 < len(starts):
            pending = branch_outputs(starts[t + 1])
        h1_ref[rows, :] = _layer_norm(res_ref[rows, :] + mix, ln1g_ref[...], ln1b_ref[...])


def _post_call(res, act, omla, gate, wog, wom, wout, ln1g, ln1b):
    n = res.shape[0]
    tm = POST_TM
    row = lambda width: pl.BlockSpec((tm, width), lambda i: (i, 0))
    consts = (wog, wom, wout, ln1g, ln1b)
    return pl.pallas_call(
        _post_kernel,
        grid=(n // tm,),
        in_specs=[row(D_MODEL), row(D_MODEL), row(D_MODEL), row(2 * D_MODEL)]
        + [_const_spec(c.shape) for c in consts],
        out_specs=row(D_MODEL),
        out_shape=jax.ShapeDtypeStruct((n, D_MODEL), F32),
        compiler_params=pltpu.CompilerParams(
            dimension_semantics=("parallel",), vmem_limit_bytes=VMEM_LIMIT),
        name="post",
    )(res, act, omla, gate, *consts)


def _ffn_kernel(h1_ref, w1_ref, w2_ref, g_ref, b_ref, o_ref):
    h1 = h1_ref[...]
    hb = h1.astype(BF16)
    acc = jnp.zeros(h1.shape, F32)
    for f0 in range(0, D_FF, FFN_TF):
        a = jnp.maximum(_dot(hb, w1_ref[:, f0:f0 + FFN_TF]), 0.0)
        acc = acc + _dot((a * a).astype(BF16), w2_ref[f0:f0 + FFN_TF, :])
    o_ref[...] = _layer_norm(DEEPNORM_ALPHA * h1 + acc, g_ref[...], b_ref[...])


def _ffn_call(h1, w1, w2, g, b):
    n = h1.shape[0]
    tm = FFN_TM
    row = pl.BlockSpec((tm, D_MODEL), lambda i: (i, 0))
    consts = (w1, w2, g, b)
    return pl.pallas_call(
        _ffn_kernel,
        grid=(n // tm,),
        in_specs=[row] + [_const_spec(c.shape) for c in consts],
        out_specs=row,
        out_shape=jax.ShapeDtypeStruct((n, D_MODEL), F32),
        compiler_params=pltpu.CompilerParams(
            dimension_semantics=("parallel",), vmem_limit_bytes=VMEM_LIMIT),
        name="ffn",
    )(h1, *consts)


def kernel(x, positions, ln_in_g, ln_in_b, w_in, w_gla_a2, b_gla_a2, gla_norm_g, w_o_gla,
           q_a_norm_g, w_q_b, kv_a_norm_g, w_kv_b, w_o_mla, b_gate, w_out,
           ln1_g, ln1_b, w_ff1, w_ff2, ln2_g, ln2_b):
    assert DEPTH == 1 and w_in.shape[0] == 1
    bsz, seq, _ = x.shape
    n = bsz * seq
    row2 = lambda a: a.reshape(1, -1)

    inv_freq = 1.0 / (ROPE_THETA ** (jnp.arange(0, MLA_ROPE, 2, dtype=F32) / MLA_ROPE))
    ang = positions.astype(F32)[:, None, :] * inv_freq[None, :, None]
    cos_t, sin_t = jnp.cos(ang), jnp.sin(ang)

    w_packed, wq_nope, wq_rope, wkv_k, wkv_v = _pack_call(w_in[0].T, w_q_b, w_kv_b)
    slab_pad = LANE - MLA_ROPE - GLA_GATE_RANK
    w_a2 = jnp.pad(w_gla_a2[0], ((_SLAB_ALR, slab_pad), (0, 0))).astype(BF16)

    lng, lnb = row2(ln_in_g), row2(ln_in_b)
    (res, hb, vg, rg, gate), (wog, wom, wout, w1, w2) = _pre_a_call(
        x, lng, lnb, row2(b_gate[0]), w_packed, (w_o_gla, w_o_mla, w_out, w_ff1, w_ff2))
    qe, ke, xf, qs, kd, dec, q, k, v = _pre_b_call(
        hb, cos_t, sin_t, w_a2, row2(b_gla_a2[0]),
        row2(q_a_norm_g[0]), row2(kv_a_norm_g[0]), wq_nope, wq_rope, wkv_k, wkv_v, w_packed)

    act = _gla_call(qe, ke, xf, qs, kd, dec, vg, rg, row2(gla_norm_g[0]))
    omla = _mla_call(q, k, v)

    h1 = _post_call(
        res.reshape(n, D_MODEL), act.reshape(n, GLA_DV), omla.reshape(n, MLA_HEADS * MLA_V),
        gate.reshape(n, 2 * D_MODEL), wog, wom, wout,
        row2(ln1_g[0]), row2(ln1_b[0]))

    out = _ffn_call(h1, w1, w2, row2(ln2_g[0]), row2(ln2_b[0]))
    return out.reshape(bsz, seq, D_MODEL)
```
